```python
import jax, jax.numpy as jnp
from jax import lax
import numpy as np

D_MODEL = 2048
BATCH = 1
SEQ = 16384
DEPTH = 2

N_A = DEPTH // 2
N_B = DEPTH - N_A

MIX = D_MODEL
MEM_LEN = 256
MEM_HEADS = 4
MEM_W = MIX // 4
MEM_HD = MEM_W // MEM_HEADS
MAIN_W = MIX - MEM_W

ML_HEADS = 4
ML_DV = MAIN_W // ML_HEADS
ML_DQK = ML_DV // 2
ML_CHUNK = 64

NSA_HEADS = 12
NSA_HD = MAIN_W // NSA_HEADS
NSA_KV = 2
NSA_GROUP = NSA_HEADS // NSA_KV
CMP_LEN = 32
CMP_STRIDE = 16
CMP_HIDDEN = 256
SLC_LEN = 64
SLC_TOPK = 16
WIN = 512
Q_BLOCK = 128

D_FF = 5632
FFN_RES = 0.5
ROPE_THETA = 10000.0
EPS = 1e-6
NEG = -1e30
BIG = 1e30

A_IN = 2 * ML_HEADS * ML_DQK + 2 * MAIN_W + 2 * ML_HEADS + MEM_W
B_IN = MAIN_W + 3 * NSA_HEADS + MEM_W
KV_W = 6 * NSA_KV * NSA_HD

kernel_name = "yoco_mlstm_nsa_macaron_hybrid"


def rmsnorm(x, g):
    xf = x.astype(jnp.float32)
    y = xf * lax.rsqrt(jnp.mean(xf * xf, axis=-1, keepdims=True) + EPS)
    return (y * g.astype(jnp.float32)).astype(x.dtype)


def swiglu(x, w_gate, w_up, w_down):
    return (jax.nn.silu(x @ w_gate) * (x @ w_up)) @ w_down


def half_ffn(x, g_pre, g_post, w_gate, w_up, w_down):
    return x + FFN_RES * rmsnorm(swiglu(rmsnorm(x, g_pre), w_gate, w_up, w_down), g_post)


def split_cols(a, sizes):
    return jnp.split(a, [int(o) for o in np.cumsum(sizes)[:-1]], axis=-1)


def rope_tables(seq, dim):
    inv = ROPE_THETA ** (-jnp.arange(0, dim, 2, dtype=jnp.float32) / dim)
    ang = jnp.arange(seq, dtype=jnp.float32)[:, None] * inv[None, :]
    return jnp.cos(ang), jnp.sin(ang)


def apply_rope(x, cos, sin):
    half = x.shape[-1] // 2
    x1, x2 = x[..., :half], x[..., half:]
    c = cos[None, :, None, :].astype(x.dtype)
    s = sin[None, :, None, :].astype(x.dtype)
    return jnp.concatenate([x1 * c - x2 * s, x2 * c + x1 * s], axis=-1)


def mem_attention(mq, mem_k, mem_v):
    s = jnp.einsum('bshd,bmhd->bhsm', mq, mem_k).astype(jnp.float32) * MEM_HD ** -0.5
    p = jax.nn.softmax(s, axis=-1).astype(mem_v.dtype)
    o = jnp.einsum('bhsm,bmhd->bshd', p, mem_v)
    return o.reshape(o.shape[0], o.shape[1], MEM_W)


def mlstm_chunkwise(q, k, v, i_pre, log_f):
    b_, h_, s_, dqk = q.shape
    dv = v.shape[-1]
    nc = s_ // ML_CHUNK

    def chunks(a):
        return jnp.moveaxis(a.reshape((b_, h_, nc, ML_CHUNK) + a.shape[3:]), 2, 0)

    causal = jnp.tril(jnp.ones((ML_CHUNK, ML_CHUNK), dtype=bool))

    def step(carry, inp):
        c_prev, n_prev, m_prev = carry
        qc, kc, vc, ic, fc = inp
        b = jnp.cumsum(fc, axis=-1)
        dmat = b[..., :, None] - b[..., None, :] + ic[..., None, :]
        dmat = jnp.where(causal, dmat, -jnp.inf)
        m_inter = b + m_prev[..., None]
        m_vec = jnp.maximum(jnp.max(dmat, axis=-1), m_inter)
        smat = jnp.einsum('bhld,bhsd->bhls', qc, kc) * jnp.exp(dmat - m_vec[..., None])
        inter = jnp.exp(m_inter - m_vec)
        num = jnp.einsum('bhls,bhsv->bhlv', smat, vc) + inter[..., None] * jnp.einsum('bhld,bhdv->bhlv', qc, c_prev)
        den = jnp.sum(smat, axis=-1) + inter * jnp.einsum('bhld,bhd->bhl', qc, n_prev)
        h = num / jnp.maximum(jnp.abs(den), jnp.exp(-m_vec))[..., None]
        m_last = m_vec[..., -1]
        w = jnp.exp(b[..., -1:] - b + ic - m_last[..., None])
        decay = jnp.exp(b[..., -1] + m_prev - m_last)
        c_new = decay[..., None, None] * c_prev + jnp.einsum('bhs,bhsd,bhsv->bhdv', w, kc, vc)
        n_new = decay[..., None] * n_prev + jnp.einsum('bhs,bhsd->bhd', w, kc)
        return (c_new, n_new, m_last), h

    f32 = jnp.float32
    init = (jnp.zeros((b_, h_, dqk, dv), f32), jnp.zeros((b_, h_, dqk), f32), jnp.zeros((b_, h_), f32))
    _, hs = lax.scan(step, init, (chunks(q), chunks(k), chunks(v), chunks(i_pre), chunks(log_f)))
    return jnp.moveaxis(hs, 0, 2).reshape(b_, h_, s_, dv)


def mlstm_mixer(xn, w_in, gate_bias, head_norm, mem_k, mem_v):
    b_, s_, _ = xn.shape
    q, k, v, o, gates, mq = split_cols(xn @ w_in, [ML_HEADS * ML_DQK, ML_HEADS * ML_DQK, MAIN_W, MAIN_W,
                                                  2 * ML_HEADS, MEM_W])
    f32 = jnp.float32

    def heads_first(a, d):
        return a.reshape(b_, s_, ML_HEADS, d).transpose(0, 2, 1, 3).astype(f32)

    qh = heads_first(q, ML_DQK)
    kh = heads_first(k, ML_DQK) * ML_DQK ** -0.5
    vh = heads_first(v, ML_DV)
    gp = (gates + gate_bias).astype(f32).reshape(b_, s_, 2, ML_HEADS).transpose(2, 0, 3, 1)
    h = mlstm_chunkwise(qh, kh, vh, gp[0], jax.nn.log_sigmoid(gp[1]))
    h = h.transpose(0, 2, 1, 3).astype(xn.dtype)
    h = rmsnorm(h, head_norm) * jax.nn.sigmoid(o).reshape(b_, s_, ML_HEADS, ML_DV)
    mo = mem_attention(mq.reshape(b_, s_, MEM_HEADS, MEM_HD), mem_k, mem_v)
    return jnp.concatenate([h.reshape(b_, s_, MAIN_W), mo], axis=-1)


def compress_blocks(kv, pos, w1, w2):
    b_, s_, g_, d_ = kv.shape
    n_seg = s_ // CMP_STRIDE
    per = CMP_LEN // CMP_STRIDE
    n_cmp = n_seg - per + 1
    segs = kv.reshape(b_, n_seg, CMP_STRIDE, g_, d_)
    blocks = jnp.concatenate([segs[:, i:i + n_cmp] for i in range(per)], axis=2)
    blocks = blocks + pos[None, None, :, None, :]
    flat = jnp.transpose(blocks, (0, 1, 3, 2, 4)).reshape(b_, n_cmp, g_, CMP_LEN * d_)
    return jax.nn.gelu(flat @ w1) @ w2


def shared_kv(x, kv_norm, w_kv, cmp_pos, cmp_w1, cmp_w2, cos, sin):
    b_, s_, _ = x.shape
    kv = (rmsnorm(x, kv_norm) @ w_kv).reshape(b_, s_, 6, NSA_KV, NSA_HD)
    k_cmp = compress_blocks(kv[:, :, 0], cmp_pos[0], cmp_w1[0], cmp_w2[0])
    v_cmp = compress_blocks(kv[:, :, 1], cmp_pos[1], cmp_w1[1], cmp_w2[1])
    n_sel = s_ // SLC_LEN

    def sel_blocks(a):
        return a.reshape(b_, n_sel, SLC_LEN, NSA_KV, NSA_HD).transpose(0, 3, 1, 2, 4)

    k_slc = sel_blocks(apply_rope(kv[:, :, 2], cos, sin))
    v_slc = sel_blocks(kv[:, :, 3])
    pad = ((0, 0), (WIN, 0), (0, 0), (0, 0))
    k_win = jnp.pad(apply_rope(kv[:, :, 4], cos, sin), pad)
    v_win = jnp.pad(kv[:, :, 5], pad)
    return (k_cmp, v_cmp, k_slc, v_slc, k_win, v_win)


def selection_scores(imp, n_sel):
    r = SLC_LEN // CMP_STRIDE
    c = CMP_LEN // CMP_STRIDE
    n_cmp = imp.shape[-1]
    padded = jnp.pad(imp, [(0, 0)] * (imp.ndim - 1) + [(c - 1, r * n_sel - n_cmp)])
    out = None
    for m in range(r):
        for n in range(c):
            start = m - n + c - 1
            term = lax.slice_in_dim(padded, start, start + r * (n_sel - 1) + 1, stride=r, axis=-1)
            out = term if out is None else out + term
    return out


def nsa_attention(q, q_rot, gates, k_cmp, v_cmp, k_slc, v_slc, k_win, v_win):
    b_, s_, g_, j_, d_ = q.shape
    n_cmp = k_cmp.shape[1]
    n_sel = k_slc.shape[2]
    top = min(SLC_TOPK, n_sel)
    scale = d_ ** -0.5
    cmp_end = jnp.arange(n_cmp) * CMP_STRIDE + (CMP_LEN - 1)
    blk = jnp.arange(n_sel)
    bi = jnp.arange(b_)[:, None, None, None]
    gi = jnp.arange(g_)[None, :, None, None]

    def block(qi):
        t0 = qi * Q_BLOCK
        t = t0 + jnp.arange(Q_BLOCK)
        qb = lax.dynamic_slice_in_dim(q, t0, Q_BLOCK, axis=1)
        qr = lax.dynamic_slice_in_dim(q_rot, t0, Q_BLOCK, axis=1)
        gb = lax.dynamic_slice_in_dim(gates, t0, Q_BLOCK, axis=1)
        valid_c = cmp_end[None, :] <= t[:, None]
        s_c = jnp.einsum('bqgjd,bngd->bgjqn', qb, k_cmp).astype(jnp.float32) * scale
        p_c = jnp.where(valid_c, jax.nn.softmax(jnp.where(valid_c, s_c, NEG), axis=-1), 0.0)
        o_cmp = jnp.einsum('bgjqn,bngd->bqgjd', p_c.astype(v_cmp.dtype), v_cmp)
        p_slc = selection_scores(jnp.sum(p_c, axis=2), n_sel)
        cur = t // SLC_LEN
        forced = (blk[None, :] == 0) | (blk[None, :] == cur[:, None]) | (blk[None, :] == cur[:, None] - 1)
        score = jnp.where(blk[None, :] > cur[:, None], NEG, jnp.where(forced, BIG, p_slc))
        _, idx = lax.top_k(score, top)
        ks = k_slc[bi, gi, idx]
        vs = v_slc[bi, gi, idx]
        kpos = idx[..., None] * SLC_LEN + jnp.arange(SLC_LEN)
        valid_s = kpos <= t[None, None, :, None, None]
        s_s = jnp.einsum('bqgjd,bgqnld->bgjqnl', qr, ks).astype(jnp.float32) * scale
        s_s = jnp.where(valid_s[:, :, None], s_s, NEG).reshape(b_, g_, j_, Q_BLOCK, top * SLC_LEN)
        p_s = jax.nn.softmax(s_s, axis=-1).reshape(b_, g_, j_, Q_BLOCK, top, SLC_LEN)
        o_slc = jnp.einsum('bgjqnl,bgqnld->bqgjd', p_s.astype(vs.dtype), vs)
        kw = lax.dynamic_slice_in_dim(k_win, t0, Q_BLOCK + WIN, axis=1)
        vw = lax.dynamic_slice_in_dim(v_win, t0, Q_BLOCK + WIN, axis=1)
        spos = t0 - WIN + jnp.arange(Q_BLOCK + WIN)
        valid_w = (spos[None, :] <= t[:, None]) & (spos[None, :] > t[:, None] - WIN) & (spos[None, :] >= 0)
        s_w = jnp.einsum('bqgjd,bkgd->bgjqk', qr, kw).astype(jnp.float32) * scale
        p_w = jax.nn.softmax(jnp.where(valid_w, s_w, NEG), axis=-1)
        o_win = jnp.einsum('bgjqk,bkgd->bqgjd', p_w.astype(vw.dtype), vw)
        return gb[..., 0:1] * o_cmp + gb[..., 1:2] * o_slc + gb[..., 2:3] * o_win

    out = lax.map(block, jnp.arange(s_ // Q_BLOCK))
    return jnp.moveaxis(out, 0, 1).reshape(b_, s_, g_ * j_ * d_)


def nsa_mixer(xn, w_in, gate_bias, shared, cos, sin, mem_k, mem_v):
    b_, s_, _ = xn.shape
    k_cmp, v_cmp, k_slc, v_slc, k_win, v_win = shared
    q, gl, mq = split_cols(xn @ w_in, [MAIN_W, 3 * NSA_HEADS, MEM_W])
    q = q.reshape(b_, s_, NSA_HEADS, NSA_HD)
    q_rot = apply_rope(q, cos, sin)
    gates = jax.nn.sigmoid(gl + gate_bias).reshape(b_, s_, NSA_KV, NSA_GROUP, 3)
    shp = (b_, s_, NSA_KV, NSA_GROUP, NSA_HD)
    o = nsa_attention(q.reshape(shp), q_rot.reshape(shp), gates, k_cmp, v_cmp, k_slc, v_slc, k_win, v_win)
    mo = mem_attention(mq.reshape(b_, s_, MEM_HEADS, MEM_HD), mem_k, mem_v)
    return jnp.concatenate([o, mo], axis=-1)


def setup_inputs(seed: int = 0) -> dict:
    key = jax.random.key(seed)
    ks = jax.random.split(key, 24)
    f32 = jnp.float32

    def nrm(k, shape, scale):
        return scale * jax.random.normal(k, shape, f32)

    def gain(k, shape):
        return 1.0 + 0.02 * jax.random.normal(k, shape, f32)

    ig_bias = nrm(ks[10], (N_A, ML_HEADS), 0.1)
    fg_bias = jnp.linspace(3.0, 6.0, ML_HEADS, dtype=f32)[None, :] + nrm(ks[11], (N_A, ML_HEADS), 0.1)
    return {
        "x": nrm(ks[0], (BATCH, SEQ, D_MODEL), 1.0),
        "mem": nrm(ks[1], (BATCH, MEM_LEN, D_MODEL), 1.0),
        "norm_gains": gain(ks[2], (DEPTH, 6, D_MODEL)),
        "ffn_w_gate": nrm(ks[3], (DEPTH, 2, D_MODEL, D_FF), D_MODEL ** -0.5),
        "ffn_w_up": nrm(ks[4], (DEPTH, 2, D_MODEL, D_FF), D_MODEL ** -0.5),
        "ffn_w_down": nrm(ks[5], (DEPTH, 2, D_FF, D_MODEL), D_FF ** -0.5),
        "mem_norm": gain(ks[6], (DEPTH, D_MODEL)),
        "mem_w_kv": nrm(ks[7], (DEPTH, D_MODEL, 2 * MEM_W), D_MODEL ** -0.5),
        "w_out": nrm(ks[8], (DEPTH, MIX, D_MODEL), MIX ** -0.5),
        "a_w_in": nrm(ks[9], (N_A, D_MODEL, A_IN), D_MODEL ** -0.5),
        "a_gate_bias": jnp.concatenate([ig_bias, fg_bias], axis=-1),
        "a_head_norm": gain(ks[12], (N_A, ML_HEADS, ML_DV)),
        "kv_norm": gain(ks[13], (D_MODEL,)),
        "w_kv": nrm(ks[14], (D_MODEL, KV_W), D_MODEL ** -0.5),
        "cmp_pos": nrm(ks[15], (2, CMP_LEN, NSA_HD), 0.02),
        "cmp_w1": nrm(ks[16], (2, CMP_LEN * NSA_HD, CMP_HIDDEN), (CMP_LEN * NSA_HD) ** -0.5),
        "cmp_w2": nrm(ks[17], (2, CMP_HIDDEN, NSA_HD), CMP_HIDDEN ** -0.5),
        "b_w_in": nrm(ks[18], (N_B, D_MODEL, B_IN), D_MODEL ** -0.5),
        "b_gate_bias": nrm(ks[19], (N_B, 3 * NSA_HEADS), 0.1),
    }


def reference(x, mem, norm_gains, ffn_w_gate, ffn_w_up, ffn_w_down, mem_norm, mem_w_kv, w_out,
              a_w_in, a_gate_bias, a_head_norm, kv_norm, w_kv, cmp_pos, cmp_w1, cmp_w2,
              b_w_in, b_gate_bias):
    b_, s_, _ = x.shape
    m_len = mem.shape[1]
    cos, sin = rope_tables(s_, NSA_HD)
    shared = None
    for layer in range(DEPTH):
        if layer == N_A:
            shared = shared_kv(x, kv_norm, w_kv, cmp_pos, cmp_w1, cmp_w2, cos, sin)
        g = norm_gains[layer]
        x = half_ffn(x, g[0], g[1], ffn_w_gate[layer, 0], ffn_w_up[layer, 0], ffn_w_down[layer, 0])
        xn = rmsnorm(x, g[2])
        mem_k, mem_v = jnp.split((rmsnorm(mem, mem_norm[layer]) @ mem_w_kv[layer]).reshape(
            b_, m_len, 2, MEM_HEADS, MEM_HD), 2, axis=2)
        mem_k, mem_v = mem_k[:, :, 0], mem_v[:, :, 0]
        if layer < N_A:
            heads = mlstm_mixer(xn, a_w_in[layer], a_gate_bias[layer], a_head_norm[layer], mem_k, mem_v)
        else:
            lb = layer - N_A
            heads = nsa_mixer(xn, b_w_in[lb], b_gate_bias[lb], shared, cos, sin, mem_k, mem_v)
        x = x + rmsnorm(heads @ w_out[layer], g[3])
        x = half_ffn(x, g[4], g[5], ffn_w_gate[layer, 1], ffn_w_up[layer, 1], ffn_w_down[layer, 1])
    return x
```

```python
import functools

import jax
import jax.numpy as jnp
from jax import lax
from jax.experimental import pallas as pl
from jax.experimental.pallas import tpu as pltpu

F32 = jnp.float32
BF16 = jnp.bfloat16

MEM_HEADS = 4
MEM_HD = 128
MEM_W = MEM_HEADS * MEM_HD
ML_HEADS = 4
ML_DV = 384
ML_DQK = 192
MAIN_W = ML_HEADS * ML_DV
NSA_HEADS = 12
NSA_HD = 128
NSA_KV = 2
NSA_GROUP = NSA_HEADS // NSA_KV
GROUP_W = NSA_GROUP * NSA_HD
CMP_LEN = 32
CMP_STRIDE = 16
SLC_LEN = 64
SLC_TOPK = 16
WIN = 512
Q_BLOCK = 128
FFN_RES = 0.5
ROPE_THETA = 10000.0
EPS = 1e-6
NEG = -1e30
BIG = 1e30
REMOVED = -3e38

LANES = 128
VMEM_LIMIT_BYTES = 56 * 1024 * 1024

ML_CHUNK = 256
SLC_TILE = 512


def _params(*semantics):
    return pltpu.CompilerParams(dimension_semantics=semantics, vmem_limit_bytes=VMEM_LIMIT_BYTES)


def _rms(x, gain):
    return x * lax.rsqrt(jnp.mean(x * x, axis=-1, keepdims=True) + EPS) * gain


def _dot(a, b):
    return jnp.dot(a, b, preferred_element_type=F32)


def _exact_dot(a, b01):
    a1 = a.astype(BF16)
    r1 = a - a1.astype(F32)
    a2 = r1.astype(BF16)
    a3 = (r1 - a2.astype(F32)).astype(BF16)
    return _dot(a1, b01) + _dot(a2, b01) + _dot(a3, b01)


def _exact_dot_left(b01, a):
    a1 = a.astype(BF16)
    r1 = a - a1.astype(F32)
    a2 = r1.astype(BF16)
    a3 = (r1 - a2.astype(F32)).astype(BF16)
    return _dot(b01, a1) + _dot(b01, a2) + _dot(b01, a3)


def _rms_matmul_kernel(x_ref, g_ref, w_ref, o_ref, xn_ref):
    @pl.when(pl.program_id(1) == 0)
    def _():
        xn_ref[...] = _rms(x_ref[...], g_ref[...]).astype(BF16)

    o_ref[...] = _dot(xn_ref[...], w_ref[...]).astype(o_ref.dtype)


def rms_matmul(x, gain, w, *, tm, tn, out_dtype=F32):
    s, k = x.shape
    n = w.shape[1]
    return pl.pallas_call(
        _rms_matmul_kernel,
        grid=(s // tm, n // tn),
        in_specs=[
            pl.BlockSpec((tm, k), lambda i, j: (i, 0)),
            pl.BlockSpec((1, k), lambda i, j: (0, 0)),
            pl.BlockSpec((k, tn), lambda i, j: (0, j)),
        ],
        out_specs=pl.BlockSpec((tm, tn), lambda i, j: (i, j)),
        out_shape=jax.ShapeDtypeStruct((s, n), out_dtype),
        scratch_shapes=[pltpu.VMEM((tm, k), BF16)],
        compiler_params=_params("parallel", "arbitrary"),
        name="rms_matmul",
    )(x, gain.reshape(1, k), w)


def _ffn_kernel(x_ref, gpre_ref, gpost_ref, wg_ref, wu_ref, wd_ref, o_ref, xn_ref, acc_ref):
    j = pl.program_id(1)

    @pl.when(j == 0)
    def _():
        xn_ref[...] = _rms(x_ref[...], gpre_ref[...]).astype(BF16)
        acc_ref[...] = jnp.zeros_like(acc_ref)

    xn = xn_ref[...]
    gate = _dot(xn, wg_ref[...])
    up = _dot(xn, wu_ref[...])
    hidden = (gate * jax.nn.sigmoid(gate) * up).astype(BF16)
    acc_ref[...] += _dot(hidden, wd_ref[...])

    @pl.when(j == pl.num_programs(1) - 1)
    def _():
        o_ref[...] = x_ref[...] + FFN_RES * _rms(acc_ref[...], gpost_ref[...])


def half_ffn(x, g_pre, g_post, w_gate, w_up, w_down, *, tm, tf):
    s, d = x.shape
    d_ff = w_gate.shape[1]
    return pl.pallas_call(
        _ffn_kernel,
        grid=(s // tm, d_ff // tf),
        in_specs=[
            pl.BlockSpec((tm, d), lambda i, j: (i, 0)),
            pl.BlockSpec((1, d), lambda i, j: (0, 0)),
            pl.BlockSpec((1, d), lambda i, j: (0, 0)),
            pl.BlockSpec((d, tf), lambda i, j: (0, j)),
            pl.BlockSpec((d, tf), lambda i, j: (0, j)),
            pl.BlockSpec((tf, d), lambda i, j: (j, 0)),
        ],
        out_specs=pl.BlockSpec((tm, d), lambda i, j: (i, 0)),
        out_shape=jax.ShapeDtypeStruct((s, d), F32),
        scratch_shapes=[pltpu.VMEM((tm, d), BF16), pltpu.VMEM((tm, d), F32)],
        compiler_params=_params("parallel", "arbitrary"),
        name="half_ffn",
    )(x, g_pre.reshape(1, d), g_post.reshape(1, d), w_gate, w_up, w_down)


def _out_proj_kernel(h_ref, x_ref, w_ref, g_ref, o_ref):
    y = _dot(h_ref[...].astype(BF16), w_ref[...])
    o_ref[...] = x_ref[...] + _rms(y, g_ref[...])


def out_proj(heads, x, w, gain, *, tm):
    s, d = x.shape
    k = heads.shape[1]
    return pl.pallas_call(
        _out_proj_kernel,
        grid=(s // tm,),
        in_specs=[
            pl.BlockSpec((tm, k), lambda i: (i, 0)),
            pl.BlockSpec((tm, d), lambda i: (i, 0)),
            pl.BlockSpec((k, d), lambda i: (0, 0)),
            pl.BlockSpec((1, d), lambda i: (0, 0)),
        ],
        out_specs=pl.BlockSpec((tm, d), lambda i: (i, 0)),
        out_shape=jax.ShapeDtypeStruct((s, d), F32),
        compiler_params=_params("parallel"),
        name="out_proj",
    )(heads, x, w, gain.reshape(1, d))


def _mem_attn_kernel(q_ref, kt_ref, v_ref, o_ref):
    scale = MEM_HD ** -0.5
    for h in range(MEM_HEADS):
        cols = slice(h * MEM_HD, (h + 1) * MEM_HD)
        s = _dot(q_ref[:, cols].astype(BF16), kt_ref[cols, :]) * scale
        e = jnp.exp(s - jnp.max(s, axis=-1, keepdims=True))
        p = e / jnp.sum(e, axis=-1, keepdims=True)
        o_ref[:, cols] = _dot(p.astype(BF16), v_ref[:, cols])


def mem_attention(mq, mem_kt, mem_v, *, tm):
    s = mq.shape[0]
    m = mem_v.shape[0]
    return pl.pallas_call(
        _mem_attn_kernel,
        grid=(s // tm,),
        in_specs=[
            pl.BlockSpec((tm, MEM_W), lambda i: (i, 0)),
            pl.BlockSpec((MEM_W, m), lambda i: (0, 0)),
            pl.BlockSpec((m, MEM_W), lambda i: (0, 0)),
        ],
        out_specs=pl.BlockSpec((tm, MEM_W), lambda i: (i, 0)),
        out_shape=jax.ShapeDtypeStruct((s, MEM_W), F32),
        compiler_params=_params("parallel"),
        name="mem_attention",
    )(mq, mem_kt, mem_v)


def _log_sigmoid(x):
    return jnp.minimum(x, 0.0) - jnp.log1p(jnp.exp(-jnp.abs(x)))


def _mlstm_kernel(q_ref, kt_ref, v_ref, o_ref, grow_ref, gcol_ref, brow_ref, bcol_ref, hn_ref,
                  out_ref, c_ref, m_ref):
    chunk = grow_ref.shape[1]
    dv_aug = ML_DV + LANES

    @pl.when(pl.program_id(0) == 0)
    def _():
        c_ref[...] = jnp.zeros_like(c_ref)
        m_ref[...] = jnp.zeros_like(m_ref)

    grow = grow_ref[...] + brow_ref[...]
    gcol = gcol_ref[...] + bcol_ref[...]
    r_idx = lax.broadcasted_iota(jnp.int32, (chunk, chunk), 0)
    c_idx = lax.broadcasted_iota(jnp.int32, (chunk, chunk), 1)
    causal = c_idx <= r_idx
    b_rows = _exact_dot(_log_sigmoid(grow), (r_idx <= c_idx).astype(BF16))
    b_cols = _exact_dot_left(causal.astype(BF16), _log_sigmoid(gcol))
    ones_col = (lax.broadcasted_iota(jnp.int32, (chunk, LANES), 1) == 0).astype(BF16)

    for h in range(ML_HEADS):
        b_r = b_rows[ML_HEADS + h:ML_HEADS + h + 1, :]
        i_r = grow[h:h + 1, :]
        b_c = b_cols[:, ML_HEADS + h:ML_HEADS + h + 1]
        m_prev = m_ref[h:h + 1, 0:1]
        dmat = jnp.where(causal, b_c - b_r + i_r, -jnp.inf)
        m_inter = b_c + m_prev
        m_vec = jnp.maximum(jnp.max(dmat, axis=-1, keepdims=True), m_inter)
        q = q_ref[h].astype(BF16)
        kt = kt_ref[h * ML_DQK:(h + 1) * ML_DQK, :] * (ML_DQK ** -0.5)
        smat = _dot(q, kt.astype(BF16)) * jnp.exp(dmat - m_vec)
        inter = jnp.exp(m_inter - m_vec)
        vcols = slice(h * ML_DV, (h + 1) * ML_DV)
        v_aug = jnp.concatenate([v_ref[:, vcols].astype(BF16), ones_col], axis=1)
        c_prev = c_ref[h]
        num_aug = _dot(smat.astype(BF16), v_aug) + inter * _dot(q, c_prev.astype(BF16))
        den = num_aug[:, ML_DV:ML_DV + 1]
        hval = num_aug[:, :ML_DV] / jnp.maximum(jnp.abs(den), jnp.exp(-m_vec))
        m_last = m_vec[chunk - 1:chunk, :]
        b_last = b_c[chunk - 1:chunk, :]
        w_r = jnp.exp(b_last - b_r + i_r - m_last)
        decay = jnp.exp(b_last + m_prev - m_last)
        c_ref[h] = decay * c_prev + _dot((kt * w_r).astype(BF16), v_aug)
        m_ref[h:h + 1, :] = jnp.broadcast_to(m_last, (1, LANES))
        out_ref[:, vcols] = _rms(hval, hn_ref[:, vcols]) * jax.nn.sigmoid(o_ref[:, vcols])
    del dv_aug


def mlstm_heads(q, kt, v, o, gates_row, gates_col, bias, head_norm):
    s = v.shape[0]
    chunk = ML_CHUNK
    h2 = 2 * ML_HEADS
    return pl.pallas_call(
        _mlstm_kernel,
        grid=(s // chunk,),
        in_specs=[
            pl.BlockSpec((ML_HEADS, chunk, ML_DQK), lambda c: (0, c, 0)),
            pl.BlockSpec((ML_HEADS * ML_DQK, chunk), lambda c: (0, c)),
            pl.BlockSpec((chunk, MAIN_W), lambda c: (c, 0)),
            pl.BlockSpec((chunk, MAIN_W), lambda c: (c, 0)),
            pl.BlockSpec((h2, chunk), lambda c: (0, c)),
            pl.BlockSpec((chunk, h2), lambda c: (c, 0)),
            pl.BlockSpec((h2, 1), lambda c: (0, 0)),
            pl.BlockSpec((1, h2), lambda c: (0, 0)),
            pl.BlockSpec((1, MAIN_W), lambda c: (0, 0)),
        ],
        out_specs=pl.BlockSpec((chunk, MAIN_W), lambda c: (c, 0)),
        out_shape=jax.ShapeDtypeStruct((s, MAIN_W), F32),
        scratch_shapes=[pltpu.VMEM((ML_HEADS, ML_DQK, ML_DV + LANES), F32), pltpu.VMEM((8, LANES), F32)],
        compiler_params=_params("arbitrary"),
        name="mlstm",
    )(q, kt, v, o, gates_row, gates_col, bias.reshape(h2, 1), bias.reshape(1, h2),
      head_norm.reshape(1, MAIN_W))


def _rope(x, cos2, sin2):
    return x * cos2 + pltpu.roll(x, NSA_HD // 2, axis=1) * sin2


def _kv_prep_kernel(kv_ref, cos_ref, sin_ref, kslc_ref, vslc_ref, kwin_ref, vwin_ref):
    cos2 = cos_ref[...]
    sin2 = sin_ref[...]
    for g in range(NSA_KV):
        def col(c):
            start = (c * NSA_KV + g) * NSA_HD
            return kv_ref[:, start:start + NSA_HD]
        kslc_ref[g] = _rope(col(2), cos2, sin2).astype(BF16)
        vslc_ref[g] = col(3).astype(BF16)
        kwin_ref[g] = _rope(col(4), cos2, sin2).astype(BF16)
        vwin_ref[g] = col(5).astype(BF16)


def kv_prep(kv, cos2, sin2, *, tm):
    s, w = kv.shape
    out = jax.ShapeDtypeStruct((NSA_KV, s, NSA_HD), BF16)
    ospec = pl.BlockSpec((NSA_KV, tm, NSA_HD), lambda i: (0, i, 0))
    return pl.pallas_call(
        _kv_prep_kernel,
        grid=(s // tm,),
        in_specs=[
            pl.BlockSpec((tm, w), lambda i: (i, 0)),
            pl.BlockSpec((tm, NSA_HD), lambda i: (i, 0)),
            pl.BlockSpec((tm, NSA_HD), lambda i: (i, 0)),
        ],
        out_specs=[ospec] * 4,
        out_shape=[out] * 4,
        compiler_params=_params("parallel"),
        name="kv_prep",
    )(kv, cos2, sin2)


def _compress_kernel(a_ref, b_ref, pa_ref, pb_ref, w1a_ref, w1b_ref, w2_ref, o_ref):
    xa = (a_ref[...] + pa_ref[...]).astype(BF16)
    xb = (b_ref[...] + pb_ref[...]).astype(BF16)
    hidden = _dot(xa, w1a_ref[...]) + _dot(xb, w1b_ref[...])
    act = jax.nn.gelu(hidden, approximate=True)
    o_ref[...] = _dot(act.astype(BF16), w2_ref[...])


def compress(seg_a, seg_b, pos_a, pos_b, w1a, w1b, w2, *, tm):
    n4, n, kw = seg_a.shape
    hid = w1a.shape[2]
    wmap = lambda cg, i: (cg // NSA_KV, 0, 0)
    return pl.pallas_call(
        _compress_kernel,
        grid=(n4, n // tm),
        in_specs=[
            pl.BlockSpec((None, tm, kw), lambda cg, i: (cg, i, 0)),
            pl.BlockSpec((None, tm, kw), lambda cg, i: (cg, i, 0)),
            pl.BlockSpec((None, 1, kw), wmap),
            pl.BlockSpec((None, 1, kw), wmap),
            pl.BlockSpec((None, kw, hid), wmap),
            pl.BlockSpec((None, kw, hid), wmap),
            pl.BlockSpec((None, hid, NSA_HD), wmap),
        ],
        out_specs=pl.BlockSpec((None, tm, NSA_HD), lambda cg, i: (cg, i, 0)),
        out_shape=jax.ShapeDtypeStruct((n4, n, NSA_HD), F32),
        compiler_params=_params("parallel", "parallel"),
        name="compress",
    )(seg_a, seg_b, pos_a, pos_b, w1a, w1b, w2)


def _stack_heads(q):
    return jnp.concatenate([q[:, j * NSA_HD:(j + 1) * NSA_HD] for j in range(NSA_GROUP)], axis=0)


def _nsa_cmp_kernel(q_ref, kct_ref, vc_ref, cend_ref, ocmp_ref, sel_ref, p_ref):
    qi = pl.program_id(1)
    qb = q_ref.shape[0]
    n_cmp = kct_ref.shape[1]
    n_sel = n_cmp // 4
    scale = NSA_HD ** -0.5
    t_col = qi * qb + lax.broadcasted_iota(jnp.int32, (qb, 1), 0)
    valid = cend_ref[...] <= t_col
    s_all = _dot(_stack_heads(q_ref[...]).astype(BF16), kct_ref[...])
    imp = jnp.zeros((qb, n_cmp), F32)
    for j in range(NSA_GROUP):
        rows = slice(j * qb, (j + 1) * qb)
        s = jnp.where(valid, s_all[rows] * scale, NEG)
        e = jnp.exp(s - jnp.max(s, axis=-1, keepdims=True))
        p = jnp.where(valid, e / jnp.sum(e, axis=-1, keepdims=True), 0.0)
        imp = imp + p
        p_ref[rows, :] = p.astype(BF16)
    o_all = _dot(p_ref[...], vc_ref[...])
    for j in range(NSA_GROUP):
        ocmp_ref[:, j * NSA_HD:(j + 1) * NSA_HD] = o_all[j * qb:(j + 1) * qb]

    imp0, imp1, imp2, imp3 = (imp[:, r * n_sel:(r + 1) * n_sel] for r in range(4))
    jr = lax.broadcasted_iota(jnp.int32, (n_sel, n_sel), 0)
    jc = lax.broadcasted_iota(jnp.int32, (n_sel, n_sel), 1)
    imp3_prev = _exact_dot(imp3, (jr + 1 == jc).astype(BF16))
    p_slc = ((((((imp0 + imp3_prev) + imp1) + imp0) + imp2) + imp1) + imp3) + imp2

    blk = lax.broadcasted_iota(jnp.int32, (qb, n_sel), 1)
    blk_f = blk.astype(F32)
    cur = t_col // SLC_LEN
    forced = (blk == 0) | (blk == cur) | (blk == cur - 1)
    score = jnp.where(blk > cur, NEG, jnp.where(forced, BIG, p_slc))
    sel = jnp.zeros((qb, n_sel), F32)
    for _ in range(SLC_TOPK):
        top = jnp.max(score, axis=-1, keepdims=True)
        first = jnp.min(jnp.where(score == top, blk_f, float(n_sel)), axis=-1, keepdims=True)
        hit = blk_f == first
        sel = jnp.where(hit, 1.0, sel)
        score = jnp.where(hit, REMOVED, score)
    sel_ref[...] = sel.astype(BF16)


def nsa_compressed(q, kct, vc, cmp_end):
    s = q.shape[0]
    n_cmp = kct.shape[2]
    n_sel = n_cmp // 4
    return pl.pallas_call(
        _nsa_cmp_kernel,
        grid=(NSA_KV, s // Q_BLOCK),
        in_specs=[
            pl.BlockSpec((Q_BLOCK, GROUP_W), lambda g, i: (i, g)),
            pl.BlockSpec((None, NSA_HD, n_cmp), lambda g, i: (g, 0, 0)),
            pl.BlockSpec((None, n_cmp, NSA_HD), lambda g, i: (g, 0, 0)),
            pl.BlockSpec((1, n_cmp), lambda g, i: (0, 0)),
        ],
        out_specs=[
            pl.BlockSpec((Q_BLOCK, GROUP_W), lambda g, i: (i, g)),
            pl.BlockSpec((None, Q_BLOCK, n_sel), lambda g, i: (g, i, 0)),
        ],
        out_shape=[
            jax.ShapeDtypeStruct((s, NSA_KV * GROUP_W), F32),
            jax.ShapeDtypeStruct((NSA_KV, s, n_sel), BF16),
        ],
        scratch_shapes=[pltpu.VMEM((NSA_GROUP * Q_BLOCK, n_cmp), BF16)],
        compiler_params=_params("parallel", "parallel"),
        name="nsa_compressed",
    )(q, kct, vc, cmp_end)


def _roped_heads(q_ref, cos_ref, sin_ref):
    cos2 = cos_ref[...]
    sin2 = sin_ref[...]
    q = q_ref[...]
    return jnp.concatenate(
        [_rope(q[:, j * NSA_HD:(j + 1) * NSA_HD], cos2, sin2) for j in range(NSA_GROUP)], axis=0).astype(BF16)


def _nsa_slc_kernel(q_ref, cos_ref, sin_ref, sel_ref, kt_ref, v_ref, o_ref, m_ref, l_ref, acc_ref, p_ref):
    qi = pl.program_id(1)
    qb = q_ref.shape[0]
    n_sel = sel_ref.shape[1]
    tile = p_ref.shape[1]
    scale = NSA_HD ** -0.5
    t_col = qi * qb + lax.broadcasted_iota(jnp.int32, (qb, 1), 0)
    qr = _roped_heads(q_ref, cos_ref, sin_ref)
    sel = sel_ref[...]
    m_ref[...] = jnp.full_like(m_ref, REMOVED)
    l_ref[...] = jnp.zeros_like(l_ref)
    acc_ref[...] = jnp.zeros_like(acc_ref)
    blk_row = lax.broadcasted_iota(jnp.int32, (n_sel, tile), 0)
    blk_of_col = lax.broadcasted_iota(jnp.int32, (n_sel, tile), 1) // SLC_LEN
    col = lax.broadcasted_iota(jnp.int32, (qb, tile), 1)

    def step(kt, carry):
        start = pl.multiple_of(kt * tile, tile)
        s_all = _dot(qr, kt_ref[:, pl.ds(start, tile)])
        expand = (blk_row == blk_of_col + kt * (tile // SLC_LEN)).astype(BF16)
        chosen = _dot(sel, expand)
        allowed = (chosen > 0.5) & (col + start <= t_col)
        for j in range(NSA_GROUP):
            rows = slice(j * qb, (j + 1) * qb)
            s = jnp.where(allowed, s_all[rows] * scale, NEG)
            m_old = m_ref[rows, :]
            m_new = jnp.maximum(m_old, jnp.max(s, axis=-1, keepdims=True))
            alpha = jnp.exp(m_old - m_new)
            e = jnp.exp(s - m_new)
            l_ref[rows, :] = alpha * l_ref[rows, :] + jnp.sum(e, axis=-1, keepdims=True)
            acc_ref[rows, :] = alpha * acc_ref[rows, :]
            m_ref[rows, :] = m_new
            p_ref[rows, :] = e.astype(BF16)
        acc_ref[...] += _dot(p_ref[...], v_ref[pl.ds(start, tile), :])
        return carry

    n_tiles = (qi * qb + qb + tile - 1) // tile
    lax.fori_loop(0, n_tiles, step, 0)
    out = acc_ref[...] / l_ref[...]
    for j in range(NSA_GROUP):
        o_ref[:, j * NSA_HD:(j + 1) * NSA_HD] = out[j * qb:(j + 1) * qb]


def nsa_selected(q, cos2, sin2, sel, kslc_t, vslc):
    s = q.shape[0]
    n_sel = sel.shape[2]
    rows = NSA_GROUP * Q_BLOCK
    return pl.pallas_call(
        _nsa_slc_kernel,
        grid=(NSA_KV, s // Q_BLOCK),
        in_specs=[
            pl.BlockSpec((Q_BLOCK, GROUP_W), lambda g, i: (i, g)),
            pl.BlockSpec((Q_BLOCK, NSA_HD), lambda g, i: (i, 0)),
            pl.BlockSpec((Q_BLOCK, NSA_HD), lambda g, i: (i, 0)),
            pl.BlockSpec((None, Q_BLOCK, n_sel), lambda g, i: (g, i, 0)),
            pl.BlockSpec((None, NSA_HD, s), lambda g, i: (g, 0, 0)),
            pl.BlockSpec((None, s, NSA_HD), lambda g, i: (g, 0, 0)),
        ],
        out_specs=pl.BlockSpec((Q_BLOCK, GROUP_W), lambda g, i: (i, g)),
        out_shape=jax.ShapeDtypeStruct((s, NSA_KV * GROUP_W), F32),
        scratch_shapes=[
            pltpu.VMEM((rows, 1), F32),
            pltpu.VMEM((rows, 1), F32),
            pltpu.VMEM((rows, NSA_HD), F32),
            pltpu.VMEM((rows, SLC_TILE), BF16),
        ],
        compiler_params=_params("parallel", "parallel"),
        name="nsa_selected",
    )(q, cos2, sin2, sel, kslc_t, vslc)


def _nsa_win_kernel(q_ref, cos_ref, sin_ref, kt_ref, v_ref, ocmp_ref, oslc_ref, gl_ref, gb_ref, o_ref, p_ref):
    qi = pl.program_id(1)
    qb = q_ref.shape[0]
    span = p_ref.shape[1]
    scale = NSA_HD ** -0.5
    t0 = qi * qb
    t_col = t0 + lax.broadcasted_iota(jnp.int32, (qb, 1), 0)
    start = pl.multiple_of(jnp.maximum(t0 - WIN, 0), qb)
    spos = start + lax.broadcasted_iota(jnp.int32, (qb, span), 1)
    valid = (spos <= t_col) & (spos > t_col - WIN)
    qr = _roped_heads(q_ref, cos_ref, sin_ref)
    s_all = _dot(qr, kt_ref[:, pl.ds(start, span)])
    for j in range(NSA_GROUP):
        rows = slice(j * qb, (j + 1) * qb)
        s = jnp.where(valid, s_all[rows] * scale, NEG)
        e = jnp.exp(s - jnp.max(s, axis=-1, keepdims=True))
        p_ref[rows, :] = (e / jnp.sum(e, axis=-1, keepdims=True)).astype(BF16)
    o_win = _dot(p_ref[...], v_ref[pl.ds(start, span), :])
    gates = jax.nn.sigmoid(gl_ref[...] + gb_ref[...])
    for j in range(NSA_GROUP):
        cols = slice(j * NSA_HD, (j + 1) * NSA_HD)
        o_ref[:, cols] = (gates[:, 3 * j:3 * j + 1] * ocmp_ref[:, cols]
                          + gates[:, 3 * j + 1:3 * j + 2] * oslc_ref[:, cols]
                          + gates[:, 3 * j + 2:3 * j + 3] * o_win[j * qb:(j + 1) * qb])


def nsa_window_combine(q, cos2, sin2, kwin_t, vwin, o_cmp, o_slc, gate_logits, gate_bias):
    s = q.shape[0]
    span = Q_BLOCK + WIN
    qspec = pl.BlockSpec((Q_BLOCK, GROUP_W), lambda g, i: (i, g))
    return pl.pallas_call(
        _nsa_win_kernel,
        grid=(NSA_KV, s // Q_BLOCK),
        in_specs=[
            qspec,
            pl.BlockSpec((Q_BLOCK, NSA_HD), lambda g, i: (i, 0)),
            pl.BlockSpec((Q_BLOCK, NSA_HD), lambda g, i: (i, 0)),
            pl.BlockSpec((None, NSA_HD, s), lambda g, i: (g, 0, 0)),
            pl.BlockSpec((None, s, NSA_HD), lambda g, i: (g, 0, 0)),
            qspec,
            qspec,
            pl.BlockSpec((Q_BLOCK, LANES), lambda g, i: (i, g)),
            pl.BlockSpec((1, LANES), lambda g, i: (0, g)),
        ],
        out_specs=qspec,
        out_shape=jax.ShapeDtypeStruct((s, NSA_KV * GROUP_W), F32),
        scratch_shapes=[pltpu.VMEM((NSA_GROUP * Q_BLOCK, span), BF16)],
        compiler_params=_params("parallel", "parallel"),
        name="nsa_window",
    )(q, cos2, sin2, kwin_t, vwin, o_cmp, o_slc, gate_logits, gate_bias)


def _row_tile(s, want):
    return want if s % want == 0 else s


def _rope_tables(seq):
    inv = ROPE_THETA ** (-jnp.arange(0, NSA_HD, 2, dtype=F32) / NSA_HD)
    ang = jnp.arange(seq, dtype=F32)[:, None] * inv[None, :]
    cos, sin = jnp.cos(ang), jnp.sin(ang)
    return jnp.concatenate([cos, cos], axis=1), jnp.concatenate([-sin, sin], axis=1)


def _pad_cols(a, width):
    return jnp.pad(a, ((0, 0), (0, width - a.shape[1])))


def _mem_kv(mem, gain, w_kv):
    kv = rms_matmul(mem, gain, w_kv.astype(BF16), tm=mem.shape[0], tn=MEM_W, out_dtype=BF16)
    return kv[:, :MEM_W].T, kv[:, MEM_W:]


def _mlstm_layer_heads(x, gain, w_in, gate_bias, head_norm, mem_kt, mem_v):
    s = x.shape[0]
    qk_w = ML_HEADS * ML_DQK
    w_q, w_k, w_v, w_o, w_g, w_mq = jnp.split(
        w_in, [qk_w, 2 * qk_w, 2 * qk_w + MAIN_W, 2 * qk_w + 2 * MAIN_W, 2 * qk_w + 2 * MAIN_W + 2 * ML_HEADS], axis=1)
    w_main = jnp.concatenate([w_q, w_k, w_v, w_o, w_mq], axis=1).astype(BF16)
    tm = _row_tile(s, 1024)
    proj = rms_matmul(x, gain, w_main, tm=tm, tn=512)
    gates = rms_matmul(x, gain, _pad_cols(w_g, LANES).astype(BF16), tm=tm, tn=LANES)[:, :2 * ML_HEADS]
    q, k, v, o, mq = jnp.split(proj, [qk_w, 2 * qk_w, 2 * qk_w + MAIN_W, 2 * qk_w + 2 * MAIN_W], axis=1)
    q_heads = q.reshape(s, ML_HEADS, ML_DQK).transpose(1, 0, 2)
    h = mlstm_heads(q_heads, k.T, v, o, gates.T, gates, gate_bias, head_norm)
    mo = mem_attention(mq, mem_kt, mem_v, tm=_row_tile(s, 512))
    return jnp.concatenate([h, mo], axis=1)


def _shared_kv(x, kv_norm, w_kv, cmp_pos, cmp_w1, cmp_w2, cos2, sin2):
    s = x.shape[0]
    n_seg = s // CMP_STRIDE
    n_sel = s // SLC_LEN
    seg_w = CMP_STRIDE * NSA_HD
    kv = rms_matmul(x, kv_norm, w_kv.astype(BF16), tm=_row_tile(s, 1024), tn=512)
    kslc, vslc, kwin, vwin = kv_prep(kv, cos2, sin2, tm=_row_tile(s, 512))
    segs = kv[:, :2 * NSA_KV * NSA_HD].reshape(n_seg, CMP_STRIDE, 2 * NSA_KV, NSA_HD)
    segs = segs.transpose(2, 0, 1, 3).reshape(2 * NSA_KV, n_seg, seg_w)
    pos = cmp_pos.reshape(2, 2, 1, seg_w)
    w1 = cmp_w1.astype(BF16)
    cmp = compress(segs, jnp.roll(segs, -1, axis=1), pos[:, 0], pos[:, 1], w1[:, :seg_w], w1[:, seg_w:],
                   cmp_w2.astype(BF16), tm=_row_tile(n_seg, 256))
    cmp = cmp.reshape(2 * NSA_KV, n_sel, 4, NSA_HD).transpose(0, 2, 1, 3).reshape(2 * NSA_KV, n_seg, NSA_HD)
    pos_id = jnp.arange(n_seg, dtype=jnp.int32)
    cmp_block = 4 * (pos_id % n_sel) + pos_id // n_sel
    cmp_end = (cmp_block * CMP_STRIDE + (CMP_LEN - 1)).reshape(1, n_seg)
    return dict(
        kct=cmp[:NSA_KV].astype(BF16).transpose(0, 2, 1), vc=cmp[NSA_KV:].astype(BF16), cmp_end=cmp_end,
        kslc_t=kslc.transpose(0, 2, 1), vslc=vslc, kwin_t=kwin.transpose(0, 2, 1), vwin=vwin)


def _nsa_layer_heads(x, gain, w_in, gate_bias, shared, cos2, sin2, mem_kt, mem_v):
    s = x.shape[0]
    n_gate = 3 * NSA_GROUP
    w_q, w_g, w_mq = jnp.split(w_in, [MAIN_W, MAIN_W + 3 * NSA_HEADS], axis=1)
    w_main = jnp.concatenate([w_q, w_mq], axis=1).astype(BF16)
    w_gate = jnp.concatenate([_pad_cols(w_g[:, g * n_gate:(g + 1) * n_gate], LANES) for g in range(NSA_KV)], axis=1)
    b_gate = jnp.concatenate(
        [_pad_cols(gate_bias[None, g * n_gate:(g + 1) * n_gate], LANES) for g in range(NSA_KV)], axis=1)
    tm = _row_tile(s, 1024)
    proj = rms_matmul(x, gain, w_main, tm=tm, tn=512)
    gate_logits = rms_matmul(x, gain, w_gate.astype(BF16), tm=tm, tn=LANES)
    q, mq = proj[:, :MAIN_W], proj[:, MAIN_W:]
    o_cmp, sel = nsa_compressed(q, shared["kct"], shared["vc"], shared["cmp_end"])
    o_slc = nsa_selected(q, cos2, sin2, sel, shared["kslc_t"], shared["vslc"])
    o = nsa_window_combine(q, cos2, sin2, shared["kwin_t"], shared["vwin"], o_cmp, o_slc, gate_logits, b_gate)
    mo = mem_attention(mq, mem_kt, mem_v, tm=_row_tile(s, 512))
    return jnp.concatenate([o, mo], axis=1)


def kernel(x, mem, norm_gains, ffn_w_gate, ffn_w_up, ffn_w_down, mem_norm, mem_w_kv, w_out, a_w_in, a_gate_bias,
           a_head_norm, kv_norm, w_kv, cmp_pos, cmp_w1, cmp_w2, b_w_in, b_gate_bias):
    batch, seq, d_model = x.shape
    depth = norm_gains.shape[0]
    n_a = a_w_in.shape[0]
    assert seq % (SLC_LEN * LANES) == 0 and seq % ML_CHUNK == 0, "sequence must tile the selection-block lanes"
    cos2, sin2 = _rope_tables(seq)
    tm_ffn = _row_tile(seq, 512)

    def ffn(xb, layer, half, g_pre, g_post):
        return half_ffn(xb, g_pre, g_post, ffn_w_gate[layer, half].astype(BF16), ffn_w_up[layer, half].astype(BF16),
                        ffn_w_down[layer, half].astype(BF16), tm=tm_ffn, tf=512)

    outs = []
    for b in range(batch):
        xb = x[b]
        shared = None
        for layer in range(depth):
            if layer == n_a:
                shared = _shared_kv(xb, kv_norm, w_kv, cmp_pos, cmp_w1, cmp_w2, cos2, sin2)
            g = norm_gains[layer]
            xb = ffn(xb, layer, 0, g[0], g[1])
            mem_kt, mem_v = _mem_kv(mem[b], mem_norm[layer], mem_w_kv[layer])
            if layer < n_a:
                heads = _mlstm_layer_heads(xb, g[2], a_w_in[layer], a_gate_bias[layer], a_head_norm[layer],
                                           mem_kt, mem_v)
            else:
                lb = layer - n_a
                heads = _nsa_layer_heads(xb, g[2], b_w_in[lb], b_gate_bias[lb], shared, cos2, sin2, mem_kt, mem_v)
            xb = out_proj(heads, xb, w_out[layer].astype(BF16), g[3], tm=_row_tile(seq, 256))
            xb = ffn(xb, layer, 1, g[4], g[5])
        outs.append(xb)
    return jnp.stack(outs, axis=0)
```

```python
import functools

import jax
import jax.numpy as jnp
from jax import lax
from jax.experimental import pallas as pl
from jax.experimental.pallas import tpu as pltpu

F32 = jnp.float32
BF16 = jnp.bfloat16

MEM_HEADS = 4
MEM_HD = 128
MEM_W = MEM_HEADS * MEM_HD
ML_HEADS = 4
ML_DV = 384
ML_DQK = 192
MAIN_W = ML_HEADS * ML_DV
NSA_HEADS = 12
NSA_HD = 128
NSA_KV = 2
NSA_GROUP = NSA_HEADS // NSA_KV
GROUP_W = NSA_GROUP * NSA_HD
CMP_LEN = 32
CMP_STRIDE = 16
SLC_LEN = 64
SLC_TOPK = 16
WIN = 512
Q_BLOCK = 128
FFN_RES = 0.5
ROPE_THETA = 10000.0
EPS = 1e-6
NEG = -1e30
BIG = 1e30
REMOVED = -3e38
LOG2E = 1.4426950408889634
TINY = 1e-30

LANES = 128
VMEM_LIMIT_BYTES = 56 * 1024 * 1024

ML_CHUNK = 256
SLC_TILE = 512
ONES_ROWS = 16


def _params(*semantics):
    return pltpu.CompilerParams(dimension_semantics=semantics, vmem_limit_bytes=VMEM_LIMIT_BYTES)


def _rms(x, gain):
    return x * lax.rsqrt(jnp.mean(x * x, axis=-1, keepdims=True) + EPS) * gain


def _dot(a, b):
    return jnp.dot(a, b, preferred_element_type=F32)


def _exact_dot(a, b01):
    a1 = a.astype(BF16)
    r1 = a - a1.astype(F32)
    a2 = r1.astype(BF16)
    a3 = (r1 - a2.astype(F32)).astype(BF16)
    return _dot(a1, b01) + _dot(a2, b01) + _dot(a3, b01)


def _exact_dot_left(b01, a):
    a1 = a.astype(BF16)
    r1 = a - a1.astype(F32)
    a2 = r1.astype(BF16)
    a3 = (r1 - a2.astype(F32)).astype(BF16)
    return _dot(b01, a1) + _dot(b01, a2) + _dot(b01, a3)


def _rms_matmul_kernel(x_ref, g_ref, w_ref, o_ref, xn_ref):
    @pl.when(pl.program_id(1) == 0)
    def _():
        xn_ref[...] = _rms(x_ref[...], g_ref[...]).astype(BF16)

    o_ref[...] = _dot(xn_ref[...], w_ref[...]).astype(o_ref.dtype)


def rms_matmul(x, gain, w, *, tm, tn, out_dtype=F32):
    s, k = x.shape
    n = w.shape[1]
    return pl.pallas_call(
        _rms_matmul_kernel,
        grid=(s // tm, n // tn),
        in_specs=[
            pl.BlockSpec((tm, k), lambda i, j: (i, 0)),
            pl.BlockSpec((1, k), lambda i, j: (0, 0)),
            pl.BlockSpec((k, tn), lambda i, j: (0, j)),
        ],
        out_specs=pl.BlockSpec((tm, tn), lambda i, j: (i, j)),
        out_shape=jax.ShapeDtypeStruct((s, n), out_dtype),
        scratch_shapes=[pltpu.VMEM((tm, k), BF16)],
        compiler_params=_params("parallel", "arbitrary"),
        name="rms_matmul",
    )(x, gain.reshape(1, k), w)


def _ffn_kernel(x_ref, gpre_ref, gpost_ref, wg_ref, wu_ref, wd_ref, o_ref, xn_ref, acc_ref):
    j = pl.program_id(1)

    @pl.when(j == 0)
    def _():
        xn_ref[...] = _rms(x_ref[...], gpre_ref[...]).astype(BF16)
        acc_ref[...] = jnp.zeros_like(acc_ref)

    xn = xn_ref[...]
    gate = _dot(xn, wg_ref[...])
    up = _dot(xn, wu_ref[...])
    hidden = (gate * jax.nn.sigmoid(gate) * up).astype(BF16)
    acc_ref[...] += _dot(hidden, wd_ref[...])

    @pl.when(j == pl.num_programs(1) - 1)
    def _():
        o_ref[...] = x_ref[...] + FFN_RES * _rms(acc_ref[...], gpost_ref[...])


def half_ffn(x, g_pre, g_post, w_gate, w_up, w_down, *, tm, tf):
    s, d = x.shape
    d_ff = w_gate.shape[1]
    return pl.pallas_call(
        _ffn_kernel,
        grid=(s // tm, d_ff // tf),
        in_specs=[
            pl.BlockSpec((tm, d), lambda i, j: (i, 0)),
            pl.BlockSpec((1, d), lambda i, j: (0, 0)),
            pl.BlockSpec((1, d), lambda i, j: (0, 0)),
            pl.BlockSpec((d, tf), lambda i, j: (0, j)),
            pl.BlockSpec((d, tf), lambda i, j: (0, j)),
            pl.BlockSpec((tf, d), lambda i, j: (j, 0)),
        ],
        out_specs=pl.BlockSpec((tm, d), lambda i, j: (i, 0)),
        out_shape=jax.ShapeDtypeStruct((s, d), F32),
        scratch_shapes=[pltpu.VMEM((tm, d), BF16), pltpu.VMEM((tm, d), F32)],
        compiler_params=_params("parallel", "arbitrary"),
        name="half_ffn",
    )(x, g_pre.reshape(1, d), g_post.reshape(1, d), w_gate, w_up, w_down)


def _out_proj_kernel(h_ref, x_ref, w_ref, g_ref, o_ref):
    y = _dot(h_ref[...].astype(BF16), w_ref[...])
    o_ref[...] = x_ref[...] + _rms(y, g_ref[...])


def out_proj(heads, x, w, gain, *, tm):
    s, d = x.shape
    k = heads.shape[1]
    return pl.pallas_call(
        _out_proj_kernel,
        grid=(s // tm,),
        in_specs=[
            pl.BlockSpec((tm, k), lambda i: (i, 0)),
            pl.BlockSpec((tm, d), lambda i: (i, 0)),
            pl.BlockSpec((k, d), lambda i: (0, 0)),
            pl.BlockSpec((1, d), lambda i: (0, 0)),
        ],
        out_specs=pl.BlockSpec((tm, d), lambda i: (i, 0)),
        out_shape=jax.ShapeDtypeStruct((s, d), F32),
        compiler_params=_params("parallel"),
        name="out_proj",
    )(heads, x, w, gain.reshape(1, d))


def _mem_attn_kernel(q_ref, kt_ref, v_ref, o_ref):
    scale = MEM_HD ** -0.5
    for h in range(MEM_HEADS):
        cols = slice(h * MEM_HD, (h + 1) * MEM_HD)
        s = _dot(q_ref[:, cols].astype(BF16), kt_ref[cols, :]) * scale
        e = jnp.exp(s - jnp.max(s, axis=-1, keepdims=True))
        p = e / jnp.sum(e, axis=-1, keepdims=True)
        o_ref[:, cols] = _dot(p.astype(BF16), v_ref[:, cols])


def mem_attention(mq, mem_kt, mem_v, *, tm):
    s = mq.shape[0]
    m = mem_v.shape[0]
    return pl.pallas_call(
        _mem_attn_kernel,
        grid=(s // tm,),
        in_specs=[
            pl.BlockSpec((tm, MEM_W), lambda i: (i, 0)),
            pl.BlockSpec((MEM_W, m), lambda i: (0, 0)),
            pl.BlockSpec((m, MEM_W), lambda i: (0, 0)),
        ],
        out_specs=pl.BlockSpec((tm, MEM_W), lambda i: (i, 0)),
        out_shape=jax.ShapeDtypeStruct((s, MEM_W), F32),
        compiler_params=_params("parallel"),
        name="mem_attention",
    )(mq, mem_kt, mem_v)


def _log_sigmoid(x):
    return jnp.minimum(x, 0.0) - jnp.log1p(jnp.exp(-jnp.abs(x)))


def _mlstm_kernel(q_ref, kt_ref, v_ref, o_ref, grow_ref, gcol_ref, brow_ref, bcol_ref, hn_ref,
                  out_ref, c_ref, m_ref):
    chunk = grow_ref.shape[1]

    @pl.when(pl.program_id(0) == 0)
    def _():
        c_ref[...] = jnp.zeros_like(c_ref)
        m_ref[...] = jnp.zeros_like(m_ref)

    grow = grow_ref[...] + brow_ref[...]
    gcol = gcol_ref[...] + bcol_ref[...]
    r_idx = lax.broadcasted_iota(jnp.int32, (chunk, chunk), 0)
    c_idx = lax.broadcasted_iota(jnp.int32, (chunk, chunk), 1)
    causal = c_idx <= r_idx
    b_rows = _exact_dot(_log_sigmoid(grow), (r_idx <= c_idx).astype(BF16))
    b_cols = _exact_dot_left(causal.astype(BF16), _log_sigmoid(gcol))
    ones_col = (lax.broadcasted_iota(jnp.int32, (chunk, LANES), 1) == 0).astype(BF16)

    for h in range(ML_HEADS):
        b_r = b_rows[ML_HEADS + h:ML_HEADS + h + 1, :]
        i_r = grow[h:h + 1, :]
        b_c = b_cols[:, ML_HEADS + h:ML_HEADS + h + 1]
        m_prev = m_ref[h:h + 1, 0:1]
        dmat = jnp.where(causal, b_c - b_r + i_r, -jnp.inf)
        m_inter = b_c + m_prev
        m_vec = jnp.maximum(jnp.max(dmat, axis=-1, keepdims=True), m_inter)
        q = q_ref[h].astype(BF16)
        kt = kt_ref[h * ML_DQK:(h + 1) * ML_DQK, :] * (ML_DQK ** -0.5)
        smat = _dot(q, kt.astype(BF16)) * jnp.exp(dmat - m_vec)
        inter = jnp.exp(m_inter - m_vec)
        vcols = slice(h * ML_DV, (h + 1) * ML_DV)
        v_aug = jnp.concatenate([v_ref[:, vcols].astype(BF16), ones_col], axis=1)
        c_prev = c_ref[h]
        num_aug = _dot(smat.astype(BF16), v_aug) + inter * _dot(q, c_prev.astype(BF16))
        den = num_aug[:, ML_DV:ML_DV + 1]
        hval = num_aug[:, :ML_DV] / jnp.maximum(jnp.abs(den), jnp.exp(-m_vec))
        m_last = m_vec[chunk - 1:chunk, :]
        b_last = b_c[chunk - 1:chunk, :]
        w_r = jnp.exp(b_last - b_r + i_r - m_last)
        decay = jnp.exp(b_last + m_prev - m_last)
        c_ref[h] = decay * c_prev + _dot((kt * w_r).astype(BF16), v_aug)
        m_ref[h:h + 1, :] = jnp.broadcast_to(m_last, (1, LANES))
        out_ref[:, vcols] = _rms(hval, hn_ref[:, vcols]) * jax.nn.sigmoid(o_ref[:, vcols])


def mlstm_heads(q, kt, v, o, gates_row, gates_col, bias, head_norm):
    s = v.shape[0]
    chunk = ML_CHUNK
    h2 = 2 * ML_HEADS
    return pl.pallas_call(
        _mlstm_kernel,
        grid=(s // chunk,),
        in_specs=[
            pl.BlockSpec((ML_HEADS, chunk, ML_DQK), lambda c: (0, c, 0)),
            pl.BlockSpec((ML_HEADS * ML_DQK, chunk), lambda c: (0, c)),
            pl.BlockSpec((chunk, MAIN_W), lambda c: (c, 0)),
            pl.BlockSpec((chunk, MAIN_W), lambda c: (c, 0)),
            pl.BlockSpec((h2, chunk), lambda c: (0, c)),
            pl.BlockSpec((chunk, h2), lambda c: (c, 0)),
            pl.BlockSpec((h2, 1), lambda c: (0, 0)),
            pl.BlockSpec((1, h2), lambda c: (0, 0)),
            pl.BlockSpec((1, MAIN_W), lambda c: (0, 0)),
        ],
        out_specs=pl.BlockSpec((chunk, MAIN_W), lambda c: (c, 0)),
        out_shape=jax.ShapeDtypeStruct((s, MAIN_W), F32),
        scratch_shapes=[pltpu.VMEM((ML_HEADS, ML_DQK, ML_DV + LANES), F32), pltpu.VMEM((8, LANES), F32)],
        compiler_params=_params("arbitrary"),
        name="mlstm",
    )(q, kt, v, o, gates_row, gates_col, bias.reshape(h2, 1), bias.reshape(1, h2),
      head_norm.reshape(1, MAIN_W))


def _rope(x, cos2, sin2):
    return x * cos2 + pltpu.roll(x, NSA_HD // 2, axis=1) * sin2


def _kv_prep_kernel(kv_ref, cos_ref, sin_ref, kslc_ref, vslc_ref, kwin_ref, vwin_ref):
    cos2 = cos_ref[...]
    sin2 = sin_ref[...]
    for g in range(NSA_KV):
        def col(c):
            start = (c * NSA_KV + g) * NSA_HD
            return kv_ref[:, start:start + NSA_HD]
        kslc_ref[g] = _rope(col(2), cos2, sin2).astype(BF16)
        vslc_ref[g] = col(3).astype(BF16)
        kwin_ref[g] = _rope(col(4), cos2, sin2).astype(BF16)
        vwin_ref[g] = col(5).astype(BF16)


def kv_prep(kv, cos2, sin2, *, tm):
    s, w = kv.shape
    out = jax.ShapeDtypeStruct((NSA_KV, s, NSA_HD), BF16)
    ospec = pl.BlockSpec((NSA_KV, tm, NSA_HD), lambda i: (0, i, 0))
    return pl.pallas_call(
        _kv_prep_kernel,
        grid=(s // tm,),
        in_specs=[
            pl.BlockSpec((tm, w), lambda i: (i, 0)),
            pl.BlockSpec((tm, NSA_HD), lambda i: (i, 0)),
            pl.BlockSpec((tm, NSA_HD), lambda i: (i, 0)),
        ],
        out_specs=[ospec] * 4,
        out_shape=[out] * 4,
        compiler_params=_params("parallel"),
        name="kv_prep",
    )(kv, cos2, sin2)


def _compress_kernel(a_ref, b_ref, pa_ref, pb_ref, w1a_ref, w1b_ref, w2_ref, o_ref):
    xa = (a_ref[...] + pa_ref[...]).astype(BF16)
    xb = (b_ref[...] + pb_ref[...]).astype(BF16)
    hidden = _dot(xa, w1a_ref[...]) + _dot(xb, w1b_ref[...])
    act = jax.nn.gelu(hidden, approximate=True)
    o_ref[...] = _dot(act.astype(BF16), w2_ref[...])


def compress(seg_a, seg_b, pos_a, pos_b, w1a, w1b, w2, *, tm):
    n4, n, kw = seg_a.shape
    hid = w1a.shape[2]
    wmap = lambda cg, i: (cg // NSA_KV, 0, 0)
    return pl.pallas_call(
        _compress_kernel,
        grid=(n4, n // tm),
        in_specs=[
            pl.BlockSpec((None, tm, kw), lambda cg, i: (cg, i, 0)),
            pl.BlockSpec((None, tm, kw), lambda cg, i: (cg, i, 0)),
            pl.BlockSpec((None, 1, kw), wmap),
            pl.BlockSpec((None, 1, kw), wmap),
            pl.BlockSpec((None, kw, hid), wmap),
            pl.BlockSpec((None, kw, hid), wmap),
            pl.BlockSpec((None, hid, NSA_HD), wmap),
        ],
        out_specs=pl.BlockSpec((None, tm, NSA_HD), lambda cg, i: (cg, i, 0)),
        out_shape=jax.ShapeDtypeStruct((n4, n, NSA_HD), F32),
        compiler_params=_params("parallel", "parallel"),
        name="compress",
    )(seg_a, seg_b, pos_a, pos_b, w1a, w1b, w2)


def _rope_t(x, cos2t, sin2t):
    half = NSA_HD // 2
    return x * cos2t + jnp.concatenate([x[half:], x[:half]], axis=0) * sin2t


def _col_max(chunks):
    return jnp.max(functools.reduce(jnp.maximum, chunks), axis=0, keepdims=True)


def _nsa_kernel(q_ref, cos_ref, sin_ref, gl_ref, gb_ref, kc_ref, vct_ref, kslc_ref, vslct_ref, kwin_ref, vwint_ref,
                o_ref, qt_ref, qrt_ref, sel_ref, p_ref, acc_ref):
    qi = pl.program_id(1)
    qb = Q_BLOCK
    n_cmp = kc_ref.shape[0]
    n_sel = n_cmp // 4
    t0 = qi * qb
    t_row = t0 + lax.broadcasted_iota(jnp.int32, (1, qb), 1)
    hcols = [slice(j * qb, (j + 1) * qb) for j in range(NSA_GROUP)]

    cos2t = cos_ref[...]
    sin2t = sin_ref[...]
    for j in range(NSA_GROUP):
        qt = q_ref[:, j * NSA_HD:(j + 1) * NSA_HD].T * (NSA_HD ** -0.5 * LOG2E)
        qt_ref[:, hcols[j]] = qt.astype(BF16)
        qrt_ref[:, hcols[j]] = _rope_t(qt, cos2t, sin2t).astype(BF16)

    s_c = _dot(kc_ref[...], qt_ref[...])
    blk = lax.broadcasted_iota(jnp.int32, (n_sel, qb), 0)
    cmp_bias = [jnp.where(SLC_LEN * blk + (CMP_STRIDE * r + CMP_LEN - 1) <= t_row, 0.0, NEG) for r in range(4)]
    has_valid = (t_row >= CMP_LEN - 1).astype(F32)
    imp = [jnp.zeros((n_sel, qb), F32) for _ in range(4)]
    for j in range(NSA_GROUP):
        sb = [s_c[r * n_sel:(r + 1) * n_sel, hcols[j]] + cmp_bias[r] for r in range(4)]
        m = _col_max(sb)
        e = [jnp.exp2(x - m) for x in sb]
        den = functools.reduce(jnp.add, [jnp.sum(x, axis=0, keepdims=True) for x in e])
        rinv = has_valid / jnp.maximum(den, TINY)
        for r in range(4):
            p = e[r] * rinv
            imp[r] = imp[r] + p
            p_ref[r * n_sel:(r + 1) * n_sel, hcols[j]] = p.astype(BF16)
    o_cmp = _dot(vct_ref[...], p_ref[0:n_cmp, :])

    imp3_prev = jnp.where(blk == 0, 0.0, pltpu.roll(imp[3], 1, axis=0))
    p_slc = ((((((imp[0] + imp3_prev) + imp[1]) + imp[0]) + imp[2]) + imp[1]) + imp[3]) + imp[2]
    cur = jnp.right_shift(t_row, 6)
    forced = (blk == 0) | (blk == cur) | (blk == cur - 1)
    score = jnp.where(blk > cur, NEG, jnp.where(forced, BIG, p_slc))
    blk_f = blk.astype(F32)
    sel = jnp.zeros((n_sel, qb), F32)
    for _ in range(SLC_TOPK):
        top = jnp.max(score, axis=0, keepdims=True)
        first = jnp.min(jnp.where(score == top, blk_f, float(n_sel)), axis=0, keepdims=True)
        hit = blk_f == first
        sel = jnp.where(hit, 1.0, sel)
        score = jnp.where(hit, REMOVED, score)
    sel_ref[...] = sel

    acc_ref[...] = jnp.zeros_like(acc_ref)
    n_blk = SLC_TILE // SLC_LEN
    kpos0 = lax.broadcasted_iota(jnp.int32, (SLC_LEN, qb), 0)

    def step(kt, m_run):
        start = pl.multiple_of(kt * SLC_TILE, SLC_TILE)
        s_t = _dot(kslc_ref[pl.ds(start, SLC_TILE), :], qrt_ref[...])
        bias = []
        for b in range(n_blk):
            chosen = sel_ref[pl.ds(kt * n_blk + b, 1), :] > 0.5
            ok = chosen & (kpos0 + (start + b * SLC_LEN) <= t_row)
            bias.append(jnp.where(ok, 0.0, NEG))
        m_new, alpha = [], []
        for j in range(NSA_GROUP):
            sb = [s_t[b * SLC_LEN:(b + 1) * SLC_LEN, hcols[j]] + bias[b] for b in range(n_blk)]
            m_old = m_run[:, hcols[j]]
            m_j = jnp.maximum(m_old, _col_max(sb))
            for b in range(n_blk):
                p_ref[b * SLC_LEN:(b + 1) * SLC_LEN, hcols[j]] = jnp.exp2(sb[b] - m_j).astype(BF16)
            m_new.append(m_j)
            alpha.append(jnp.exp2(m_old - m_j))
        pv = _dot(vslct_ref[:, pl.ds(start, SLC_TILE)], p_ref[0:SLC_TILE, :])
        acc_ref[...] = acc_ref[...] * jnp.concatenate(alpha, axis=1) + pv
        return jnp.concatenate(m_new, axis=1)

    n_tiles = (t0 + qb + SLC_TILE - 1) // SLC_TILE
    lax.fori_loop(0, n_tiles, step, jnp.full((1, NSA_GROUP * qb), REMOVED, F32))
    acc = acc_ref[...]
    o_slc = acc[:NSA_HD] / acc[NSA_HD:NSA_HD + 1]

    span = qb + WIN
    start = pl.multiple_of(jnp.maximum(t0 - WIN, 0), qb)
    s_w = _dot(kwin_ref[pl.ds(start, span), :], qrt_ref[...])
    spos0 = lax.broadcasted_iota(jnp.int32, (qb, qb), 0)
    win_bias = []
    for c in range(span // qb):
        spos = spos0 + (start + c * qb)
        win_bias.append(jnp.where((spos <= t_row) & (spos > t_row - WIN), 0.0, NEG))
    for j in range(NSA_GROUP):
        sb = [s_w[c * qb:(c + 1) * qb, hcols[j]] + win_bias[c] for c in range(span // qb)]
        m = _col_max(sb)
        for c in range(span // qb):
            p_ref[c * qb:(c + 1) * qb, hcols[j]] = jnp.exp2(sb[c] - m).astype(BF16)
    pw = _dot(vwint_ref[:, pl.ds(start, span)], p_ref[0:span, :])
    o_win = pw[:NSA_HD] / pw[NSA_HD:NSA_HD + 1]

    gates_t = jax.nn.sigmoid(gl_ref[...] + gb_ref[...]).T
    for j in range(NSA_GROUP):
        o_t = (gates_t[3 * j:3 * j + 1] * o_cmp[:, hcols[j]]
               + gates_t[3 * j + 1:3 * j + 2] * o_slc[:, hcols[j]]
               + gates_t[3 * j + 2:3 * j + 3] * o_win[:, hcols[j]])
        o_ref[:, j * NSA_HD:(j + 1) * NSA_HD] = o_t.T


def nsa_attention(q, cos2t, sin2t, gate_logits, gate_bias, kc, vct, kslc, vslct, kwin, vwint):
    s = q.shape[0]
    n_cmp = kc.shape[1]
    rows_aug = NSA_HD + ONES_ROWS
    width = NSA_GROUP * Q_BLOCK
    qspec = pl.BlockSpec((Q_BLOCK, GROUP_W), lambda g, i: (i, g))
    tspec = pl.BlockSpec((NSA_HD, Q_BLOCK), lambda g, i: (0, i))

    def resident(shape):
        return pl.BlockSpec((None,) + shape, lambda g, i: (g, 0, 0), pipeline_mode=pl.Buffered(1))

    return pl.pallas_call(
        _nsa_kernel,
        grid=(NSA_KV, s // Q_BLOCK),
        in_specs=[
            qspec, tspec, tspec,
            pl.BlockSpec((Q_BLOCK, LANES), lambda g, i: (i, g)),
            pl.BlockSpec((1, LANES), lambda g, i: (0, g)),
            resident((n_cmp, NSA_HD)), resident((NSA_HD, n_cmp)),
            resident((s, NSA_HD)), resident((rows_aug, s)),
            resident((s, NSA_HD)), resident((rows_aug, s)),
        ],
        out_specs=qspec,
        out_shape=jax.ShapeDtypeStruct((s, NSA_KV * GROUP_W), F32),
        scratch_shapes=[
            pltpu.VMEM((NSA_HD, width), BF16),
            pltpu.VMEM((NSA_HD, width), BF16),
            pltpu.VMEM((n_cmp // 4, Q_BLOCK), F32),
            pltpu.VMEM((max(n_cmp, Q_BLOCK + WIN, SLC_TILE), width), BF16),
            pltpu.VMEM((rows_aug, width), F32),
        ],
        compiler_params=_params("parallel", "arbitrary"),
        name="nsa_attention",
    )(q, cos2t, sin2t, gate_logits, gate_bias, kc, vct, kslc, vslct, kwin, vwint)


def _row_tile(s, want):
    return want if s % want == 0 else s


def _rope_tables(seq):
    inv = ROPE_THETA ** (-jnp.arange(0, NSA_HD, 2, dtype=F32) / NSA_HD)
    ang = jnp.arange(seq, dtype=F32)[:, None] * inv[None, :]
    cos, sin = jnp.cos(ang), jnp.sin(ang)
    return jnp.concatenate([cos, cos], axis=1), jnp.concatenate([-sin, sin], axis=1)


def _pad_cols(a, width):
    return jnp.pad(a, ((0, 0), (0, width - a.shape[1])))


def _mem_kv(mem, gain, w_kv):
    kv = rms_matmul(mem, gain, w_kv.astype(BF16), tm=mem.shape[0], tn=MEM_W, out_dtype=BF16)
    return kv[:, :MEM_W].T, kv[:, MEM_W:]


def _mlstm_layer_heads(x, gain, w_in, gate_bias, head_norm, mem_kt, mem_v):
    s = x.shape[0]
    qk_w = ML_HEADS * ML_DQK
    w_q, w_k, w_v, w_o, w_g, w_mq = jnp.split(
        w_in, [qk_w, 2 * qk_w, 2 * qk_w + MAIN_W, 2 * qk_w + 2 * MAIN_W, 2 * qk_w + 2 * MAIN_W + 2 * ML_HEADS], axis=1)
    w_main = jnp.concatenate([w_q, w_k, w_v, w_o, w_mq], axis=1).astype(BF16)
    tm = _row_tile(s, 1024)
    proj = rms_matmul(x, gain, w_main, tm=tm, tn=512)
    gates = rms_matmul(x, gain, _pad_cols(w_g, LANES).astype(BF16), tm=tm, tn=LANES)[:, :2 * ML_HEADS]
    q, k, v, o, mq = jnp.split(proj, [qk_w, 2 * qk_w, 2 * qk_w + MAIN_W, 2 * qk_w + 2 * MAIN_W], axis=1)
    q_heads = q.reshape(s, ML_HEADS, ML_DQK).transpose(1, 0, 2)
    h = mlstm_heads(q_heads, k.T, v, o, gates.T, gates, gate_bias, head_norm)
    mo = mem_attention(mq, mem_kt, mem_v, tm=_row_tile(s, 512))
    return jnp.concatenate([h, mo], axis=1)


def _with_ones_rows(v):
    g, s, _ = v.shape
    return jnp.concatenate([v.transpose(0, 2, 1), jnp.ones((g, ONES_ROWS, s), v.dtype)], axis=1)


def _shared_kv(x, kv_norm, w_kv, cmp_pos, cmp_w1, cmp_w2, cos2, sin2):
    s = x.shape[0]
    n_seg = s // CMP_STRIDE
    n_sel = s // SLC_LEN
    seg_w = CMP_STRIDE * NSA_HD
    kv = rms_matmul(x, kv_norm, w_kv.astype(BF16), tm=_row_tile(s, 1024), tn=512)
    kslc, vslc, kwin, vwin = kv_prep(kv, cos2, sin2, tm=_row_tile(s, 512))
    segs = kv[:, :2 * NSA_KV * NSA_HD].reshape(n_seg, CMP_STRIDE, 2 * NSA_KV, NSA_HD)
    segs = segs.transpose(2, 0, 1, 3).reshape(2 * NSA_KV, n_seg, seg_w)
    pos = cmp_pos.reshape(2, 2, 1, seg_w)
    w1 = cmp_w1.astype(BF16)
    cmp = compress(segs, jnp.roll(segs, -1, axis=1), pos[:, 0], pos[:, 1], w1[:, :seg_w], w1[:, seg_w:],
                   cmp_w2.astype(BF16), tm=_row_tile(n_seg, 256))
    cmp = cmp.reshape(2 * NSA_KV, n_sel, 4, NSA_HD).transpose(0, 2, 1, 3).reshape(2 * NSA_KV, n_seg, NSA_HD)
    cmp = cmp.astype(BF16)
    return dict(kc=cmp[:NSA_KV], vct=cmp[NSA_KV:].transpose(0, 2, 1), kslc=kslc, vslct=_with_ones_rows(vslc),
                kwin=kwin, vwint=_with_ones_rows(vwin))


def _nsa_layer_heads(x, gain, w_in, gate_bias, shared, cos2t, sin2t, mem_kt, mem_v):
    s = x.shape[0]
    n_gate = 3 * NSA_GROUP
    w_q, w_g, w_mq = jnp.split(w_in, [MAIN_W, MAIN_W + 3 * NSA_HEADS], axis=1)
    w_main = jnp.concatenate([w_q, w_mq], axis=1).astype(BF16)
    w_gate = jnp.concatenate([_pad_cols(w_g[:, g * n_gate:(g + 1) * n_gate], LANES) for g in range(NSA_KV)], axis=1)
    b_gate = jnp.concatenate(
        [_pad_cols(gate_bias[None, g * n_gate:(g + 1) * n_gate], LANES) for g in range(NSA_KV)], axis=1)
    tm = _row_tile(s, 1024)
    proj = rms_matmul(x, gain, w_main, tm=tm, tn=512)
    gate_logits = rms_matmul(x, gain, w_gate.astype(BF16), tm=tm, tn=LANES)
    q, mq = proj[:, :MAIN_W], proj[:, MAIN_W:]
    o = nsa_attention(q, cos2t, sin2t, gate_logits, b_gate, shared["kc"], shared["vct"], shared["kslc"],
                      shared["vslct"], shared["kwin"], shared["vwint"])
    mo = mem_attention(mq, mem_kt, mem_v, tm=_row_tile(s, 512))
    return jnp.concatenate([o, mo], axis=1)


def kernel(x, mem, norm_gains, ffn_w_gate, ffn_w_up, ffn_w_down, mem_norm, mem_w_kv, w_out, a_w_in, a_gate_bias,
           a_head_norm, kv_norm, w_kv, cmp_pos, cmp_w1, cmp_w2, b_w_in, b_gate_bias):
    batch, seq, d_model = x.shape
    depth = norm_gains.shape[0]
    n_a = a_w_in.shape[0]
    assert seq % (SLC_LEN * LANES) == 0 and seq % ML_CHUNK == 0, "sequence must tile the selection-block lanes"
    cos2, sin2 = _rope_tables(seq)
    cos2t, sin2t = cos2.T, sin2.T
    tm_ffn = _row_tile(seq, 512)

    def ffn(xb, layer, half, g_pre, g_post):
        return half_ffn(xb, g_pre, g_post, ffn_w_gate[layer, half].astype(BF16), ffn_w_up[layer, half].astype(BF16),
                        ffn_w_down[layer, half].astype(BF16), tm=tm_ffn, tf=512)

    outs = []
    for b in range(batch):
        xb = x[b]
        shared = None
        for layer in range(depth):
            if layer == n_a:
                shared = _shared_kv(xb, kv_norm, w_kv, cmp_pos, cmp_w1, cmp_w2, cos2, sin2)
            g = norm_gains[layer]
            xb = ffn(xb, layer, 0, g[0], g[1])
            mem_kt, mem_v = _mem_kv(mem[b], mem_norm[layer], mem_w_kv[layer])
            if layer < n_a:
                heads = _mlstm_layer_heads(xb, g[2], a_w_in[layer], a_gate_bias[layer], a_head_norm[layer],
                                           mem_kt, mem_v)
            else:
                lb = layer - n_a
                heads = _nsa_layer_heads(xb, g[2], b_w_in[lb], b_gate_bias[lb], shared, cos2t, sin2t, mem_kt, mem_v)
            xb = out_proj(heads, xb, w_out[layer].astype(BF16), g[3], tm=_row_tile(seq, 256))
            xb = ffn(xb, layer, 1, g[4], g[5])
        outs.append(xb)
    return jnp.stack(outs, axis=0)
```

```python
import functools

import jax
import jax.numpy as jnp
from jax import lax
from jax.experimental import pallas as pl
from jax.experimental.pallas import tpu as pltpu

F32 = jnp.float32
BF16 = jnp.bfloat16

MEM_HEADS = 4
MEM_HD = 128
MEM_W = MEM_HEADS * MEM_HD
ML_HEADS = 4
ML_DV = 384
ML_DQK = 192
MAIN_W = ML_HEADS * ML_DV
NSA_HEADS = 12
NSA_HD = 128
NSA_KV = 2
NSA_GROUP = NSA_HEADS // NSA_KV
GROUP_W = NSA_GROUP * NSA_HD
CMP_LEN = 32
CMP_STRIDE = 16
SLC_LEN = 64
SLC_TOPK = 16
WIN = 512
Q_BLOCK = 128
FFN_RES = 0.5
ROPE_THETA = 10000.0
EPS = 1e-6
NEG = -1e30
BIG = 1e30
REMOVED = -3e38
LOG2E = 1.4426950408889634
TINY = 1e-30

LANES = 128
VMEM_LIMIT_BYTES = 56 * 1024 * 1024

ML_CHUNK = 256
SLC_TILE = 512
ONES_ROWS = 16


def _params(*semantics):
    return pltpu.CompilerParams(dimension_semantics=semantics, vmem_limit_bytes=VMEM_LIMIT_BYTES)


def _rms(x, gain):
    return x * lax.rsqrt(jnp.mean(x * x, axis=-1, keepdims=True) + EPS) * gain


def _dot(a, b):
    return jnp.dot(a, b, preferred_element_type=F32)


def _exact_dot(a, b01):
    a1 = a.astype(BF16)
    r1 = a - a1.astype(F32)
    a2 = r1.astype(BF16)
    a3 = (r1 - a2.astype(F32)).astype(BF16)
    return _dot(a1, b01) + _dot(a2, b01) + _dot(a3, b01)


def _exact_dot_left(b01, a):
    a1 = a.astype(BF16)
    r1 = a - a1.astype(F32)
    a2 = r1.astype(BF16)
    a3 = (r1 - a2.astype(F32)).astype(BF16)
    return _dot(b01, a1) + _dot(b01, a2) + _dot(b01, a3)


def _rms_matmul_kernel(x_ref, g_ref, w_ref, o_ref, xn_ref):
    @pl.when(pl.program_id(1) == 0)
    def _():
        xn_ref[...] = _rms(x_ref[...], g_ref[...]).astype(BF16)

    o_ref[...] = _dot(xn_ref[...], w_ref[...]).astype(o_ref.dtype)


def rms_matmul(x, gain, w, *, tm, tn, out_dtype=F32):
    s, k = x.shape
    n = w.shape[1]
    return pl.pallas_call(
        _rms_matmul_kernel,
        grid=(s // tm, n // tn),
        in_specs=[
            pl.BlockSpec((tm, k), lambda i, j: (i, 0)),
            pl.BlockSpec((1, k), lambda i, j: (0, 0)),
            pl.BlockSpec((k, tn), lambda i, j: (0, j)),
        ],
        out_specs=pl.BlockSpec((tm, tn), lambda i, j: (i, j)),
        out_shape=jax.ShapeDtypeStruct((s, n), out_dtype),
        scratch_shapes=[pltpu.VMEM((tm, k), BF16)],
        compiler_params=_params("parallel", "arbitrary"),
        name="rms_matmul",
    )(x, gain.reshape(1, k), w)


def _ffn_kernel(x_ref, gpre_ref, gpost_ref, wg_ref, wu_ref, wd_ref, o_ref, xn_ref, acc_ref):
    j = pl.program_id(1)

    @pl.when(j == 0)
    def _():
        xn_ref[...] = _rms(x_ref[...], gpre_ref[...]).astype(BF16)
        acc_ref[...] = jnp.zeros_like(acc_ref)

    xn = xn_ref[...]
    gate = _dot(xn, wg_ref[...])
    up = _dot(xn, wu_ref[...])
    hidden = (gate * jax.nn.sigmoid(gate) * up).astype(BF16)
    acc_ref[...] += _dot(hidden, wd_ref[...])

    @pl.when(j == pl.num_programs(1) - 1)
    def _():
        o_ref[...] = x_ref[...] + FFN_RES * _rms(acc_ref[...], gpost_ref[...])


def half_ffn(x, g_pre, g_post, w_gate, w_up, w_down, *, tm, tf):
    s, d = x.shape
    d_ff = w_gate.shape[1]
    return pl.pallas_call(
        _ffn_kernel,
        grid=(s // tm, d_ff // tf),
        in_specs=[
            pl.BlockSpec((tm, d), lambda i, j: (i, 0)),
            pl.BlockSpec((1, d), lambda i, j: (0, 0)),
            pl.BlockSpec((1, d), lambda i, j: (0, 0)),
            pl.BlockSpec((d, tf), lambda i, j: (0, j)),
            pl.BlockSpec((d, tf), lambda i, j: (0, j)),
            pl.BlockSpec((tf, d), lambda i, j: (j, 0)),
        ],
        out_specs=pl.BlockSpec((tm, d), lambda i, j: (i, 0)),
        out_shape=jax.ShapeDtypeStruct((s, d), F32),
        scratch_shapes=[pltpu.VMEM((tm, d), BF16), pltpu.VMEM((tm, d), F32)],
        compiler_params=_params("parallel", "arbitrary"),
        name="half_ffn",
    )(x, g_pre.reshape(1, d), g_post.reshape(1, d), w_gate, w_up, w_down)


def _out_proj_kernel(h_ref, mo_ref, x_ref, wh_ref, wm_ref, g_ref, o_ref):
    y = _dot(h_ref[...].astype(BF16), wh_ref[...]) + _dot(mo_ref[...].astype(BF16), wm_ref[...])
    o_ref[...] = x_ref[...] + _rms(y, g_ref[...])


def out_proj(main, mo, x, w, gain, *, tm):
    s, d = x.shape
    km, ko = main.shape[1], mo.shape[1]
    return pl.pallas_call(
        _out_proj_kernel,
        grid=(s // tm,),
        in_specs=[
            pl.BlockSpec((tm, km), lambda i: (i, 0)),
            pl.BlockSpec((tm, ko), lambda i: (i, 0)),
            pl.BlockSpec((tm, d), lambda i: (i, 0)),
            pl.BlockSpec((km, d), lambda i: (0, 0)),
            pl.BlockSpec((ko, d), lambda i: (0, 0)),
            pl.BlockSpec((1, d), lambda i: (0, 0)),
        ],
        out_specs=pl.BlockSpec((tm, d), lambda i: (i, 0)),
        out_shape=jax.ShapeDtypeStruct((s, d), F32),
        compiler_params=_params("parallel"),
        name="out_proj",
    )(main, mo, x, w[:km], w[km:], gain.reshape(1, d))


def _mem_attn_kernel(q_ref, kt_ref, v_ref, o_ref):
    scale = MEM_HD ** -0.5
    for h in range(MEM_HEADS):
        cols = slice(h * MEM_HD, (h + 1) * MEM_HD)
        s = _dot(q_ref[:, cols].astype(BF16), kt_ref[cols, :]) * scale
        e = jnp.exp(s - jnp.max(s, axis=-1, keepdims=True))
        p = e / jnp.sum(e, axis=-1, keepdims=True)
        o_ref[:, cols] = _dot(p.astype(BF16), v_ref[:, cols])


def mem_attention(proj, mq_block, mem_kt, mem_v, *, tm):
    s = proj.shape[0]
    m = mem_v.shape[0]
    return pl.pallas_call(
        _mem_attn_kernel,
        grid=(s // tm,),
        in_specs=[
            pl.BlockSpec((tm, MEM_W), lambda i: (i, mq_block)),
            pl.BlockSpec((MEM_W, m), lambda i: (0, 0)),
            pl.BlockSpec((m, MEM_W), lambda i: (0, 0)),
        ],
        out_specs=pl.BlockSpec((tm, MEM_W), lambda i: (i, 0)),
        out_shape=jax.ShapeDtypeStruct((s, MEM_W), F32),
        compiler_params=_params("parallel"),
        name="mem_attention",
    )(proj, mem_kt, mem_v)


def _log_sigmoid(x):
    return jnp.minimum(x, 0.0) - jnp.log1p(jnp.exp(-jnp.abs(x)))


def _mlstm_kernel(q_ref, kt_ref, v_ref, o_ref, grow_ref, gcol_ref, brow_ref, bcol_ref, hn_ref,
                  out_ref, c_ref, m_ref):
    chunk = grow_ref.shape[1]

    @pl.when(pl.program_id(0) == 0)
    def _():
        c_ref[...] = jnp.zeros_like(c_ref)
        m_ref[...] = jnp.zeros_like(m_ref)

    grow = grow_ref[...] + brow_ref[...]
    gcol = gcol_ref[...] + bcol_ref[...]
    r_idx = lax.broadcasted_iota(jnp.int32, (chunk, chunk), 0)
    c_idx = lax.broadcasted_iota(jnp.int32, (chunk, chunk), 1)
    causal = c_idx <= r_idx
    b_rows = _exact_dot(_log_sigmoid(grow), (r_idx <= c_idx).astype(BF16))
    b_cols = _exact_dot_left(causal.astype(BF16), _log_sigmoid(gcol))
    ones_col = (lax.broadcasted_iota(jnp.int32, (chunk, LANES), 1) == 0).astype(BF16)

    for h in range(ML_HEADS):
        b_r = b_rows[ML_HEADS + h:ML_HEADS + h + 1, :]
        i_r = grow[h:h + 1, :]
        b_c = b_cols[:, ML_HEADS + h:ML_HEADS + h + 1]
        m_prev = m_ref[h:h + 1, 0:1]
        dmat = jnp.where(causal, b_c - b_r + i_r, -jnp.inf)
        m_inter = b_c + m_prev
        m_vec = jnp.maximum(jnp.max(dmat, axis=-1, keepdims=True), m_inter)
        q = q_ref[h].astype(BF16)
        kt = kt_ref[h * ML_DQK:(h + 1) * ML_DQK, :] * (ML_DQK ** -0.5)
        smat = _dot(q, kt.astype(BF16)) * jnp.exp(dmat - m_vec)
        inter = jnp.exp(m_inter - m_vec)
        vcols = slice(h * ML_DV, (h + 1) * ML_DV)
        v_aug = jnp.concatenate([v_ref[:, vcols].astype(BF16), ones_col], axis=1)
        c_prev = c_ref[h]
        num_aug = _dot(smat.astype(BF16), v_aug) + inter * _dot(q, c_prev.astype(BF16))
        den = num_aug[:, ML_DV:ML_DV + 1]
        hval = num_aug[:, :ML_DV] / jnp.maximum(jnp.abs(den), jnp.exp(-m_vec))
        m_last = m_vec[chunk - 1:chunk, :]
        b_last = b_c[chunk - 1:chunk, :]
        w_r = jnp.exp(b_last - b_r + i_r - m_last)
        decay = jnp.exp(b_last + m_prev - m_last)
        c_ref[h] = decay * c_prev + _dot((kt * w_r).astype(BF16), v_aug)
        m_ref[h:h + 1, :] = jnp.broadcast_to(m_last, (1, LANES))
        out_ref[:, vcols] = _rms(hval, hn_ref[:, vcols]) * jax.nn.sigmoid(o_ref[:, vcols])


def mlstm_heads(q, kt, proj, v_block, o_block, gates_row, gates_col, bias, head_norm):
    s = proj.shape[0]
    chunk = ML_CHUNK
    h2 = 2 * ML_HEADS
    return pl.pallas_call(
        _mlstm_kernel,
        grid=(s // chunk,),
        in_specs=[
            pl.BlockSpec((ML_HEADS, chunk, ML_DQK), lambda c: (0, c, 0)),
            pl.BlockSpec((ML_HEADS * ML_DQK, chunk), lambda c: (0, c)),
            pl.BlockSpec((chunk, MAIN_W), lambda c: (c, v_block)),
            pl.BlockSpec((chunk, MAIN_W), lambda c: (c, o_block)),
            pl.BlockSpec((h2, chunk), lambda c: (0, c)),
            pl.BlockSpec((chunk, h2), lambda c: (c, 0)),
            pl.BlockSpec((h2, 1), lambda c: (0, 0)),
            pl.BlockSpec((1, h2), lambda c: (0, 0)),
            pl.BlockSpec((1, MAIN_W), lambda c: (0, 0)),
        ],
        out_specs=pl.BlockSpec((chunk, MAIN_W), lambda c: (c, 0)),
        out_shape=jax.ShapeDtypeStruct((s, MAIN_W), F32),
        scratch_shapes=[pltpu.VMEM((ML_HEADS, ML_DQK, ML_DV + LANES), F32), pltpu.VMEM((8, LANES), F32)],
        compiler_params=_params("arbitrary"),
        name="mlstm",
    )(q, kt, proj, proj, gates_row, gates_col, bias.reshape(h2, 1), bias.reshape(1, h2),
      head_norm.reshape(1, MAIN_W))


def _rope(x, cos2, sin2):
    return x * cos2 + pltpu.roll(x, NSA_HD // 2, axis=1) * sin2


def _kv_prep_kernel(kv_ref, cos_ref, sin_ref, kslc_ref, vslc_ref, kwin_ref, vwin_ref):
    cos2 = cos_ref[...]
    sin2 = sin_ref[...]
    for g in range(NSA_KV):
        def col(c):
            start = (c * NSA_KV + g) * NSA_HD
            return kv_ref[:, start:start + NSA_HD]
        kslc_ref[g] = _rope(col(2), cos2, sin2).astype(BF16)
        vslc_ref[g] = col(3).astype(BF16)
        kwin_ref[g] = _rope(col(4), cos2, sin2).astype(BF16)
        vwin_ref[g] = col(5).astype(BF16)


def kv_prep(kv, cos2, sin2, *, tm):
    s, w = kv.shape
    out = jax.ShapeDtypeStruct((NSA_KV, s, NSA_HD), BF16)
    ospec = pl.BlockSpec((NSA_KV, tm, NSA_HD), lambda i: (0, i, 0))
    return pl.pallas_call(
        _kv_prep_kernel,
        grid=(s // tm,),
        in_specs=[
            pl.BlockSpec((tm, w), lambda i: (i, 0)),
            pl.BlockSpec((tm, NSA_HD), lambda i: (i, 0)),
            pl.BlockSpec((tm, NSA_HD), lambda i: (i, 0)),
        ],
        out_specs=[ospec] * 4,
        out_shape=[out] * 4,
        compiler_params=_params("parallel"),
        name="kv_prep",
    )(kv, cos2, sin2)


def _compress_kernel(a_ref, b_ref, pa_ref, pb_ref, w1a_ref, w1b_ref, w2_ref, o_ref):
    xa = (a_ref[...] + pa_ref[...]).astype(BF16)
    xb = (b_ref[...] + pb_ref[...]).astype(BF16)
    hidden = _dot(xa, w1a_ref[...]) + _dot(xb, w1b_ref[...])
    act = jax.nn.gelu(hidden, approximate=True)
    o_ref[...] = _dot(act.astype(BF16), w2_ref[...])


def compress(seg_a, seg_b, pos_a, pos_b, w1a, w1b, w2, *, tm):
    n4, n, kw = seg_a.shape
    hid = w1a.shape[2]
    wmap = lambda cg, i: (cg // NSA_KV, 0, 0)
    return pl.pallas_call(
        _compress_kernel,
        grid=(n4, n // tm),
        in_specs=[
            pl.BlockSpec((None, tm, kw), lambda cg, i: (cg, i, 0)),
            pl.BlockSpec((None, tm, kw), lambda cg, i: (cg, i, 0)),
            pl.BlockSpec((None, 1, kw), wmap),
            pl.BlockSpec((None, 1, kw), wmap),
            pl.BlockSpec((None, kw, hid), wmap),
            pl.BlockSpec((None, kw, hid), wmap),
            pl.BlockSpec((None, hid, NSA_HD), wmap),
        ],
        out_specs=pl.BlockSpec((None, tm, NSA_HD), lambda cg, i: (cg, i, 0)),
        out_shape=jax.ShapeDtypeStruct((n4, n, NSA_HD), F32),
        compiler_params=_params("parallel", "parallel"),
        name="compress",
    )(seg_a, seg_b, pos_a, pos_b, w1a, w1b, w2)


def _rope_t(x, cos2t, sin2t):
    half = NSA_HD // 2
    return x * cos2t + jnp.concatenate([x[half:], x[:half]], axis=0) * sin2t


def _col_max(chunks):
    return jnp.max(functools.reduce(jnp.maximum, chunks), axis=0, keepdims=True)


def _nsa_kernel(q_ref, cos_ref, sin_ref, gl_ref, gb_ref, kc_ref, vct_ref, kslc_ref, vslct_ref, kwin_ref, vwint_ref,
                o_ref, qt_ref, qaug_ref, sel_ref, p_ref, acc_ref, s_ref, m_ref):
    qi = pl.program_id(1)
    qb = Q_BLOCK
    n_cmp = kc_ref.shape[0]
    n_sel = n_cmp // 4
    t0 = qi * qb
    t_row = t0 + lax.broadcasted_iota(jnp.int32, (1, qb), 1)
    hcols = [slice(j * qb, (j + 1) * qb) for j in range(NSA_GROUP)]

    cos2t = cos_ref[...]
    sin2t = sin_ref[...]
    for j in range(NSA_GROUP):
        qt = q_ref[:, j * NSA_HD:(j + 1) * NSA_HD].T * (NSA_HD ** -0.5 * LOG2E)
        qt_ref[:, hcols[j]] = qt.astype(BF16)
        qaug_ref[0:NSA_HD, hcols[j]] = _rope_t(qt, cos2t, sin2t).astype(BF16)

    s_c = _dot(kc_ref[...], qt_ref[...])
    blk = lax.broadcasted_iota(jnp.int32, (n_sel, qb), 0)
    cmp_bias = [jnp.where(SLC_LEN * blk + (CMP_STRIDE * r + CMP_LEN - 1) <= t_row, 0.0, NEG) for r in range(4)]
    has_valid = (t_row >= CMP_LEN - 1).astype(F32)
    imp = [jnp.zeros((n_sel, qb), F32) for _ in range(4)]
    for j in range(NSA_GROUP):
        sb = [s_c[r * n_sel:(r + 1) * n_sel, hcols[j]] + cmp_bias[r] for r in range(4)]
        m = _col_max(sb)
        e = [jnp.exp2(x - m) for x in sb]
        den = functools.reduce(jnp.add, [jnp.sum(x, axis=0, keepdims=True) for x in e])
        rinv = has_valid / jnp.maximum(den, TINY)
        for r in range(4):
            p = e[r] * rinv
            imp[r] = imp[r] + p
            p_ref[r * n_sel:(r + 1) * n_sel, hcols[j]] = p.astype(BF16)
    o_cmp = _dot(vct_ref[...], p_ref[0:n_cmp, :])

    imp3_prev = jnp.where(blk == 0, 0.0, pltpu.roll(imp[3], 1, axis=0))
    p_slc = ((((((imp[0] + imp3_prev) + imp[1]) + imp[0]) + imp[2]) + imp[1]) + imp[3]) + imp[2]
    cur = jnp.right_shift(t_row, 6)
    forced = (blk == 0) | (blk == cur) | (blk == cur - 1)
    score = jnp.where(blk > cur, NEG, jnp.where(forced, BIG, p_slc))
    blk_f = blk.astype(F32)
    sel = jnp.zeros((n_sel, qb), F32)
    for _ in range(SLC_TOPK):
        top = jnp.max(score, axis=0, keepdims=True)
        first = jnp.min(jnp.where(score == top, blk_f, float(n_sel)), axis=0, keepdims=True)
        hit = blk_f == first
        sel = jnp.where(hit, 1.0, sel)
        score = jnp.where(hit, REMOVED, score)
    sel_ref[...] = sel

    acc_ref[...] = jnp.zeros_like(acc_ref)
    m_ref[...] = jnp.full_like(m_ref, REMOVED)
    tiles_per_window = LANES * SLC_LEN // SLC_TILE
    kpos0 = lax.broadcasted_iota(jnp.int32, (SLC_TILE, qb), 0)

    def scores(kt, slot):
        first = pl.multiple_of((kt // tiles_per_window) * LANES, LANES)
        unselected = ((sel_ref[pl.ds(first, LANES), :] - 1.0) * BIG).astype(BF16)
        for j in range(NSA_GROUP):
            qaug_ref[NSA_HD:NSA_HD + LANES, hcols[j]] = unselected
        start = pl.multiple_of(kt * SLC_TILE, SLC_TILE)
        s_ref[slot] = _dot(kslc_ref[pl.ds(start, SLC_TILE), :], qaug_ref[...])

    def weights(kt, slot, causal):
        if causal:
            future = jnp.where(kpos0 + kt * SLC_TILE <= t_row, 0.0, NEG)
        alpha = []
        for j in range(NSA_GROUP):
            s = s_ref[slot, :, hcols[j]]
            if causal:
                s = s + future
            m_old = m_ref[0:1, hcols[j]]
            m_j = jnp.maximum(m_old, jnp.max(s, axis=0, keepdims=True))
            p_ref[slot * SLC_TILE:(slot + 1) * SLC_TILE, hcols[j]] = jnp.exp2(s - m_j).astype(BF16)
            m_ref[0:1, hcols[j]] = m_j
            alpha.append(jnp.exp2(m_old - m_j))
        return jnp.concatenate(alpha, axis=1)

    def weighted_values(kt, slot):
        start = pl.multiple_of(kt * SLC_TILE, SLC_TILE)
        return _dot(vslct_ref[:, pl.ds(start, SLC_TILE)], p_ref[slot * SLC_TILE:(slot + 1) * SLC_TILE, :])

    def absorb(kt, slot, causal):
        alpha = weights(kt, slot, causal)
        acc_ref[...] = acc_ref[...] * alpha + weighted_values(kt, slot)

    last = (t0 + qb - 1) // SLC_TILE
    scores(0, 0)

    def pair(i, carry):
        scores(2 * i + 1, 1)
        absorb(2 * i, 0, False)
        scores(jnp.minimum(2 * i + 2, last), 0)
        absorb(2 * i + 1, 1, False)
        return carry

    lax.fori_loop(0, last // 2, pair, 0)

    @pl.when(last % 2 == 0)
    def _():
        absorb(last, 0, True)

    @pl.when(last % 2 == 1)
    def _():
        scores(last, 1)
        absorb(last - 1, 0, False)
        absorb(last, 1, True)

    acc = acc_ref[...]
    o_slc = acc[:NSA_HD] / acc[NSA_HD:NSA_HD + 1]

    span = qb + WIN
    start = pl.multiple_of(jnp.maximum(t0 - WIN, 0), qb)
    s_w = _dot(kwin_ref[pl.ds(start, span), :], qaug_ref[0:NSA_HD, :])
    spos0 = lax.broadcasted_iota(jnp.int32, (qb, qb), 0)
    win_bias = []
    for c in range(span // qb):
        spos = spos0 + (start + c * qb)
        win_bias.append(jnp.where((spos <= t_row) & (spos > t_row - WIN), 0.0, NEG))
    for j in range(NSA_GROUP):
        sb = [s_w[c * qb:(c + 1) * qb, hcols[j]] + win_bias[c] for c in range(span // qb)]
        m = _col_max(sb)
        for c in range(span // qb):
            p_ref[c * qb:(c + 1) * qb, hcols[j]] = jnp.exp2(sb[c] - m).astype(BF16)
    pw = _dot(vwint_ref[:, pl.ds(start, span)], p_ref[0:span, :])
    o_win = pw[:NSA_HD] / pw[NSA_HD:NSA_HD + 1]

    gates_t = jax.nn.sigmoid(gl_ref[...] + gb_ref[...]).T
    for j in range(NSA_GROUP):
        o_t = (gates_t[3 * j:3 * j + 1] * o_cmp[:, hcols[j]]
               + gates_t[3 * j + 1:3 * j + 2] * o_slc[:, hcols[j]]
               + gates_t[3 * j + 2:3 * j + 3] * o_win[:, hcols[j]])
        o_ref[:, j * NSA_HD:(j + 1) * NSA_HD] = o_t.T


def nsa_attention(proj, cos2t, sin2t, gate_logits, gate_bias, kc, vct, kslc, vslct, kwin, vwint):
    s = proj.shape[0]
    n_cmp = kc.shape[1]
    rows_aug = NSA_HD + ONES_ROWS
    width = NSA_GROUP * Q_BLOCK
    qspec = pl.BlockSpec((Q_BLOCK, GROUP_W), lambda g, i: (i, g))
    tspec = pl.BlockSpec((NSA_HD, Q_BLOCK), lambda g, i: (0, i))

    def resident(shape):
        return pl.BlockSpec((None,) + shape, lambda g, i: (g, 0, 0), pipeline_mode=pl.Buffered(1))

    return pl.pallas_call(
        _nsa_kernel,
        grid=(NSA_KV, s // Q_BLOCK),
        in_specs=[
            qspec, tspec, tspec,
            pl.BlockSpec((Q_BLOCK, LANES), lambda g, i: (i, g)),
            pl.BlockSpec((1, LANES), lambda g, i: (0, g)),
            resident((n_cmp, NSA_HD)), resident((NSA_HD, n_cmp)),
            resident((s, NSA_HD + LANES)), resident((rows_aug, s)),
            resident((s, NSA_HD)), resident((rows_aug, s)),
        ],
        out_specs=qspec,
        out_shape=jax.ShapeDtypeStruct((s, NSA_KV * GROUP_W), F32),
        scratch_shapes=[
            pltpu.VMEM((NSA_HD, width), BF16),
            pltpu.VMEM((NSA_HD + LANES, width), BF16),
            pltpu.VMEM((n_cmp // 4, Q_BLOCK), F32),
            pltpu.VMEM((max(n_cmp, Q_BLOCK + WIN, 2 * SLC_TILE), width), BF16),
            pltpu.VMEM((rows_aug, width), F32),
            pltpu.VMEM((2, SLC_TILE, width), F32),
            pltpu.VMEM((8, width), F32),
        ],
        compiler_params=_params("parallel", "arbitrary"),
        name="nsa_attention",
    )(proj, cos2t, sin2t, gate_logits, gate_bias, kc, vct, kslc, vslct, kwin, vwint)


def _row_tile(s, want):
    return want if s % want == 0 else s


def _rope_tables(seq):
    inv = ROPE_THETA ** (-jnp.arange(0, NSA_HD, 2, dtype=F32) / NSA_HD)
    ang = jnp.arange(seq, dtype=F32)[:, None] * inv[None, :]
    cos, sin = jnp.cos(ang), jnp.sin(ang)
    return jnp.concatenate([cos, cos], axis=1), jnp.concatenate([-sin, sin], axis=1)


def _pad_cols(a, width):
    return jnp.pad(a, ((0, 0), (0, width - a.shape[1])))


def _mem_kv(mem, gain, w_kv):
    kv = rms_matmul(mem, gain, w_kv.astype(BF16), tm=mem.shape[0], tn=MEM_W, out_dtype=BF16)
    return kv[:, :MEM_W].T, kv[:, MEM_W:]


def _mlstm_layer_heads(x, gain, w_in, gate_bias, head_norm, mem_kt, mem_v):
    s = x.shape[0]
    qk_w = ML_HEADS * ML_DQK
    w_q, w_k, w_v, w_o, w_g, w_mq = jnp.split(
        w_in, [qk_w, 2 * qk_w, 2 * qk_w + MAIN_W, 2 * qk_w + 2 * MAIN_W, 2 * qk_w + 2 * MAIN_W + 2 * ML_HEADS], axis=1)
    w_main = jnp.concatenate([w_q, w_k, w_v, w_o, w_mq], axis=1).astype(BF16)
    tm = _row_tile(s, 1024)
    proj = rms_matmul(x, gain, w_main, tm=tm, tn=512)
    gates = rms_matmul(x, gain, _pad_cols(w_g, LANES).astype(BF16), tm=tm, tn=LANES)[:, :2 * ML_HEADS]
    q_heads = proj[:, :qk_w].reshape(s, ML_HEADS, ML_DQK).transpose(1, 0, 2)
    k_t = proj[:, qk_w:2 * qk_w].T
    h = mlstm_heads(q_heads, k_t, proj, 2 * qk_w // MAIN_W, 2 * qk_w // MAIN_W + 1, gates.T, gates, gate_bias,
                    head_norm)
    mo = mem_attention(proj, (2 * qk_w + 2 * MAIN_W) // MEM_W, mem_kt, mem_v, tm=_row_tile(s, 512))
    return h, mo


def _with_ones_rows(v):
    g, s, _ = v.shape
    return jnp.concatenate([v.transpose(0, 2, 1), jnp.ones((g, ONES_ROWS, s), v.dtype)], axis=1)


def _shared_kv(x, kv_norm, w_kv, cmp_pos, cmp_w1, cmp_w2, cos2, sin2):
    s = x.shape[0]
    n_seg = s // CMP_STRIDE
    n_sel = s // SLC_LEN
    seg_w = CMP_STRIDE * NSA_HD
    kv = rms_matmul(x, kv_norm, w_kv.astype(BF16), tm=_row_tile(s, 1024), tn=512)
    kslc, vslc, kwin, vwin = kv_prep(kv, cos2, sin2, tm=_row_tile(s, 512))
    segs = kv[:, :2 * NSA_KV * NSA_HD].reshape(n_seg, CMP_STRIDE, 2 * NSA_KV, NSA_HD)
    segs = segs.transpose(2, 0, 1, 3).reshape(2 * NSA_KV, n_seg, seg_w)
    pos = cmp_pos.reshape(2, 2, 1, seg_w)
    w1 = cmp_w1.astype(BF16)
    cmp = compress(segs, jnp.roll(segs, -1, axis=1), pos[:, 0], pos[:, 1], w1[:, :seg_w], w1[:, seg_w:],
                   cmp_w2.astype(BF16), tm=_row_tile(n_seg, 256))
    cmp = cmp.reshape(2 * NSA_KV, n_sel, 4, NSA_HD).transpose(0, 2, 1, 3).reshape(2 * NSA_KV, n_seg, NSA_HD)
    cmp = cmp.astype(BF16)
    block_in_window = (jnp.arange(s, dtype=jnp.int32) // SLC_LEN) % LANES
    onehot = (block_in_window[:, None] == jnp.arange(LANES, dtype=jnp.int32)[None, :]).astype(BF16)
    kslc_aug = jnp.concatenate([kslc, jnp.broadcast_to(onehot[None], (NSA_KV, s, LANES))], axis=2)
    return dict(kc=cmp[:NSA_KV], vct=cmp[NSA_KV:].transpose(0, 2, 1), kslc=kslc_aug, vslct=_with_ones_rows(vslc),
                kwin=kwin, vwint=_with_ones_rows(vwin))


def _nsa_layer_heads(x, gain, w_in, gate_bias, shared, cos2t, sin2t, mem_kt, mem_v):
    s = x.shape[0]
    n_gate = 3 * NSA_GROUP
    w_q, w_g, w_mq = jnp.split(w_in, [MAIN_W, MAIN_W + 3 * NSA_HEADS], axis=1)
    w_main = jnp.concatenate([w_q, w_mq], axis=1).astype(BF16)
    w_gate = jnp.concatenate([_pad_cols(w_g[:, g * n_gate:(g + 1) * n_gate], LANES) for g in range(NSA_KV)], axis=1)
    b_gate = jnp.concatenate(
        [_pad_cols(gate_bias[None, g * n_gate:(g + 1) * n_gate], LANES) for g in range(NSA_KV)], axis=1)
    tm = _row_tile(s, 1024)
    proj = rms_matmul(x, gain, w_main, tm=tm, tn=512)
    gate_logits = rms_matmul(x, gain, w_gate.astype(BF16), tm=tm, tn=LANES)
    o = nsa_attention(proj, cos2t, sin2t, gate_logits, b_gate, shared["kc"], shared["vct"], shared["kslc"],
                      shared["vslct"], shared["kwin"], shared["vwint"])
    mo = mem_attention(proj, MAIN_W // MEM_W, mem_kt, mem_v, tm=_row_tile(s, 512))
    return o, mo


def kernel(x, mem, norm_gains, ffn_w_gate, ffn_w_up, ffn_w_down, mem_norm, mem_w_kv, w_out, a_w_in, a_gate_bias,
           a_head_norm, kv_norm, w_kv, cmp_pos, cmp_w1, cmp_w2, b_w_in, b_gate_bias):
    batch, seq, d_model = x.shape
    depth = norm_gains.shape[0]
    n_a = a_w_in.shape[0]
    assert seq % (SLC_LEN * LANES) == 0 and seq % ML_CHUNK == 0, "sequence must tile the selection-block lanes"
    cos2, sin2 = _rope_tables(seq)
    cos2t, sin2t = cos2.T, sin2.T
    tm_ffn = _row_tile(seq, 512)

    def ffn(xb, layer, half, g_pre, g_post):
        return half_ffn(xb, g_pre, g_post, ffn_w_gate[layer, half].astype(BF16), ffn_w_up[layer, half].astype(BF16),
                        ffn_w_down[layer, half].astype(BF16), tm=tm_ffn, tf=512)

    outs = []
    for b in range(batch):
        xb = x[b]
        shared = None
        for layer in range(depth):
            if layer == n_a:
                shared = _shared_kv(xb, kv_norm, w_kv, cmp_pos, cmp_w1, cmp_w2, cos2, sin2)
            g = norm_gains[layer]
            xb = ffn(xb, layer, 0, g[0], g[1])
            mem_kt, mem_v = _mem_kv(mem[b], mem_norm[layer], mem_w_kv[layer])
            if layer < n_a:
                main, mo = _mlstm_layer_heads(xb, g[2], a_w_in[layer], a_gate_bias[layer], a_head_norm[layer],
                                              mem_kt, mem_v)
            else:
                lb = layer - n_a
                main, mo = _nsa_layer_heads(xb, g[2], b_w_in[lb], b_gate_bias[lb], shared, cos2t, sin2t, mem_kt, mem_v)
            xb = out_proj(main, mo, xb, w_out[layer].astype(BF16), g[3], tm=_row_tile(seq, 256))
            xb = ffn(xb, layer, 1, g[4], g[5])
        outs.append(xb)
    return jnp.stack(outs, axis=0)
```

```python
import functools

import jax
import jax.numpy as jnp
from jax import lax
from jax.experimental import pallas as pl
from jax.experimental.pallas import tpu as pltpu

F32 = jnp.float32
BF16 = jnp.bfloat16

MEM_HEADS = 4
MEM_HD = 128
MEM_W = MEM_HEADS * MEM_HD
ML_HEADS = 4
ML_DV = 384
ML_DQK = 192
MAIN_W = ML_HEADS * ML_DV
NSA_HEADS = 12
NSA_HD = 128
NSA_KV = 2
NSA_GROUP = NSA_HEADS // NSA_KV
GROUP_W = NSA_GROUP * NSA_HD
CMP_LEN = 32
CMP_STRIDE = 16
SLC_LEN = 64
SLC_TOPK = 16
WIN = 512
Q_BLOCK = 128
FFN_RES = 0.5
ROPE_THETA = 10000.0
EPS = 1e-6
NEG = -1e30
BIG = 1e30
REMOVED = -3e38
LOG2E = 1.4426950408889634
TINY = 1e-30

LANES = 128
VMEM_LIMIT_BYTES = 56 * 1024 * 1024

ML_CHUNK = 256
SLC_TILE = 512
ONES_ROWS = 16
SEL_CHUNK = 64


def _params(*semantics):
    return pltpu.CompilerParams(dimension_semantics=semantics, vmem_limit_bytes=VMEM_LIMIT_BYTES)


def _rms(x, gain):
    return x * lax.rsqrt(jnp.mean(x * x, axis=-1, keepdims=True) + EPS) * gain


def _dot(a, b):
    return jnp.dot(a, b, preferred_element_type=F32)


def _exact_dot(a, b01):
    a1 = a.astype(BF16)
    r1 = a - a1.astype(F32)
    a2 = r1.astype(BF16)
    a3 = (r1 - a2.astype(F32)).astype(BF16)
    return _dot(a1, b01) + _dot(a2, b01) + _dot(a3, b01)


def _exact_dot_left(b01, a):
    a1 = a.astype(BF16)
    r1 = a - a1.astype(F32)
    a2 = r1.astype(BF16)
    a3 = (r1 - a2.astype(F32)).astype(BF16)
    return _dot(b01, a1) + _dot(b01, a2) + _dot(b01, a3)


def _rms_matmul_kernel(x_ref, g_ref, w_ref, o_ref, xn_ref):
    @pl.when(pl.program_id(1) == 0)
    def _():
        xn_ref[...] = _rms(x_ref[...], g_ref[...]).astype(BF16)

    o_ref[...] = _dot(xn_ref[...], w_ref[...]).astype(o_ref.dtype)


def _rms_matmul_side_kernel(x_ref, g_ref, w_ref, wside_ref, o_ref, oside_ref, xn_ref):
    @pl.when(pl.program_id(1) == 0)
    def _():
        xn_ref[...] = _rms(x_ref[...], g_ref[...]).astype(BF16)
        oside_ref[...] = _dot(xn_ref[...], wside_ref[...])

    o_ref[...] = _dot(xn_ref[...], w_ref[...]).astype(o_ref.dtype)


def rms_matmul(x, gain, w, w_side=None, *, tm, tn, out_dtype=F32):
    s, k = x.shape
    n = w.shape[1]
    in_specs = [
        pl.BlockSpec((tm, k), lambda i, j: (i, 0)),
        pl.BlockSpec((1, k), lambda i, j: (0, 0)),
        pl.BlockSpec((k, tn), lambda i, j: (0, j)),
    ]
    out_specs = pl.BlockSpec((tm, tn), lambda i, j: (i, j))
    out_shape = jax.ShapeDtypeStruct((s, n), out_dtype)
    args = (x, gain.reshape(1, k), w)
    body = _rms_matmul_kernel
    if w_side is not None:
        n_side = w_side.shape[1]
        in_specs.append(pl.BlockSpec((k, n_side), lambda i, j: (0, 0)))
        out_specs = [out_specs, pl.BlockSpec((tm, n_side), lambda i, j: (i, 0))]
        out_shape = [out_shape, jax.ShapeDtypeStruct((s, n_side), F32)]
        args = args + (w_side,)
        body = _rms_matmul_side_kernel
    return pl.pallas_call(
        body,
        grid=(s // tm, n // tn),
        in_specs=in_specs,
        out_specs=out_specs,
        out_shape=out_shape,
        scratch_shapes=[pltpu.VMEM((tm, k), BF16)],
        compiler_params=_params("parallel", "arbitrary"),
        name="rms_matmul",
    )(*args)


def _ffn_kernel(x_ref, gpre_ref, gpost_ref, wg_ref, wu_ref, wd_ref, o_ref, xn_ref):
    j = pl.program_id(1)

    @pl.when(j == 0)
    def _():
        xn_ref[...] = _rms(x_ref[...], gpre_ref[...]).astype(BF16)
        o_ref[...] = jnp.zeros_like(o_ref)

    xn = xn_ref[...]
    gate = _dot(xn, wg_ref[...])
    up = _dot(xn, wu_ref[...])
    hidden = (gate * jax.nn.sigmoid(gate) * up).astype(BF16)
    o_ref[...] += _dot(hidden, wd_ref[...])

    @pl.when(j == pl.num_programs(1) - 1)
    def _():
        o_ref[...] = x_ref[...] + FFN_RES * _rms(o_ref[...], gpost_ref[...])


def half_ffn(x, g_pre, g_post, w_gate, w_up, w_down, *, tm, tf):
    s, d = x.shape
    d_ff = w_gate.shape[1]
    return pl.pallas_call(
        _ffn_kernel,
        grid=(s // tm, d_ff // tf),
        in_specs=[
            pl.BlockSpec((tm, d), lambda i, j: (i, 0)),
            pl.BlockSpec((1, d), lambda i, j: (0, 0)),
            pl.BlockSpec((1, d), lambda i, j: (0, 0)),
            pl.BlockSpec((d, tf), lambda i, j: (0, j)),
            pl.BlockSpec((d, tf), lambda i, j: (0, j)),
            pl.BlockSpec((tf, d), lambda i, j: (j, 0)),
        ],
        out_specs=pl.BlockSpec((tm, d), lambda i, j: (i, 0)),
        out_shape=jax.ShapeDtypeStruct((s, d), F32),
        scratch_shapes=[pltpu.VMEM((tm, d), BF16)],
        compiler_params=_params("parallel", "arbitrary"),
        name="half_ffn",
    )(x, g_pre.reshape(1, d), g_post.reshape(1, d), w_gate, w_up, w_down)


def _out_proj_kernel(h_ref, mo_ref, x_ref, wh_ref, wm_ref, g_ref, o_ref):
    y = _dot(h_ref[...].astype(BF16), wh_ref[...]) + _dot(mo_ref[...].astype(BF16), wm_ref[...])
    o_ref[...] = x_ref[...] + _rms(y, g_ref[...])


def out_proj(main, mo, x, w, gain, *, tm):
    s, d = x.shape
    km, ko = main.shape[1], mo.shape[1]
    return pl.pallas_call(
        _out_proj_kernel,
        grid=(s // tm,),
        in_specs=[
            pl.BlockSpec((tm, km), lambda i: (i, 0)),
            pl.BlockSpec((tm, ko), lambda i: (i, 0)),
            pl.BlockSpec((tm, d), lambda i: (i, 0)),
            pl.BlockSpec((km, d), lambda i: (0, 0)),
            pl.BlockSpec((ko, d), lambda i: (0, 0)),
            pl.BlockSpec((1, d), lambda i: (0, 0)),
        ],
        out_specs=pl.BlockSpec((tm, d), lambda i: (i, 0)),
        out_shape=jax.ShapeDtypeStruct((s, d), F32),
        compiler_params=_params("parallel"),
        name="out_proj",
    )(main, mo, x, w[:km], w[km:], gain.reshape(1, d))


def _mem_attn_kernel(q_ref, kt_ref, v_ref, o_ref):
    scale = MEM_HD ** -0.5
    for h in range(MEM_HEADS):
        cols = slice(h * MEM_HD, (h + 1) * MEM_HD)
        s = _dot(q_ref[:, cols].astype(BF16), kt_ref[cols, :]) * scale
        e = jnp.exp(s - jnp.max(s, axis=-1, keepdims=True))
        p = e / jnp.sum(e, axis=-1, keepdims=True)
        o_ref[:, cols] = _dot(p.astype(BF16), v_ref[:, cols])


def mem_attention(proj, mq_block, mem_kt, mem_v, *, tm):
    s = proj.shape[0]
    m = mem_v.shape[0]
    return pl.pallas_call(
        _mem_attn_kernel,
        grid=(s // tm,),
        in_specs=[
            pl.BlockSpec((tm, MEM_W), lambda i: (i, mq_block)),
            pl.BlockSpec((MEM_W, m), lambda i: (0, 0)),
            pl.BlockSpec((m, MEM_W), lambda i: (0, 0)),
        ],
        out_specs=pl.BlockSpec((tm, MEM_W), lambda i: (i, 0)),
        out_shape=jax.ShapeDtypeStruct((s, MEM_W), F32),
        compiler_params=_params("parallel"),
        name="mem_attention",
    )(proj, mem_kt, mem_v)


def _log_sigmoid(x):
    return jnp.minimum(x, 0.0) - jnp.log1p(jnp.exp(-jnp.abs(x)))


def _mlstm_kernel(q_ref, kt_ref, v_ref, o_ref, grow_ref, gcol_ref, brow_ref, bcol_ref, hn_ref,
                  out_ref, c_ref, m_ref):
    chunk = grow_ref.shape[1]

    @pl.when(pl.program_id(0) == 0)
    def _():
        c_ref[...] = jnp.zeros_like(c_ref)
        m_ref[...] = jnp.zeros_like(m_ref)

    grow = grow_ref[...] + brow_ref[...]
    gcol = gcol_ref[...] + bcol_ref[...]
    r_idx = lax.broadcasted_iota(jnp.int32, (chunk, chunk), 0)
    c_idx = lax.broadcasted_iota(jnp.int32, (chunk, chunk), 1)
    causal = c_idx <= r_idx
    b_rows = _exact_dot(_log_sigmoid(grow), (r_idx <= c_idx).astype(BF16))
    b_cols = _exact_dot_left(causal.astype(BF16), _log_sigmoid(gcol))
    ones_col = (lax.broadcasted_iota(jnp.int32, (chunk, LANES), 1) == 0).astype(BF16)

    for h in range(ML_HEADS):
        b_r = b_rows[ML_HEADS + h:ML_HEADS + h + 1, :]
        i_r = grow[h:h + 1, :]
        b_c = b_cols[:, ML_HEADS + h:ML_HEADS + h + 1]
        m_prev = m_ref[h:h + 1, 0:1]
        dmat = jnp.where(causal, b_c - b_r + i_r, -jnp.inf)
        m_inter = b_c + m_prev
        m_vec = jnp.maximum(jnp.max(dmat, axis=-1, keepdims=True), m_inter)
        q = q_ref[h].astype(BF16)
        kt = kt_ref[h * ML_DQK:(h + 1) * ML_DQK, :] * (ML_DQK ** -0.5)
        smat = _dot(q, kt.astype(BF16)) * jnp.exp(dmat - m_vec)
        inter = jnp.exp(m_inter - m_vec)
        vcols = slice(h * ML_DV, (h + 1) * ML_DV)
        v_aug = jnp.concatenate([v_ref[:, vcols].astype(BF16), ones_col], axis=1)
        c_prev = c_ref[h]
        num_aug = _dot(smat.astype(BF16), v_aug) + inter * _dot(q, c_prev.astype(BF16))
        den = num_aug[:, ML_DV:ML_DV + 1]
        hval = num_aug[:, :ML_DV] / jnp.maximum(jnp.abs(den), jnp.exp(-m_vec))
        m_last = m_vec[chunk - 1:chunk, :]
        b_last = b_c[chunk - 1:chunk, :]
        w_r = jnp.exp(b_last - b_r + i_r - m_last)
        decay = jnp.exp(b_last + m_prev - m_last)
        c_ref[h] = decay * c_prev + _dot((kt * w_r).astype(BF16), v_aug)
        m_ref[h:h + 1, :] = jnp.broadcast_to(m_last, (1, LANES))
        out_ref[:, vcols] = _rms(hval, hn_ref[:, vcols]) * jax.nn.sigmoid(o_ref[:, vcols])


def mlstm_heads(q, kt, proj, v_block, o_block, gates_row, gates_col, bias, head_norm):
    s = proj.shape[0]
    chunk = ML_CHUNK
    h2 = 2 * ML_HEADS
    return pl.pallas_call(
        _mlstm_kernel,
        grid=(s // chunk,),
        in_specs=[
            pl.BlockSpec((ML_HEADS, chunk, ML_DQK), lambda c: (0, c, 0)),
            pl.BlockSpec((ML_HEADS * ML_DQK, chunk), lambda c: (0, c)),
            pl.BlockSpec((chunk, MAIN_W), lambda c: (c, v_block)),
            pl.BlockSpec((chunk, MAIN_W), lambda c: (c, o_block)),
            pl.BlockSpec((h2, chunk), lambda c: (0, c)),
            pl.BlockSpec((chunk, h2), lambda c: (c, 0)),
            pl.BlockSpec((h2, 1), lambda c: (0, 0)),
            pl.BlockSpec((1, h2), lambda c: (0, 0)),
            pl.BlockSpec((1, MAIN_W), lambda c: (0, 0)),
        ],
        out_specs=pl.BlockSpec((chunk, MAIN_W), lambda c: (c, 0)),
        out_shape=jax.ShapeDtypeStruct((s, MAIN_W), F32),
        scratch_shapes=[pltpu.VMEM((ML_HEADS, ML_DQK, ML_DV + LANES), F32), pltpu.VMEM((8, LANES), F32)],
        compiler_params=_params("arbitrary"),
        name="mlstm",
    )(q, kt, proj, proj, gates_row, gates_col, bias.reshape(h2, 1), bias.reshape(1, h2),
      head_norm.reshape(1, MAIN_W))


def _rope(x, cos2, sin2):
    return x * cos2 + pltpu.roll(x, NSA_HD // 2, axis=1) * sin2


def _kv_prep_kernel(kv_ref, cos_ref, sin_ref, kslc_ref, vslc_ref, kwin_ref, vwin_ref):
    cos2 = cos_ref[...]
    sin2 = sin_ref[...]
    for g in range(NSA_KV):
        def col(c):
            start = (c * NSA_KV + g) * NSA_HD
            return kv_ref[:, start:start + NSA_HD]
        kslc_ref[g] = _rope(col(2), cos2, sin2).astype(BF16)
        vslc_ref[g] = col(3).astype(BF16)
        kwin_ref[g] = _rope(col(4), cos2, sin2).astype(BF16)
        vwin_ref[g] = col(5).astype(BF16)


def kv_prep(kv, cos2, sin2, *, tm):
    s, w = kv.shape
    out = jax.ShapeDtypeStruct((NSA_KV, s, NSA_HD), BF16)
    ospec = pl.BlockSpec((NSA_KV, tm, NSA_HD), lambda i: (0, i, 0))
    return pl.pallas_call(
        _kv_prep_kernel,
        grid=(s // tm,),
        in_specs=[
            pl.BlockSpec((tm, w), lambda i: (i, 0)),
            pl.BlockSpec((tm, NSA_HD), lambda i: (i, 0)),
            pl.BlockSpec((tm, NSA_HD), lambda i: (i, 0)),
        ],
        out_specs=[ospec] * 4,
        out_shape=[out] * 4,
        compiler_params=_params("parallel"),
        name="kv_prep",
    )(kv, cos2, sin2)


def _compress_kernel(a_ref, b_ref, pa_ref, pb_ref, w1a_ref, w1b_ref, w2_ref, o_ref):
    xa = (a_ref[...] + pa_ref[...]).astype(BF16)
    xb = (b_ref[...] + pb_ref[...]).astype(BF16)
    hidden = _dot(xa, w1a_ref[...]) + _dot(xb, w1b_ref[...])
    act = jax.nn.gelu(hidden, approximate=True)
    o_ref[...] = _dot(act.astype(BF16), w2_ref[...])


def compress(seg_a, seg_b, pos_a, pos_b, w1a, w1b, w2, *, tm):
    n4, n, kw = seg_a.shape
    hid = w1a.shape[2]
    wmap = lambda cg, i: (cg // NSA_KV, 0, 0)
    return pl.pallas_call(
        _compress_kernel,
        grid=(n4, n // tm),
        in_specs=[
            pl.BlockSpec((None, tm, kw), lambda cg, i: (cg, i, 0)),
            pl.BlockSpec((None, tm, kw), lambda cg, i: (cg, i, 0)),
            pl.BlockSpec((None, 1, kw), wmap),
            pl.BlockSpec((None, 1, kw), wmap),
            pl.BlockSpec((None, kw, hid), wmap),
            pl.BlockSpec((None, kw, hid), wmap),
            pl.BlockSpec((None, hid, NSA_HD), wmap),
        ],
        out_specs=pl.BlockSpec((None, tm, NSA_HD), lambda cg, i: (cg, i, 0)),
        out_shape=jax.ShapeDtypeStruct((n4, n, NSA_HD), F32),
        compiler_params=_params("parallel", "parallel"),
        name="compress",
    )(seg_a, seg_b, pos_a, pos_b, w1a, w1b, w2)


def _rope_t(x, cos2t, sin2t):
    half = NSA_HD // 2
    return x * cos2t + jnp.concatenate([x[half:], x[:half]], axis=0) * sin2t


def _col_max(chunks):
    return jnp.max(functools.reduce(jnp.maximum, chunks), axis=0, keepdims=True)


def _nsa_kernel(q_ref, cos_ref, sin_ref, gl_ref, gb_ref, kc_ref, vct_ref, kslc_ref, vslct_ref, kwin_ref, vwint_ref,
                o_ref, qt_ref, qaug_ref, sel_ref, p_ref, acc_ref, s_ref, m_ref, ocmp_ref):
    qi = pl.program_id(1)
    qb = Q_BLOCK
    n_cmp = kc_ref.shape[0]
    n_sel = n_cmp // 4
    t0 = qi * qb
    t_row = t0 + lax.broadcasted_iota(jnp.int32, (1, qb), 1)
    hcols = [slice(j * qb, (j + 1) * qb) for j in range(NSA_GROUP)]

    cos2t = cos_ref[...]
    sin2t = sin_ref[...]
    for j in range(NSA_GROUP):
        qt = q_ref[:, j * NSA_HD:(j + 1) * NSA_HD].T * (NSA_HD ** -0.5 * LOG2E)
        qt_ref[:, hcols[j]] = qt.astype(BF16)
        qaug_ref[0:NSA_HD, hcols[j]] = _rope_t(qt, cos2t, sin2t).astype(BF16)

    def compressed_and_select(n_chunks):
        rows = 4 * SEL_CHUNK * n_chunks
        n_blk = SEL_CHUNK * n_chunks
        s_c = _dot(kc_ref[0:rows, :], qt_ref[...])
        blk_i = lax.broadcasted_iota(jnp.int32, (SEL_CHUNK, qb), 0)
        pieces = [(c, r) for c in range(n_chunks) for r in range(4)]

        def piece_rows(c, r):
            return slice((4 * c + r) * SEL_CHUNK, (4 * c + r + 1) * SEL_CHUNK)

        cmp_bias = {(c, r): jnp.where(SLC_LEN * (blk_i + SEL_CHUNK * c) + (CMP_STRIDE * r + CMP_LEN - 1) <= t_row,
                                      0.0, NEG) for c, r in pieces}
        has_valid = (t_row >= CMP_LEN - 1).astype(F32)
        imp = {cr: jnp.zeros((SEL_CHUNK, qb), F32) for cr in pieces}
        for j in range(NSA_GROUP):
            sb = {cr: s_c[piece_rows(*cr), hcols[j]] + cmp_bias[cr] for cr in pieces}
            m = _col_max(list(sb.values()))
            e = {cr: jnp.exp2(sb[cr] - m) for cr in pieces}
            den = functools.reduce(jnp.add, [jnp.sum(x, axis=0, keepdims=True) for x in e.values()])
            rinv = has_valid / jnp.maximum(den, TINY)
            for cr in pieces:
                p = e[cr] * rinv
                imp[cr] = imp[cr] + p
                p_ref[piece_rows(*cr), hcols[j]] = p.astype(BF16)
        ocmp_ref[...] = _dot(vct_ref[:, 0:rows], p_ref[0:rows, :])

        imp_r = [jnp.concatenate([imp[(c, r)] for c in range(n_chunks)], axis=0) for r in range(4)]
        blk = lax.broadcasted_iota(jnp.int32, (n_blk, qb), 0)
        imp3_prev = jnp.where(blk == 0, 0.0, pltpu.roll(imp_r[3], 1, axis=0))
        p_slc = ((((((imp_r[0] + imp3_prev) + imp_r[1]) + imp_r[0]) + imp_r[2]) + imp_r[1]) + imp_r[3]) + imp_r[2]
        cur = jnp.right_shift(t_row, 6)
        forced = (blk == 0) | (blk == cur) | (blk == cur - 1)
        score = jnp.where(blk > cur, NEG, jnp.where(forced, BIG, p_slc))
        blk_f = blk.astype(F32)
        sel = jnp.zeros((n_blk, qb), F32)
        for _ in range(SLC_TOPK):
            top = jnp.max(score, axis=0, keepdims=True)
            first = jnp.min(jnp.where(score == top, blk_f, float(n_blk)), axis=0, keepdims=True)
            hit = blk_f == first
            sel = jnp.where(hit, 1.0, sel)
            score = jnp.where(hit, REMOVED, score)
        sel_ref[0:n_blk, :] = sel
        if n_blk < n_sel:
            sel_ref[n_blk:n_sel, :] = jnp.zeros((n_sel - n_blk, qb), F32)

    chunks_needed = (t0 + qb - 1) // (SLC_LEN * SEL_CHUNK) + 1
    for n_chunks in range(1, n_sel // SEL_CHUNK + 1):
        pl.when(chunks_needed == n_chunks)(functools.partial(compressed_and_select, n_chunks))

    acc_ref[...] = jnp.zeros_like(acc_ref)
    m_ref[...] = jnp.full_like(m_ref, REMOVED)
    tiles_per_window = LANES * SLC_LEN // SLC_TILE
    kpos0 = lax.broadcasted_iota(jnp.int32, (SLC_TILE, qb), 0)

    def scores(kt, slot):
        first = pl.multiple_of((kt // tiles_per_window) * LANES, LANES)
        unselected = ((sel_ref[pl.ds(first, LANES), :] - 1.0) * BIG).astype(BF16)
        for j in range(NSA_GROUP):
            qaug_ref[NSA_HD:NSA_HD + LANES, hcols[j]] = unselected
        start = pl.multiple_of(kt * SLC_TILE, SLC_TILE)
        s_ref[slot] = _dot(kslc_ref[pl.ds(start, SLC_TILE), :], qaug_ref[...])

    def weights(kt, slot, causal):
        if causal:
            future = jnp.where(kpos0 + kt * SLC_TILE <= t_row, 0.0, NEG)
        alpha = []
        for j in range(NSA_GROUP):
            s = s_ref[slot, :, hcols[j]]
            if causal:
                s = s + future
            m_old = m_ref[0:1, hcols[j]]
            m_j = jnp.maximum(m_old, jnp.max(s, axis=0, keepdims=True))
            p_ref[slot * SLC_TILE:(slot + 1) * SLC_TILE, hcols[j]] = jnp.exp2(s - m_j).astype(BF16)
            m_ref[0:1, hcols[j]] = m_j
            alpha.append(jnp.exp2(m_old - m_j))
        return jnp.concatenate(alpha, axis=1)

    def weighted_values(kt, slot):
        start = pl.multiple_of(kt * SLC_TILE, SLC_TILE)
        return _dot(vslct_ref[:, pl.ds(start, SLC_TILE)], p_ref[slot * SLC_TILE:(slot + 1) * SLC_TILE, :])

    def absorb(kt, slot, causal):
        alpha = weights(kt, slot, causal)
        acc_ref[...] = acc_ref[...] * alpha + weighted_values(kt, slot)

    last = (t0 + qb - 1) // SLC_TILE
    scores(0, 0)

    def pair(i, carry):
        scores(2 * i + 1, 1)
        absorb(2 * i, 0, False)
        scores(jnp.minimum(2 * i + 2, last), 0)
        absorb(2 * i + 1, 1, False)
        return carry

    lax.fori_loop(0, last // 2, pair, 0)

    @pl.when(last % 2 == 0)
    def _():
        absorb(last, 0, True)

    @pl.when(last % 2 == 1)
    def _():
        scores(last, 1)
        absorb(last - 1, 0, False)
        absorb(last, 1, True)

    acc = acc_ref[...]
    o_slc = acc[:NSA_HD] / acc[NSA_HD:NSA_HD + 1]

    span = qb + WIN
    start = pl.multiple_of(jnp.maximum(t0 - WIN, 0), qb)
    s_w = _dot(kwin_ref[pl.ds(start, span), :], qaug_ref[0:NSA_HD, :])
    spos0 = lax.broadcasted_iota(jnp.int32, (qb, qb), 0)
    win_bias = []
    for c in range(span // qb):
        spos = spos0 + (start + c * qb)
        win_bias.append(jnp.where((spos <= t_row) & (spos > t_row - WIN), 0.0, NEG))
    for j in range(NSA_GROUP):
        sb = [s_w[c * qb:(c + 1) * qb, hcols[j]] + win_bias[c] for c in range(span // qb)]
        m = _col_max(sb)
        for c in range(span // qb):
            p_ref[c * qb:(c + 1) * qb, hcols[j]] = jnp.exp2(sb[c] - m).astype(BF16)
    pw = _dot(vwint_ref[:, pl.ds(start, span)], p_ref[0:span, :])
    o_win = pw[:NSA_HD] / pw[NSA_HD:NSA_HD + 1]

    gates_t = jax.nn.sigmoid(gl_ref[...] + gb_ref[...]).T
    for j in range(NSA_GROUP):
        o_t = (gates_t[3 * j:3 * j + 1] * ocmp_ref[:, hcols[j]]
               + gates_t[3 * j + 1:3 * j + 2] * o_slc[:, hcols[j]]
               + gates_t[3 * j + 2:3 * j + 3] * o_win[:, hcols[j]])
        o_ref[:, j * NSA_HD:(j + 1) * NSA_HD] = o_t.T


def nsa_attention(proj, cos2t, sin2t, gate_logits, gate_bias, kc, vct, kslc, vslct, kwin, vwint):
    s = proj.shape[0]
    n_cmp = kc.shape[1]
    rows_aug = NSA_HD + ONES_ROWS
    width = NSA_GROUP * Q_BLOCK
    qspec = pl.BlockSpec((Q_BLOCK, GROUP_W), lambda g, i: (i, g))
    tspec = pl.BlockSpec((NSA_HD, Q_BLOCK), lambda g, i: (0, i))

    def resident(shape):
        return pl.BlockSpec((None,) + shape, lambda g, i: (g, 0, 0), pipeline_mode=pl.Buffered(1))

    return pl.pallas_call(
        _nsa_kernel,
        grid=(NSA_KV, s // Q_BLOCK),
        in_specs=[
            qspec, tspec, tspec,
            pl.BlockSpec((Q_BLOCK, LANES), lambda g, i: (i, g)),
            pl.BlockSpec((1, LANES), lambda g, i: (0, g)),
            resident((n_cmp, NSA_HD)), resident((NSA_HD, n_cmp)),
            resident((s, NSA_HD + LANES)), resident((rows_aug, s)),
            resident((s, NSA_HD)), resident((rows_aug, s)),
        ],
        out_specs=qspec,
        out_shape=jax.ShapeDtypeStruct((s, NSA_KV * GROUP_W), F32),
        scratch_shapes=[
            pltpu.VMEM((NSA_HD, width), BF16),
            pltpu.VMEM((NSA_HD + LANES, width), BF16),
            pltpu.VMEM((n_cmp // 4, Q_BLOCK), F32),
            pltpu.VMEM((max(n_cmp, Q_BLOCK + WIN, 2 * SLC_TILE), width), BF16),
            pltpu.VMEM((rows_aug, width), F32),
            pltpu.VMEM((2, SLC_TILE, width), F32),
            pltpu.VMEM((8, width), F32),
            pltpu.VMEM((NSA_HD, width), F32),
        ],
        compiler_params=_params("parallel", "arbitrary"),
        name="nsa_attention",
    )(proj, cos2t, sin2t, gate_logits, gate_bias, kc, vct, kslc, vslct, kwin, vwint)


def _row_tile(s, want):
    return want if s % want == 0 else s


def _rope_tables(seq):
    inv = ROPE_THETA ** (-jnp.arange(0, NSA_HD, 2, dtype=F32) / NSA_HD)
    ang = jnp.arange(seq, dtype=F32)[:, None] * inv[None, :]
    cos, sin = jnp.cos(ang), jnp.sin(ang)
    return jnp.concatenate([cos, cos], axis=1), jnp.concatenate([-sin, sin], axis=1)


def _pad_cols(a, width):
    return jnp.pad(a, ((0, 0), (0, width - a.shape[1])))


def _mem_kv(mem, gain, w_kv):
    kv = rms_matmul(mem, gain, w_kv.astype(BF16), tm=mem.shape[0], tn=MEM_W, out_dtype=BF16)
    return kv[:, :MEM_W].T, kv[:, MEM_W:]


def _mlstm_layer_heads(x, gain, w_in, gate_bias, head_norm, mem_kt, mem_v):
    s = x.shape[0]
    qk_w = ML_HEADS * ML_DQK
    w_q, w_k, w_v, w_o, w_g, w_mq = jnp.split(
        w_in, [qk_w, 2 * qk_w, 2 * qk_w + MAIN_W, 2 * qk_w + 2 * MAIN_W, 2 * qk_w + 2 * MAIN_W + 2 * ML_HEADS], axis=1)
    w_main = jnp.concatenate([w_q, w_k, w_v, w_o, w_mq], axis=1).astype(BF16)
    tm = _row_tile(s, 1024)
    proj, gates = rms_matmul(x, gain, w_main, _pad_cols(w_g, LANES).astype(BF16), tm=tm, tn=512)
    gates = gates[:, :2 * ML_HEADS]
    q_heads = proj[:, :qk_w].reshape(s, ML_HEADS, ML_DQK).transpose(1, 0, 2)
    k_t = proj[:, qk_w:2 * qk_w].T
    h = mlstm_heads(q_heads, k_t, proj, 2 * qk_w // MAIN_W, 2 * qk_w // MAIN_W + 1, gates.T, gates, gate_bias,
                    head_norm)
    mo = mem_attention(proj, (2 * qk_w + 2 * MAIN_W) // MEM_W, mem_kt, mem_v, tm=_row_tile(s, 512))
    return h, mo


def _with_ones_rows(v):
    g, s, _ = v.shape
    return jnp.concatenate([v.transpose(0, 2, 1), jnp.ones((g, ONES_ROWS, s), v.dtype)], axis=1)


def _shared_kv(x, kv_norm, w_kv, cmp_pos, cmp_w1, cmp_w2, cos2, sin2):
    s = x.shape[0]
    n_seg = s // CMP_STRIDE
    n_sel = s // SLC_LEN
    seg_w = CMP_STRIDE * NSA_HD
    kv = rms_matmul(x, kv_norm, w_kv.astype(BF16), tm=_row_tile(s, 1024), tn=512)
    kslc, vslc, kwin, vwin = kv_prep(kv, cos2, sin2, tm=_row_tile(s, 512))
    segs = kv[:, :2 * NSA_KV * NSA_HD].reshape(n_seg, CMP_STRIDE, 2 * NSA_KV, NSA_HD)
    segs = segs.transpose(2, 0, 1, 3).reshape(2 * NSA_KV, n_seg, seg_w)
    pos = cmp_pos.reshape(2, 2, 1, seg_w)
    w1 = cmp_w1.astype(BF16)
    cmp = compress(segs, jnp.roll(segs, -1, axis=1), pos[:, 0], pos[:, 1], w1[:, :seg_w], w1[:, seg_w:],
                   cmp_w2.astype(BF16), tm=_row_tile(n_seg, 256))
    cmp = cmp.reshape(2 * NSA_KV, n_sel // SEL_CHUNK, SEL_CHUNK, 4, NSA_HD).transpose(0, 1, 3, 2, 4)
    cmp = cmp.reshape(2 * NSA_KV, n_seg, NSA_HD)
    cmp = cmp.astype(BF16)
    block_in_window = (jnp.arange(s, dtype=jnp.int32) // SLC_LEN) % LANES
    onehot = (block_in_window[:, None] == jnp.arange(LANES, dtype=jnp.int32)[None, :]).astype(BF16)
    kslc_aug = jnp.concatenate([kslc, jnp.broadcast_to(onehot[None], (NSA_KV, s, LANES))], axis=2)
    return dict(kc=cmp[:NSA_KV], vct=cmp[NSA_KV:].transpose(0, 2, 1), kslc=kslc_aug, vslct=_with_ones_rows(vslc),
                kwin=kwin, vwint=_with_ones_rows(vwin))


def _nsa_layer_heads(x, gain, w_in, gate_bias, shared, cos2t, sin2t, mem_kt, mem_v):
    s = x.shape[0]
    n_gate = 3 * NSA_GROUP
    w_q, w_g, w_mq = jnp.split(w_in, [MAIN_W, MAIN_W + 3 * NSA_HEADS], axis=1)
    w_main = jnp.concatenate([w_q, w_mq], axis=1).astype(BF16)
    w_gate = jnp.concatenate([_pad_cols(w_g[:, g * n_gate:(g + 1) * n_gate], LANES) for g in range(NSA_KV)], axis=1)
    b_gate = jnp.concatenate(
        [_pad_cols(gate_bias[None, g * n_gate:(g + 1) * n_gate], LANES) for g in range(NSA_KV)], axis=1)
    tm = _row_tile(s, 1024)
    proj, gate_logits = rms_matmul(x, gain, w_main, w_gate.astype(BF16), tm=tm, tn=512)
    o = nsa_attention(proj, cos2t, sin2t, gate_logits, b_gate, shared["kc"], shared["vct"], shared["kslc"],
                      shared["vslct"], shared["kwin"], shared["vwint"])
    mo = mem_attention(proj, MAIN_W // MEM_W, mem_kt, mem_v, tm=_row_tile(s, 512))
    return o, mo


def kernel(x, mem, norm_gains, ffn_w_gate, ffn_w_up, ffn_w_down, mem_norm, mem_w_kv, w_out, a_w_in, a_gate_bias,
           a_head_norm, kv_norm, w_kv, cmp_pos, cmp_w1, cmp_w2, b_w_in, b_gate_bias):
    batch, seq, d_model = x.shape
    depth = norm_gains.shape[0]
    n_a = a_w_in.shape[0]
    assert seq % (SLC_LEN * LANES) == 0 and seq % ML_CHUNK == 0, "sequence must tile the selection-block lanes"
    cos2, sin2 = _rope_tables(seq)
    cos2t, sin2t = cos2.T, sin2.T
    tm_ffn = _row_tile(seq, 512)

    def ffn(xb, layer, half, g_pre, g_post):
        return half_ffn(xb, g_pre, g_post, ffn_w_gate[layer, half].astype(BF16), ffn_w_up[layer, half].astype(BF16),
                        ffn_w_down[layer, half].astype(BF16), tm=tm_ffn, tf=512)

    outs = []
    for b in range(batch):
        xb = x[b]
        shared = None
        for layer in range(depth):
            if layer == n_a:
                shared = _shared_kv(xb, kv_norm, w_kv, cmp_pos, cmp_w1, cmp_w2, cos2, sin2)
            g = norm_gains[layer]
            xb = ffn(xb, layer, 0, g[0], g[1])
            mem_kt, mem_v = _mem_kv(mem[b], mem_norm[layer], mem_w_kv[layer])
            if layer < n_a:
                main, mo = _mlstm_layer_heads(xb, g[2], a_w_in[layer], a_gate_bias[layer], a_head_norm[layer],
                                              mem_kt, mem_v)
            else:
                lb = layer - n_a
                main, mo = _nsa_layer_heads(xb, g[2], b_w_in[lb], b_gate_bias[lb], shared, cos2t, sin2t, mem_kt, mem_v)
            xb = out_proj(main, mo, xb, w_out[layer].astype(BF16), g[3], tm=_row_tile(seq, 256))
            xb = ffn(xb, layer, 1, g[4], g[5])
        outs.append(xb)
    return jnp.stack(outs, axis=0)
```

```python
import functools

import jax
import jax.numpy as jnp
from jax import lax
from jax.experimental import pallas as pl
from jax.experimental.pallas import tpu as pltpu

F32 = jnp.float32
BF16 = jnp.bfloat16

MEM_HEADS = 4
MEM_HD = 128
MEM_W = MEM_HEADS * MEM_HD
ML_HEADS = 4
ML_DV = 384
ML_DQK = 192
MAIN_W = ML_HEADS * ML_DV
NSA_HEADS = 12
NSA_HD = 128
NSA_KV = 2
NSA_GROUP = NSA_HEADS // NSA_KV
GROUP_W = NSA_GROUP * NSA_HD
CMP_LEN = 32
CMP_STRIDE = 16
SLC_LEN = 64
SLC_TOPK = 16
WIN = 512
Q_BLOCK = 128
FFN_RES = 0.5
ROPE_THETA = 10000.0
EPS = 1e-6
NEG = -1e30
BIG = 1e30
REMOVED = -3e38
LOG2E = 1.4426950408889634
TINY = 1e-30

LANES = 128
VMEM_LIMIT_BYTES = 56 * 1024 * 1024

ML_CHUNK = 256
ML_DQK_PAD = 256
SLC_TILE = 512
ONES_ROWS = 16
SEL_CHUNK = 64


def _params(*semantics):
    return pltpu.CompilerParams(dimension_semantics=semantics, vmem_limit_bytes=VMEM_LIMIT_BYTES)


def _rms(x, gain):
    return x * lax.rsqrt(jnp.mean(x * x, axis=-1, keepdims=True) + EPS) * gain


def _dot(a, b):
    return jnp.dot(a, b, preferred_element_type=F32)


def _exact_dot(a, b01):
    a1 = a.astype(BF16)
    r1 = a - a1.astype(F32)
    a2 = r1.astype(BF16)
    a3 = (r1 - a2.astype(F32)).astype(BF16)
    return _dot(a1, b01) + _dot(a2, b01) + _dot(a3, b01)


def _exact_dot_left(b01, a):
    a1 = a.astype(BF16)
    r1 = a - a1.astype(F32)
    a2 = r1.astype(BF16)
    a3 = (r1 - a2.astype(F32)).astype(BF16)
    return _dot(b01, a1) + _dot(b01, a2) + _dot(b01, a3)


def _rms_matmul_kernel(x_ref, g_ref, w_ref, o_ref, xn_ref):
    @pl.when(pl.program_id(1) == 0)
    def _():
        xn_ref[...] = _rms(x_ref[...], g_ref[...]).astype(BF16)

    o_ref[...] = _dot(xn_ref[...], w_ref[...]).astype(o_ref.dtype)


def _rms_matmul_side_kernel(x_ref, g_ref, w_ref, wside_ref, o_ref, oside_ref, xn_ref):
    @pl.when(pl.program_id(1) == 0)
    def _():
        xn_ref[...] = _rms(x_ref[...], g_ref[...]).astype(BF16)
        oside_ref[...] = _dot(xn_ref[...], wside_ref[...])

    o_ref[...] = _dot(xn_ref[...], w_ref[...]).astype(o_ref.dtype)


def rms_matmul(x, gain, w, w_side=None, *, tm, tn, out_dtype=F32):
    s, k = x.shape
    n = w.shape[1]
    in_specs = [
        pl.BlockSpec((tm, k), lambda i, j: (i, 0)),
        pl.BlockSpec((1, k), lambda i, j: (0, 0)),
        pl.BlockSpec((k, tn), lambda i, j: (0, j)),
    ]
    out_specs = pl.BlockSpec((tm, tn), lambda i, j: (i, j))
    out_shape = jax.ShapeDtypeStruct((s, n), out_dtype)
    args = (x, gain.reshape(1, k), w)
    body = _rms_matmul_kernel
    if w_side is not None:
        n_side = w_side.shape[1]
        in_specs.append(pl.BlockSpec((k, n_side), lambda i, j: (0, 0)))
        out_specs = [out_specs, pl.BlockSpec((tm, n_side), lambda i, j: (i, 0))]
        out_shape = [out_shape, jax.ShapeDtypeStruct((s, n_side), F32)]
        args = args + (w_side,)
        body = _rms_matmul_side_kernel
    return pl.pallas_call(
        body,
        grid=(s // tm, n // tn),
        in_specs=in_specs,
        out_specs=out_specs,
        out_shape=out_shape,
        scratch_shapes=[pltpu.VMEM((tm, k), BF16)],
        compiler_params=_params("parallel", "arbitrary"),
        name="rms_matmul",
    )(*args)


def _ffn_kernel(x_ref, gpre_ref, gpost_ref, wg_ref, wu_ref, wd_ref, o_ref, xn_ref):
    j = pl.program_id(1)

    @pl.when(j == 0)
    def _():
        xn_ref[...] = _rms(x_ref[...], gpre_ref[...]).astype(BF16)
        o_ref[...] = jnp.zeros_like(o_ref)

    xn = xn_ref[...]
    gate = _dot(xn, wg_ref[...])
    up = _dot(xn, wu_ref[...])
    hidden = (gate * jax.nn.sigmoid(gate) * up).astype(BF16)
    o_ref[...] += _dot(hidden, wd_ref[...])

    @pl.when(j == pl.num_programs(1) - 1)
    def _():
        o_ref[...] = x_ref[...] + FFN_RES * _rms(o_ref[...], gpost_ref[...])


def half_ffn(x, g_pre, g_post, w_gate, w_up, w_down, layer, half, *, tm, tf):
    s, d = x.shape
    d_ff = w_gate.shape[3]
    return pl.pallas_call(
        _ffn_kernel,
        grid=(s // tm, d_ff // tf),
        in_specs=[
            pl.BlockSpec((tm, d), lambda i, j: (i, 0)),
            pl.BlockSpec((1, d), lambda i, j: (0, 0)),
            pl.BlockSpec((1, d), lambda i, j: (0, 0)),
            pl.BlockSpec((None, None, d, tf), lambda i, j: (layer, half, 0, j)),
            pl.BlockSpec((None, None, d, tf), lambda i, j: (layer, half, 0, j)),
            pl.BlockSpec((None, None, tf, d), lambda i, j: (layer, half, j, 0)),
        ],
        out_specs=pl.BlockSpec((tm, d), lambda i, j: (i, 0)),
        out_shape=jax.ShapeDtypeStruct((s, d), F32),
        scratch_shapes=[pltpu.VMEM((tm, d), BF16)],
        compiler_params=_params("parallel", "arbitrary"),
        name="half_ffn",
    )(x, g_pre.reshape(1, d), g_post.reshape(1, d), w_gate, w_up, w_down)


def _out_proj_kernel(h_ref, mo_ref, x_ref, wh_ref, wm_ref, g_ref, o_ref):
    y = _dot(h_ref[...].astype(BF16), wh_ref[...]) + _dot(mo_ref[...].astype(BF16), wm_ref[...])
    o_ref[...] = x_ref[...] + _rms(y, g_ref[...])


def out_proj(main, mo, x, w, layer, gain, *, tm):
    s, d = x.shape
    km, ko = main.shape[1], mo.shape[1]
    assert km % ko == 0
    return pl.pallas_call(
        _out_proj_kernel,
        grid=(s // tm,),
        in_specs=[
            pl.BlockSpec((tm, km), lambda i: (i, 0)),
            pl.BlockSpec((tm, ko), lambda i: (i, 0)),
            pl.BlockSpec((tm, d), lambda i: (i, 0)),
            pl.BlockSpec((None, km, d), lambda i: (layer, 0, 0)),
            pl.BlockSpec((None, ko, d), lambda i: (layer, km // ko, 0)),
            pl.BlockSpec((1, d), lambda i: (0, 0)),
        ],
        out_specs=pl.BlockSpec((tm, d), lambda i: (i, 0)),
        out_shape=jax.ShapeDtypeStruct((s, d), F32),
        compiler_params=_params("parallel"),
        name="out_proj",
    )(main, mo, x, w, w, gain.reshape(1, d))


def _mem_attn_kernel(q_ref, kt_ref, v_ref, o_ref):
    scale = MEM_HD ** -0.5
    for h in range(MEM_HEADS):
        cols = slice(h * MEM_HD, (h + 1) * MEM_HD)
        s = _dot(q_ref[:, cols].astype(BF16), kt_ref[cols, :]) * scale
        e = jnp.exp(s - jnp.max(s, axis=-1, keepdims=True))
        p = e / jnp.sum(e, axis=-1, keepdims=True)
        o_ref[:, cols] = _dot(p.astype(BF16), v_ref[:, cols])


def mem_attention(proj, mq_block, mem_kt, mem_v, *, tm):
    s = proj.shape[0]
    m = mem_v.shape[0]
    return pl.pallas_call(
        _mem_attn_kernel,
        grid=(s // tm,),
        in_specs=[
            pl.BlockSpec((tm, MEM_W), lambda i: (i, mq_block)),
            pl.BlockSpec((MEM_W, m), lambda i: (0, 0)),
            pl.BlockSpec((m, MEM_W), lambda i: (0, 0)),
        ],
        out_specs=pl.BlockSpec((tm, MEM_W), lambda i: (i, 0)),
        out_shape=jax.ShapeDtypeStruct((s, MEM_W), F32),
        compiler_params=_params("parallel"),
        name="mem_attention",
    )(proj, mem_kt, mem_v)


def _log_sigmoid(x):
    return jnp.minimum(x, 0.0) - jnp.log1p(jnp.exp(-jnp.abs(x)))


def _mlstm_kernel(q_ref, k_ref, v_ref, o_ref, grow_ref, gcol_ref, brow_ref, bcol_ref, hn_ref,
                  out_ref, c_ref, m_ref):
    chunk = grow_ref.shape[1]

    @pl.when(pl.program_id(0) == 0)
    def _():
        c_ref[...] = jnp.zeros_like(c_ref)
        m_ref[...] = jnp.zeros_like(m_ref)

    grow = grow_ref[...] + brow_ref[...]
    gcol = gcol_ref[...] + bcol_ref[...]
    r_idx = lax.broadcasted_iota(jnp.int32, (chunk, chunk), 0)
    c_idx = lax.broadcasted_iota(jnp.int32, (chunk, chunk), 1)
    causal = c_idx <= r_idx
    b_rows = _exact_dot(_log_sigmoid(grow), (r_idx <= c_idx).astype(BF16))
    b_cols = _exact_dot_left(causal.astype(BF16), _log_sigmoid(gcol))
    ones_col = (lax.broadcasted_iota(jnp.int32, (chunk, LANES), 1) == 0).astype(BF16)

    for h in range(ML_HEADS):
        b_r = b_rows[ML_HEADS + h:ML_HEADS + h + 1, :]
        i_r = grow[h:h + 1, :]
        b_c = b_cols[:, ML_HEADS + h:ML_HEADS + h + 1]
        m_prev = m_ref[h:h + 1, 0:1]
        dmat = jnp.where(causal, b_c - b_r + i_r, -jnp.inf)
        m_inter = b_c + m_prev
        m_vec = jnp.maximum(jnp.max(dmat, axis=-1, keepdims=True), m_inter)
        qk_cols = slice(h * ML_DQK_PAD, (h + 1) * ML_DQK_PAD)
        q = q_ref[:, qk_cols].astype(BF16)
        kt = (k_ref[:, qk_cols] * (ML_DQK ** -0.5)).T
        smat = _dot(q, kt.astype(BF16)) * jnp.exp(dmat - m_vec)
        inter = jnp.exp(m_inter - m_vec)
        vcols = slice(h * ML_DV, (h + 1) * ML_DV)
        v_aug = jnp.concatenate([v_ref[:, vcols].astype(BF16), ones_col], axis=1)
        c_prev = c_ref[h]
        num_aug = _dot(smat.astype(BF16), v_aug) + inter * _dot(q, c_prev.astype(BF16))
        den = num_aug[:, ML_DV:ML_DV + 1]
        hval = num_aug[:, :ML_DV] / jnp.maximum(jnp.abs(den), jnp.exp(-m_vec))
        m_last = m_vec[chunk - 1:chunk, :]
        b_last = b_c[chunk - 1:chunk, :]
        w_r = jnp.exp(b_last - b_r + i_r - m_last)
        decay = jnp.exp(b_last + m_prev - m_last)
        c_ref[h] = decay * c_prev + _dot((kt * w_r).astype(BF16), v_aug)
        m_ref[h:h + 1, :] = jnp.broadcast_to(m_last, (1, LANES))
        out_ref[:, vcols] = _rms(hval, hn_ref[:, vcols]) * jax.nn.sigmoid(o_ref[:, vcols])


def mlstm_heads(proj, q_block, k_block, v_block, o_block, gates_row, gates_col, bias, head_norm):
    s = proj.shape[0]
    chunk = ML_CHUNK
    h2 = 2 * ML_HEADS
    qk_w = ML_HEADS * ML_DQK_PAD
    return pl.pallas_call(
        _mlstm_kernel,
        grid=(s // chunk,),
        in_specs=[
            pl.BlockSpec((chunk, qk_w), lambda c: (c, q_block)),
            pl.BlockSpec((chunk, qk_w), lambda c: (c, k_block)),
            pl.BlockSpec((chunk, MAIN_W), lambda c: (c, v_block)),
            pl.BlockSpec((chunk, MAIN_W), lambda c: (c, o_block)),
            pl.BlockSpec((h2, chunk), lambda c: (0, c)),
            pl.BlockSpec((chunk, h2), lambda c: (c, 0)),
            pl.BlockSpec((h2, 1), lambda c: (0, 0)),
            pl.BlockSpec((1, h2), lambda c: (0, 0)),
            pl.BlockSpec((1, MAIN_W), lambda c: (0, 0)),
        ],
        out_specs=pl.BlockSpec((chunk, MAIN_W), lambda c: (c, 0)),
        out_shape=jax.ShapeDtypeStruct((s, MAIN_W), F32),
        scratch_shapes=[pltpu.VMEM((ML_HEADS, ML_DQK_PAD, ML_DV + LANES), F32), pltpu.VMEM((8, LANES), F32)],
        compiler_params=_params("arbitrary"),
        name="mlstm",
    )(proj, proj, proj, proj, gates_row, gates_col, bias.reshape(h2, 1), bias.reshape(1, h2),
      head_norm.reshape(1, MAIN_W))


def _rope(x, cos2, sin2):
    return x * cos2 + pltpu.roll(x, NSA_HD // 2, axis=1) * sin2


def _kv_prep_kernel(kv_ref, cos_ref, sin_ref, kslc_ref, vslc_ref, kwin_ref, vwin_ref):
    cos2 = cos_ref[...]
    sin2 = sin_ref[...]
    for g in range(NSA_KV):
        def col(c):
            start = (c * NSA_KV + g) * NSA_HD
            return kv_ref[:, start:start + NSA_HD]
        kslc_ref[g] = _rope(col(2), cos2, sin2).astype(BF16)
        vslc_ref[g] = col(3).astype(BF16)
        kwin_ref[g] = _rope(col(4), cos2, sin2).astype(BF16)
        vwin_ref[g] = col(5).astype(BF16)


def kv_prep(kv, cos2, sin2, *, tm):
    s, w = kv.shape
    out = jax.ShapeDtypeStruct((NSA_KV, s, NSA_HD), BF16)
    ospec = pl.BlockSpec((NSA_KV, tm, NSA_HD), lambda i: (0, i, 0))
    return pl.pallas_call(
        _kv_prep_kernel,
        grid=(s // tm,),
        in_specs=[
            pl.BlockSpec((tm, w), lambda i: (i, 0)),
            pl.BlockSpec((tm, NSA_HD), lambda i: (i, 0)),
            pl.BlockSpec((tm, NSA_HD), lambda i: (i, 0)),
        ],
        out_specs=[ospec] * 4,
        out_shape=[out] * 4,
        compiler_params=_params("parallel"),
        name="kv_prep",
    )(kv, cos2, sin2)


def _compress_kernel(a_ref, b_ref, pa_ref, pb_ref, w1a_ref, w1b_ref, w2_ref, o_ref):
    xa = (a_ref[...] + pa_ref[...]).astype(BF16)
    xb = (b_ref[...] + pb_ref[...]).astype(BF16)
    hidden = _dot(xa, w1a_ref[...]) + _dot(xb, w1b_ref[...])
    act = jax.nn.gelu(hidden, approximate=True)
    o_ref[...] = _dot(act.astype(BF16), w2_ref[...])


def compress(seg_a, seg_b, pos_a, pos_b, w1a, w1b, w2, *, tm):
    n4, n, kw = seg_a.shape
    hid = w1a.shape[2]
    wmap = lambda cg, i: (cg // NSA_KV, 0, 0)
    return pl.pallas_call(
        _compress_kernel,
        grid=(n4, n // tm),
        in_specs=[
            pl.BlockSpec((None, tm, kw), lambda cg, i: (cg, i, 0)),
            pl.BlockSpec((None, tm, kw), lambda cg, i: (cg, i, 0)),
            pl.BlockSpec((None, 1, kw), wmap),
            pl.BlockSpec((None, 1, kw), wmap),
            pl.BlockSpec((None, kw, hid), wmap),
            pl.BlockSpec((None, kw, hid), wmap),
            pl.BlockSpec((None, hid, NSA_HD), wmap),
        ],
        out_specs=pl.BlockSpec((None, tm, NSA_HD), lambda cg, i: (cg, i, 0)),
        out_shape=jax.ShapeDtypeStruct((n4, n, NSA_HD), F32),
        compiler_params=_params("parallel", "parallel"),
        name="compress",
    )(seg_a, seg_b, pos_a, pos_b, w1a, w1b, w2)


def _rope_t(x, cos2t, sin2t):
    half = NSA_HD // 2
    return x * cos2t + jnp.concatenate([x[half:], x[:half]], axis=0) * sin2t


def _col_max(chunks):
    return jnp.max(functools.reduce(jnp.maximum, chunks), axis=0, keepdims=True)


def _nsa_kernel(q_ref, cos_ref, sin_ref, gl_ref, gb_ref, kc_ref, vct_ref, kslc_ref, vslct_ref, kwin_ref, vwint_ref,
                o_ref, qt_ref, qaug_ref, sel_ref, p_ref, acc_ref, s_ref, m_ref, ocmp_ref):
    qi = pl.program_id(1)
    qb = Q_BLOCK
    n_cmp = kc_ref.shape[0]
    n_sel = n_cmp // 4
    t0 = qi * qb
    t_row = t0 + lax.broadcasted_iota(jnp.int32, (1, qb), 1)
    hcols = [slice(j * qb, (j + 1) * qb) for j in range(NSA_GROUP)]

    cos2t = cos_ref[...]
    sin2t = sin_ref[...]
    for j in range(NSA_GROUP):
        qt = q_ref[:, j * NSA_HD:(j + 1) * NSA_HD].T * (NSA_HD ** -0.5 * LOG2E)
        qt_ref[:, hcols[j]] = qt.astype(BF16)
        qaug_ref[0:NSA_HD, hcols[j]] = _rope_t(qt, cos2t, sin2t).astype(BF16)

    def compressed_and_select(n_chunks):
        rows = 4 * SEL_CHUNK * n_chunks
        n_blk = SEL_CHUNK * n_chunks
        s_c = _dot(kc_ref[0:rows, :], qt_ref[...])
        blk_i = lax.broadcasted_iota(jnp.int32, (SEL_CHUNK, qb), 0)
        pieces = [(c, r) for c in range(n_chunks) for r in range(4)]

        def piece_rows(c, r):
            return slice((4 * c + r) * SEL_CHUNK, (4 * c + r + 1) * SEL_CHUNK)

        cmp_bias = {(c, r): jnp.where(SLC_LEN * (blk_i + SEL_CHUNK * c) + (CMP_STRIDE * r + CMP_LEN - 1) <= t_row,
                                      0.0, NEG) for c, r in pieces}
        has_valid = (t_row >= CMP_LEN - 1).astype(F32)
        imp = {cr: jnp.zeros((SEL_CHUNK, qb), F32) for cr in pieces}
        for j in range(NSA_GROUP):
            sb = {cr: s_c[piece_rows(*cr), hcols[j]] + cmp_bias[cr] for cr in pieces}
            m = _col_max(list(sb.values()))
            e = {cr: jnp.exp2(sb[cr] - m) for cr in pieces}
            den = functools.reduce(jnp.add, [jnp.sum(x, axis=0, keepdims=True) for x in e.values()])
            rinv = has_valid / jnp.maximum(den, TINY)
            for cr in pieces:
                p = e[cr] * rinv
                imp[cr] = imp[cr] + p
                p_ref[piece_rows(*cr), hcols[j]] = p.astype(BF16)
        ocmp_ref[...] = _dot(vct_ref[:, 0:rows], p_ref[0:rows, :])

        imp_r = [jnp.concatenate([imp[(c, r)] for c in range(n_chunks)], axis=0) for r in range(4)]
        blk = lax.broadcasted_iota(jnp.int32, (n_blk, qb), 0)
        imp3_prev = jnp.where(blk == 0, 0.0, pltpu.roll(imp_r[3], 1, axis=0))
        p_slc = ((((((imp_r[0] + imp3_prev) + imp_r[1]) + imp_r[0]) + imp_r[2]) + imp_r[1]) + imp_r[3]) + imp_r[2]
        cur = jnp.right_shift(t_row, 6)
        forced = (blk == 0) | (blk == cur) | (blk == cur - 1)
        score = jnp.where(blk > cur, NEG, jnp.where(forced, BIG, p_slc))
        blk_f = blk.astype(F32)
        sel = jnp.zeros((n_blk, qb), F32)
        for _ in range(SLC_TOPK):
            top = jnp.max(score, axis=0, keepdims=True)
            first = jnp.min(jnp.where(score == top, blk_f, float(n_blk)), axis=0, keepdims=True)
            hit = blk_f == first
            sel = jnp.where(hit, 1.0, sel)
            score = jnp.where(hit, REMOVED, score)
        sel_ref[0:n_blk, :] = sel
        if n_blk < n_sel:
            sel_ref[n_blk:n_sel, :] = jnp.zeros((n_sel - n_blk, qb), F32)

    chunks_needed = (t0 + qb - 1) // (SLC_LEN * SEL_CHUNK) + 1
    for n_chunks in range(1, n_sel // SEL_CHUNK + 1):
        pl.when(chunks_needed == n_chunks)(functools.partial(compressed_and_select, n_chunks))

    acc_ref[...] = jnp.zeros_like(acc_ref)
    m_ref[...] = jnp.full_like(m_ref, REMOVED)
    tiles_per_window = LANES * SLC_LEN // SLC_TILE
    kpos0 = lax.broadcasted_iota(jnp.int32, (SLC_TILE, qb), 0)

    def scores(kt, slot):
        first = pl.multiple_of((kt // tiles_per_window) * LANES, LANES)
        unselected = ((sel_ref[pl.ds(first, LANES), :] - 1.0) * BIG).astype(BF16)
        for j in range(NSA_GROUP):
            qaug_ref[NSA_HD:NSA_HD + LANES, hcols[j]] = unselected
        start = pl.multiple_of(kt * SLC_TILE, SLC_TILE)
        s_ref[slot] = _dot(kslc_ref[pl.ds(start, SLC_TILE), :], qaug_ref[...])

    def weights(kt, slot, causal):
        if causal:
            future = jnp.where(kpos0 + kt * SLC_TILE <= t_row, 0.0, NEG)
        alpha = []
        for j in range(NSA_GROUP):
            s = s_ref[slot, :, hcols[j]]
            if causal:
                s = s + future
            m_old = m_ref[0:1, hcols[j]]
            m_j = jnp.maximum(m_old, jnp.max(s, axis=0, keepdims=True))
            p_ref[slot * SLC_TILE:(slot + 1) * SLC_TILE, hcols[j]] = jnp.exp2(s - m_j).astype(BF16)
            m_ref[0:1, hcols[j]] = m_j
            alpha.append(jnp.exp2(m_old - m_j))
        return jnp.concatenate(alpha, axis=1)

    def weighted_values(kt, slot):
        start = pl.multiple_of(kt * SLC_TILE, SLC_TILE)
        return _dot(vslct_ref[:, pl.ds(start, SLC_TILE)], p_ref[slot * SLC_TILE:(slot + 1) * SLC_TILE, :])

    def absorb(kt, slot, causal):
        alpha = weights(kt, slot, causal)
        acc_ref[...] = acc_ref[...] * alpha + weighted_values(kt, slot)

    last = (t0 + qb - 1) // SLC_TILE
    scores(0, 0)

    def pair(i, carry):
        scores(2 * i + 1, 1)
        absorb(2 * i, 0, False)
        scores(jnp.minimum(2 * i + 2, last), 0)
        absorb(2 * i + 1, 1, False)
        return carry

    lax.fori_loop(0, last // 2, pair, 0)

    @pl.when(last % 2 == 0)
    def _():
        absorb(last, 0, True)

    @pl.when(last % 2 == 1)
    def _():
        scores(last, 1)
        absorb(last - 1, 0, False)
        absorb(last, 1, True)

    acc = acc_ref[...]
    o_slc = acc[:NSA_HD] / acc[NSA_HD:NSA_HD + 1]

    span = qb + WIN
    start = pl.multiple_of(jnp.maximum(t0 - WIN, 0), qb)
    s_w = _dot(kwin_ref[pl.ds(start, span), :], qaug_ref[0:NSA_HD, :])
    spos0 = lax.broadcasted_iota(jnp.int32, (qb, qb), 0)
    win_bias = []
    for c in range(span // qb):
        spos = spos0 + (start + c * qb)
        win_bias.append(jnp.where((spos <= t_row) & (spos > t_row - WIN), 0.0, NEG))
    for j in range(NSA_GROUP):
        sb = [s_w[c * qb:(c + 1) * qb, hcols[j]] + win_bias[c] for c in range(span // qb)]
        m = _col_max(sb)
        for c in range(span // qb):
            p_ref[c * qb:(c + 1) * qb, hcols[j]] = jnp.exp2(sb[c] - m).astype(BF16)
    pw = _dot(vwint_ref[:, pl.ds(start, span)], p_ref[0:span, :])
    o_win = pw[:NSA_HD] / pw[NSA_HD:NSA_HD + 1]

    gates_t = jax.nn.sigmoid(gl_ref[...] + gb_ref[...]).T
    for j in range(NSA_GROUP):
        o_t = (gates_t[3 * j:3 * j + 1] * ocmp_ref[:, hcols[j]]
               + gates_t[3 * j + 1:3 * j + 2] * o_slc[:, hcols[j]]
               + gates_t[3 * j + 2:3 * j + 3] * o_win[:, hcols[j]])
        o_ref[:, j * NSA_HD:(j + 1) * NSA_HD] = o_t.T


def nsa_attention(proj, cos2t, sin2t, gate_logits, gate_bias, kc, vct, kslc, vslct, kwin, vwint):
    s = proj.shape[0]
    n_cmp = kc.shape[1]
    rows_aug = NSA_HD + ONES_ROWS
    width = NSA_GROUP * Q_BLOCK
    qspec = pl.BlockSpec((Q_BLOCK, GROUP_W), lambda g, i: (i, g))
    tspec = pl.BlockSpec((NSA_HD, Q_BLOCK), lambda g, i: (0, i))

    def resident(shape):
        return pl.BlockSpec((None,) + shape, lambda g, i: (g, 0, 0), pipeline_mode=pl.Buffered(1))

    return pl.pallas_call(
        _nsa_kernel,
        grid=(NSA_KV, s // Q_BLOCK),
        in_specs=[
            qspec, tspec, tspec,
            pl.BlockSpec((Q_BLOCK, LANES), lambda g, i: (i, g)),
            pl.BlockSpec((1, LANES), lambda g, i: (0, g)),
            resident((n_cmp, NSA_HD)), resident((NSA_HD, n_cmp)),
            resident((s, NSA_HD + LANES)), resident((rows_aug, s)),
            resident((s, NSA_HD)), resident((rows_aug, s)),
        ],
        out_specs=qspec,
        out_shape=jax.ShapeDtypeStruct((s, NSA_KV * GROUP_W), F32),
        scratch_shapes=[
            pltpu.VMEM((NSA_HD, width), BF16),
            pltpu.VMEM((NSA_HD + LANES, width), BF16),
            pltpu.VMEM((n_cmp // 4, Q_BLOCK), F32),
            pltpu.VMEM((max(n_cmp, Q_BLOCK + WIN, 2 * SLC_TILE), width), BF16),
            pltpu.VMEM((rows_aug, width), F32),
            pltpu.VMEM((2, SLC_TILE, width), F32),
            pltpu.VMEM((8, width), F32),
            pltpu.VMEM((NSA_HD, width), F32),
        ],
        compiler_params=_params("parallel", "arbitrary"),
        name="nsa_attention",
    )(proj, cos2t, sin2t, gate_logits, gate_bias, kc, vct, kslc, vslct, kwin, vwint)


def _row_tile(s, want):
    return want if s % want == 0 else s


def _rope_tables(seq):
    inv = ROPE_THETA ** (-jnp.arange(0, NSA_HD, 2, dtype=F32) / NSA_HD)
    ang = jnp.arange(seq, dtype=F32)[:, None] * inv[None, :]
    cos, sin = jnp.cos(ang), jnp.sin(ang)
    return jnp.concatenate([cos, cos], axis=1), jnp.concatenate([-sin, sin], axis=1)


def _pad_cols(a, width):
    return jnp.pad(a, ((0, 0), (0, width - a.shape[1])))


def _mem_kv(mem, gain, w_kv):
    kv = rms_matmul(mem, gain, w_kv.astype(BF16), tm=mem.shape[0], tn=MEM_W, out_dtype=BF16)
    return kv[:, :MEM_W].T, kv[:, MEM_W:]


def _mlstm_layer_heads(x, gain, w_in, gate_bias, head_norm, mem_kt, mem_v):
    s = x.shape[0]
    qk_w = ML_HEADS * ML_DQK
    w_q, w_k, w_v, w_o, w_g, w_mq = jnp.split(
        w_in, [qk_w, 2 * qk_w, 2 * qk_w + MAIN_W, 2 * qk_w + 2 * MAIN_W, 2 * qk_w + 2 * MAIN_W + 2 * ML_HEADS], axis=1)
    def head_slots(w):
        w = w.reshape(w.shape[0], ML_HEADS, ML_DQK)
        return jnp.pad(w, ((0, 0), (0, 0), (0, ML_DQK_PAD - ML_DQK))).reshape(w.shape[0], ML_HEADS * ML_DQK_PAD)

    w_main = jnp.concatenate([w_v, w_o, head_slots(w_q), head_slots(w_k), w_mq], axis=1).astype(BF16)
    qk_pad_w = ML_HEADS * ML_DQK_PAD
    q_block = 2 * MAIN_W // qk_pad_w
    mq_block = (2 * MAIN_W + 2 * qk_pad_w) // MEM_W
    assert 2 * MAIN_W % qk_pad_w == 0 and (2 * MAIN_W + 2 * qk_pad_w) % MEM_W == 0
    tm = _row_tile(s, 1024)
    proj, gates = rms_matmul(x, gain, w_main, _pad_cols(w_g, LANES).astype(BF16), tm=tm, tn=512)
    gates = gates[:, :2 * ML_HEADS]
    h = mlstm_heads(proj, q_block, q_block + 1, 0, 1, gates.T, gates, gate_bias, head_norm)
    mo = mem_attention(proj, mq_block, mem_kt, mem_v, tm=_row_tile(s, 512))
    return h, mo


def _with_ones_rows(v):
    g, s, _ = v.shape
    return jnp.concatenate([v.transpose(0, 2, 1), jnp.ones((g, ONES_ROWS, s), v.dtype)], axis=1)


def _shared_kv(x, kv_norm, w_kv, cmp_pos, cmp_w1, cmp_w2, cos2, sin2):
    s = x.shape[0]
    n_seg = s // CMP_STRIDE
    n_sel = s // SLC_LEN
    seg_w = CMP_STRIDE * NSA_HD
    kv = rms_matmul(x, kv_norm, w_kv.astype(BF16), tm=_row_tile(s, 1024), tn=512)
    kslc, vslc, kwin, vwin = kv_prep(kv, cos2, sin2, tm=_row_tile(s, 512))
    segs = kv[:, :2 * NSA_KV * NSA_HD].reshape(n_seg, CMP_STRIDE, 2 * NSA_KV, NSA_HD)
    segs = segs.transpose(2, 0, 1, 3).reshape(2 * NSA_KV, n_seg, seg_w)
    pos = cmp_pos.reshape(2, 2, 1, seg_w)
    w1 = cmp_w1.astype(BF16)
    cmp = compress(segs, jnp.roll(segs, -1, axis=1), pos[:, 0], pos[:, 1], w1[:, :seg_w], w1[:, seg_w:],
                   cmp_w2.astype(BF16), tm=_row_tile(n_seg, 256))
    cmp = cmp.reshape(2 * NSA_KV, n_sel // SEL_CHUNK, SEL_CHUNK, 4, NSA_HD).transpose(0, 1, 3, 2, 4)
    cmp = cmp.reshape(2 * NSA_KV, n_seg, NSA_HD)
    cmp = cmp.astype(BF16)
    block_in_window = (jnp.arange(s, dtype=jnp.int32) // SLC_LEN) % LANES
    onehot = (block_in_window[:, None] == jnp.arange(LANES, dtype=jnp.int32)[None, :]).astype(BF16)
    kslc_aug = jnp.concatenate([kslc, jnp.broadcast_to(onehot[None], (NSA_KV, s, LANES))], axis=2)
    return dict(kc=cmp[:NSA_KV], vct=cmp[NSA_KV:].transpose(0, 2, 1), kslc=kslc_aug, vslct=_with_ones_rows(vslc),
                kwin=kwin, vwint=_with_ones_rows(vwin))


def _nsa_layer_heads(x, gain, w_in, gate_bias, shared, cos2t, sin2t, mem_kt, mem_v):
    s = x.shape[0]
    n_gate = 3 * NSA_GROUP
    w_q, w_g, w_mq = jnp.split(w_in, [MAIN_W, MAIN_W + 3 * NSA_HEADS], axis=1)
    w_main = jnp.concatenate([w_q, w_mq], axis=1).astype(BF16)
    w_gate = jnp.concatenate([_pad_cols(w_g[:, g * n_gate:(g + 1) * n_gate], LANES) for g in range(NSA_KV)], axis=1)
    b_gate = jnp.concatenate(
        [_pad_cols(gate_bias[None, g * n_gate:(g + 1) * n_gate], LANES) for g in range(NSA_KV)], axis=1)
    tm = _row_tile(s, 1024)
    proj, gate_logits = rms_matmul(x, gain, w_main, w_gate.astype(BF16), tm=tm, tn=512)
    o = nsa_attention(proj, cos2t, sin2t, gate_logits, b_gate, shared["kc"], shared["vct"], shared["kslc"],
                      shared["vslct"], shared["kwin"], shared["vwint"])
    mo = mem_attention(proj, MAIN_W // MEM_W, mem_kt, mem_v, tm=_row_tile(s, 512))
    return o, mo


def kernel(x, mem, norm_gains, ffn_w_gate, ffn_w_up, ffn_w_down, mem_norm, mem_w_kv, w_out, a_w_in, a_gate_bias,
           a_head_norm, kv_norm, w_kv, cmp_pos, cmp_w1, cmp_w2, b_w_in, b_gate_bias):
    batch, seq, d_model = x.shape
    depth = norm_gains.shape[0]
    n_a = a_w_in.shape[0]
    assert seq % (SLC_LEN * LANES) == 0 and seq % ML_CHUNK == 0, "sequence must tile the selection-block lanes"
    cos2, sin2 = _rope_tables(seq)
    cos2t, sin2t = cos2.T, sin2.T
    tm_ffn = _row_tile(seq, 512)

    w_gate16, w_up16, w_down16 = ffn_w_gate.astype(BF16), ffn_w_up.astype(BF16), ffn_w_down.astype(BF16)
    w_out16 = w_out.astype(BF16)

    def ffn(xb, layer, half, g_pre, g_post):
        return half_ffn(xb, g_pre, g_post, w_gate16, w_up16, w_down16, layer, half, tm=tm_ffn, tf=512)

    outs = []
    for b in range(batch):
        xb = x[b]
        shared = None
        for layer in range(depth):
            if layer == n_a:
                shared = _shared_kv(xb, kv_norm, w_kv, cmp_pos, cmp_w1, cmp_w2, cos2, sin2)
            g = norm_gains[layer]
            xb = ffn(xb, layer, 0, g[0], g[1])
            mem_kt, mem_v = _mem_kv(mem[b], mem_norm[layer], mem_w_kv[layer])
            if layer < n_a:
                main, mo = _mlstm_layer_heads(xb, g[2], a_w_in[layer], a_gate_bias[layer], a_head_norm[layer],
                                              mem_kt, mem_v)
            else:
                lb = layer - n_a
                main, mo = _nsa_layer_heads(xb, g[2], b_w_in[lb], b_gate_bias[lb], shared, cos2t, sin2t, mem_kt, mem_v)
            xb = out_proj(main, mo, xb, w_out16, layer, g[3], tm=_row_tile(seq, 256))
            xb = ffn(xb, layer, 1, g[4], g[5])
        outs.append(xb)
    return jnp.stack(outs, axis=0)
```

```python
import functools

import jax
import jax.numpy as jnp
from jax import lax
from jax.experimental import pallas as pl
from jax.experimental.pallas import tpu as pltpu

F32 = jnp.float32
BF16 = jnp.bfloat16

MEM_HEADS = 4
MEM_HD = 128
MEM_W = MEM_HEADS * MEM_HD
ML_HEADS = 4
ML_DV = 384
ML_DQK = 192
MAIN_W = ML_HEADS * ML_DV
NSA_HEADS = 12
NSA_HD = 128
NSA_KV = 2
NSA_GROUP = NSA_HEADS // NSA_KV
GROUP_W = NSA_GROUP * NSA_HD
CMP_LEN = 32
CMP_STRIDE = 16
SLC_LEN = 64
SLC_TOPK = 16
WIN = 512
Q_BLOCK = 128
FFN_RES = 0.5
ROPE_THETA = 10000.0
EPS = 1e-6
NEG = -1e30
BIG = 1e30
REMOVED = -3e38
LOG2E = 1.4426950408889634
TINY = 1e-30

LANES = 128
VMEM_LIMIT_BYTES = 56 * 1024 * 1024

ML_CHUNK = 256
ML_DQK_PAD = 256
SLC_TILE = 512
ONES_ROWS = 16
SEL_CHUNK = 64


def _params(*semantics):
    return pltpu.CompilerParams(dimension_semantics=semantics, vmem_limit_bytes=VMEM_LIMIT_BYTES)


def _rms(x, gain):
    return x * lax.rsqrt(jnp.mean(x * x, axis=-1, keepdims=True) + EPS) * gain


def _dot(a, b):
    return jnp.dot(a, b, preferred_element_type=F32)


def _exact_dot(a, b01):
    a1 = a.astype(BF16)
    r1 = a - a1.astype(F32)
    a2 = r1.astype(BF16)
    a3 = (r1 - a2.astype(F32)).astype(BF16)
    return _dot(a1, b01) + _dot(a2, b01) + _dot(a3, b01)


def _exact_dot_left(b01, a):
    a1 = a.astype(BF16)
    r1 = a - a1.astype(F32)
    a2 = r1.astype(BF16)
    a3 = (r1 - a2.astype(F32)).astype(BF16)
    return _dot(b01, a1) + _dot(b01, a2) + _dot(b01, a3)


def _rms_matmul_kernel(x_ref, g_ref, w_ref, o_ref, xn_ref):
    @pl.when(pl.program_id(1) == 0)
    def _():
        xn_ref[...] = _rms(x_ref[...], g_ref[...]).astype(BF16)

    o_ref[...] = _dot(xn_ref[...], w_ref[...]).astype(o_ref.dtype)


def _rms_matmul_side_kernel(x_ref, g_ref, w_ref, wside_ref, o_ref, oside_ref, xn_ref):
    @pl.when(pl.program_id(1) == 0)
    def _():
        xn_ref[...] = _rms(x_ref[...], g_ref[...]).astype(BF16)
        oside_ref[...] = _dot(xn_ref[...], wside_ref[...])

    o_ref[...] = _dot(xn_ref[...], w_ref[...]).astype(o_ref.dtype)


def rms_matmul(x, gain, w, w_side=None, *, tm, tn, out_dtype=F32):
    s, k = x.shape
    n = w.shape[1]
    in_specs = [
        pl.BlockSpec((tm, k), lambda i, j: (i, 0)),
        pl.BlockSpec((1, k), lambda i, j: (0, 0)),
        pl.BlockSpec((k, tn), lambda i, j: (0, j)),
    ]
    out_specs = pl.BlockSpec((tm, tn), lambda i, j: (i, j))
    out_shape = jax.ShapeDtypeStruct((s, n), out_dtype)
    args = (x, gain.reshape(1, k), w)
    body = _rms_matmul_kernel
    if w_side is not None:
        n_side = w_side.shape[1]
        in_specs.append(pl.BlockSpec((k, n_side), lambda i, j: (0, 0)))
        out_specs = [out_specs, pl.BlockSpec((tm, n_side), lambda i, j: (i, 0))]
        out_shape = [out_shape, jax.ShapeDtypeStruct((s, n_side), F32)]
        args = args + (w_side,)
        body = _rms_matmul_side_kernel
    return pl.pallas_call(
        body,
        grid=(s // tm, n // tn),
        in_specs=in_specs,
        out_specs=out_specs,
        out_shape=out_shape,
        scratch_shapes=[pltpu.VMEM((tm, k), BF16)],
        compiler_params=_params("parallel", "arbitrary"),
        name="rms_matmul",
    )(*args)


def _ffn_kernel(x_ref, gpre_ref, gpost_ref, wg_ref, wu_ref, wd_ref, o_ref, xn_ref):
    j = pl.program_id(1)

    @pl.when(j == 0)
    def _():
        xn_ref[...] = _rms(x_ref[...], gpre_ref[...]).astype(BF16)
        o_ref[...] = jnp.zeros_like(o_ref)

    xn = xn_ref[...]
    gate = _dot(xn, wg_ref[...])
    up = _dot(xn, wu_ref[...])
    hidden = (gate * jax.nn.sigmoid(gate) * up).astype(BF16)
    o_ref[...] += _dot(hidden, wd_ref[...])

    @pl.when(j == pl.num_programs(1) - 1)
    def _():
        o_ref[...] = x_ref[...] + FFN_RES * _rms(o_ref[...], gpost_ref[...])


def half_ffn(x, g_pre, g_post, w_gate, w_up, w_down, layer, half, *, tm, tf):
    s, d = x.shape
    d_ff = w_gate.shape[3]
    return pl.pallas_call(
        _ffn_kernel,
        grid=(s // tm, d_ff // tf),
        in_specs=[
            pl.BlockSpec((tm, d), lambda i, j: (i, 0)),
            pl.BlockSpec((1, d), lambda i, j: (0, 0)),
            pl.BlockSpec((1, d), lambda i, j: (0, 0)),
            pl.BlockSpec((None, None, d, tf), lambda i, j: (layer, half, 0, j)),
            pl.BlockSpec((None, None, d, tf), lambda i, j: (layer, half, 0, j)),
            pl.BlockSpec((None, None, tf, d), lambda i, j: (layer, half, j, 0)),
        ],
        out_specs=pl.BlockSpec((tm, d), lambda i, j: (i, 0)),
        out_shape=jax.ShapeDtypeStruct((s, d), F32),
        scratch_shapes=[pltpu.VMEM((tm, d), BF16)],
        compiler_params=_params("parallel", "arbitrary"),
        name="half_ffn",
    )(x, g_pre.reshape(1, d), g_post.reshape(1, d), w_gate, w_up, w_down)


def _out_proj_kernel(h_ref, mo_ref, x_ref, wh_ref, wm_ref, g_ref, o_ref):
    y = _dot(h_ref[...].astype(BF16), wh_ref[...]) + _dot(mo_ref[...].astype(BF16), wm_ref[...])
    o_ref[...] = x_ref[...] + _rms(y, g_ref[...])


def out_proj(main, mo, x, w, layer, gain, *, tm):
    s, d = x.shape
    km, ko = main.shape[1], mo.shape[1]
    assert km % ko == 0
    return pl.pallas_call(
        _out_proj_kernel,
        grid=(s // tm,),
        in_specs=[
            pl.BlockSpec((tm, km), lambda i: (i, 0)),
            pl.BlockSpec((tm, ko), lambda i: (i, 0)),
            pl.BlockSpec((tm, d), lambda i: (i, 0)),
            pl.BlockSpec((None, km, d), lambda i: (layer, 0, 0)),
            pl.BlockSpec((None, ko, d), lambda i: (layer, km // ko, 0)),
            pl.BlockSpec((1, d), lambda i: (0, 0)),
        ],
        out_specs=pl.BlockSpec((tm, d), lambda i: (i, 0)),
        out_shape=jax.ShapeDtypeStruct((s, d), F32),
        compiler_params=_params("parallel"),
        name="out_proj",
    )(main, mo, x, w, w, gain.reshape(1, d))


def _mem_attn_kernel(q_ref, kt_ref, v_ref, o_ref):
    scale = MEM_HD ** -0.5
    for h in range(MEM_HEADS):
        cols = slice(h * MEM_HD, (h + 1) * MEM_HD)
        s = _dot(q_ref[:, cols].astype(BF16), kt_ref[cols, :]) * scale
        e = jnp.exp(s - jnp.max(s, axis=-1, keepdims=True))
        p = e / jnp.sum(e, axis=-1, keepdims=True)
        o_ref[:, cols] = _dot(p.astype(BF16), v_ref[:, cols])


def mem_attention(proj, mq_block, mem_kt, mem_v, *, tm):
    s = proj.shape[0]
    m = mem_v.shape[0]
    return pl.pallas_call(
        _mem_attn_kernel,
        grid=(s // tm,),
        in_specs=[
            pl.BlockSpec((tm, MEM_W), lambda i: (i, mq_block)),
            pl.BlockSpec((MEM_W, m), lambda i: (0, 0)),
            pl.BlockSpec((m, MEM_W), lambda i: (0, 0)),
        ],
        out_specs=pl.BlockSpec((tm, MEM_W), lambda i: (i, 0)),
        out_shape=jax.ShapeDtypeStruct((s, MEM_W), F32),
        compiler_params=_params("parallel"),
        name="mem_attention",
    )(proj, mem_kt, mem_v)


def _log_sigmoid(x):
    return jnp.minimum(x, 0.0) - jnp.log1p(jnp.exp(-jnp.abs(x)))


def _mlstm_kernel(q_ref, k_ref, v_ref, o_ref, grow_ref, gcol_ref, brow_ref, bcol_ref, hn_ref,
                  out_ref, c_ref, m_ref):
    chunk = grow_ref.shape[1]

    @pl.when(pl.program_id(0) == 0)
    def _():
        c_ref[...] = jnp.zeros_like(c_ref)
        m_ref[...] = jnp.zeros_like(m_ref)

    grow = grow_ref[...] + brow_ref[...]
    gcol = gcol_ref[...] + bcol_ref[...]
    r_idx = lax.broadcasted_iota(jnp.int32, (chunk, chunk), 0)
    c_idx = lax.broadcasted_iota(jnp.int32, (chunk, chunk), 1)
    causal = c_idx <= r_idx
    b_rows = _exact_dot(_log_sigmoid(grow), (r_idx <= c_idx).astype(BF16))
    b_cols = _exact_dot_left(causal.astype(BF16), _log_sigmoid(gcol))
    ones_col = (lax.broadcasted_iota(jnp.int32, (chunk, LANES), 1) == 0).astype(BF16)

    for h in range(ML_HEADS):
        b_r = b_rows[ML_HEADS + h:ML_HEADS + h + 1, :]
        i_r = grow[h:h + 1, :]
        b_c = b_cols[:, ML_HEADS + h:ML_HEADS + h + 1]
        m_prev = m_ref[h:h + 1, 0:1]
        dmat = jnp.where(causal, b_c - b_r + i_r, -jnp.inf)
        m_inter = b_c + m_prev
        m_vec = jnp.maximum(jnp.max(dmat, axis=-1, keepdims=True), m_inter)
        qk_cols = slice(h * ML_DQK_PAD, (h + 1) * ML_DQK_PAD)
        q = q_ref[:, qk_cols].astype(BF16)
        kt = (k_ref[:, qk_cols] * (ML_DQK ** -0.5)).T
        smat = _dot(q, kt.astype(BF16)) * jnp.exp(dmat - m_vec)
        inter = jnp.exp(m_inter - m_vec)
        vcols = slice(h * ML_DV, (h + 1) * ML_DV)
        v_aug = jnp.concatenate([v_ref[:, vcols].astype(BF16), ones_col], axis=1)
        c_prev = c_ref[h]
        num_aug = _dot(smat.astype(BF16), v_aug) + inter * _dot(q, c_prev.astype(BF16))
        den = num_aug[:, ML_DV:ML_DV + 1]
        hval = num_aug[:, :ML_DV] / jnp.maximum(jnp.abs(den), jnp.exp(-m_vec))
        m_last = m_vec[chunk - 1:chunk, :]
        b_last = b_c[chunk - 1:chunk, :]
        w_r = jnp.exp(b_last - b_r + i_r - m_last)
        decay = jnp.exp(b_last + m_prev - m_last)
        c_ref[h] = decay * c_prev + _dot((kt * w_r).astype(BF16), v_aug)
        m_ref[h:h + 1, :] = jnp.broadcast_to(m_last, (1, LANES))
        out_ref[:, vcols] = _rms(hval, hn_ref[:, vcols]) * jax.nn.sigmoid(o_ref[:, vcols])


def mlstm_heads(proj, q_block, k_block, v_block, o_block, gates_row, gates_col, bias, head_norm):
    s = proj.shape[0]
    chunk = ML_CHUNK
    h2 = 2 * ML_HEADS
    qk_w = ML_HEADS * ML_DQK_PAD
    return pl.pallas_call(
        _mlstm_kernel,
        grid=(s // chunk,),
        in_specs=[
            pl.BlockSpec((chunk, qk_w), lambda c: (c, q_block)),
            pl.BlockSpec((chunk, qk_w), lambda c: (c, k_block)),
            pl.BlockSpec((chunk, MAIN_W), lambda c: (c, v_block)),
            pl.BlockSpec((chunk, MAIN_W), lambda c: (c, o_block)),
            pl.BlockSpec((h2, chunk), lambda c: (0, c)),
            pl.BlockSpec((chunk, h2), lambda c: (c, 0)),
            pl.BlockSpec((h2, 1), lambda c: (0, 0)),
            pl.BlockSpec((1, h2), lambda c: (0, 0)),
            pl.BlockSpec((1, MAIN_W), lambda c: (0, 0)),
        ],
        out_specs=pl.BlockSpec((chunk, MAIN_W), lambda c: (c, 0)),
        out_shape=jax.ShapeDtypeStruct((s, MAIN_W), F32),
        scratch_shapes=[pltpu.VMEM((ML_HEADS, ML_DQK_PAD, ML_DV + LANES), F32), pltpu.VMEM((8, LANES), F32)],
        compiler_params=_params("arbitrary"),
        name="mlstm",
    )(proj, proj, proj, proj, gates_row, gates_col, bias.reshape(h2, 1), bias.reshape(1, h2),
      head_norm.reshape(1, MAIN_W))


def _rope(x, cos2, sin2):
    return x * cos2 + pltpu.roll(x, NSA_HD // 2, axis=1) * sin2


def _kv_prep_kernel(kv_ref, cos_ref, sin_ref, kslc_ref, vslc_ref, kwin_ref, vwin_ref):
    cos2 = cos_ref[...]
    sin2 = sin_ref[...]
    for g in range(NSA_KV):
        def col(c):
            start = (c * NSA_KV + g) * NSA_HD
            return kv_ref[:, start:start + NSA_HD]
        kslc_ref[g] = _rope(col(2), cos2, sin2).astype(BF16)
        vslc_ref[g] = col(3).astype(BF16)
        kwin_ref[g] = _rope(col(4), cos2, sin2).astype(BF16)
        vwin_ref[g] = col(5).astype(BF16)


def kv_prep(kv, cos2, sin2, *, tm):
    s, w = kv.shape
    out = jax.ShapeDtypeStruct((NSA_KV, s, NSA_HD), BF16)
    ospec = pl.BlockSpec((NSA_KV, tm, NSA_HD), lambda i: (0, i, 0))
    return pl.pallas_call(
        _kv_prep_kernel,
        grid=(s // tm,),
        in_specs=[
            pl.BlockSpec((tm, w), lambda i: (i, 0)),
            pl.BlockSpec((tm, NSA_HD), lambda i: (i, 0)),
            pl.BlockSpec((tm, NSA_HD), lambda i: (i, 0)),
        ],
        out_specs=[ospec] * 4,
        out_shape=[out] * 4,
        compiler_params=_params("parallel"),
        name="kv_prep",
    )(kv, cos2, sin2)


def _compress_kernel(seg_ref, pos_ref, w1_ref, w2_ref, o_ref):
    n_seg = seg_ref.shape[0]
    first, second = None, None
    for tok in range(CMP_STRIDE):
        x = seg_ref[:, tok, :]
        for half in range(2):
            row = half * CMP_STRIDE + tok
            part = _dot((x + pos_ref[row:row + 1, :]).astype(BF16), w1_ref[row * NSA_HD:(row + 1) * NSA_HD, :])
            if half == 0:
                first = part if first is None else first + part
            else:
                second = part if second is None else second + part
    hidden = first + pltpu.roll(second, n_seg - 1, axis=0)
    act = jax.nn.gelu(hidden, approximate=True)
    o_ref[...] = _dot(act.astype(BF16), w2_ref[...])


def compress(kv, pos, w1, w2):
    s, w = kv.shape
    n_seg = s // CMP_STRIDE
    hid = w1.shape[2]
    wmap = lambda cg: (cg // NSA_KV, 0, 0)
    return pl.pallas_call(
        _compress_kernel,
        grid=(2 * NSA_KV,),
        in_specs=[
            pl.BlockSpec((n_seg, CMP_STRIDE, NSA_HD), lambda cg: (0, 0, cg)),
            pl.BlockSpec((None, CMP_LEN, NSA_HD), wmap),
            pl.BlockSpec((None, CMP_LEN * NSA_HD, hid), wmap),
            pl.BlockSpec((None, hid, NSA_HD), wmap),
        ],
        out_specs=pl.BlockSpec((None, n_seg, NSA_HD), lambda cg: (cg, 0, 0)),
        out_shape=jax.ShapeDtypeStruct((2 * NSA_KV, n_seg, NSA_HD), F32),
        compiler_params=_params("parallel"),
        name="compress",
    )(kv.reshape(n_seg, CMP_STRIDE, w), pos, w1, w2)


def _rope_t(x, cos2t, sin2t):
    half = NSA_HD // 2
    return x * cos2t + jnp.concatenate([x[half:], x[:half]], axis=0) * sin2t


def _col_max(chunks):
    return jnp.max(functools.reduce(jnp.maximum, chunks), axis=0, keepdims=True)


def _nsa_kernel(q_ref, cos_ref, sin_ref, gl_ref, gb_ref, kc_ref, vct_ref, kslc_ref, vslct_ref, kwin_ref, vwint_ref,
                o_ref, qt_ref, qaug_ref, sel_ref, p_ref, acc_ref, s_ref, m_ref, ocmp_ref):
    qi = pl.program_id(1)
    qb = Q_BLOCK
    n_cmp = kc_ref.shape[0]
    n_sel = n_cmp // 4
    t0 = qi * qb
    t_row = t0 + lax.broadcasted_iota(jnp.int32, (1, qb), 1)
    hcols = [slice(j * qb, (j + 1) * qb) for j in range(NSA_GROUP)]

    cos2t = cos_ref[...]
    sin2t = sin_ref[...]
    for j in range(NSA_GROUP):
        qt = q_ref[:, j * NSA_HD:(j + 1) * NSA_HD].T * (NSA_HD ** -0.5 * LOG2E)
        qt_ref[:, hcols[j]] = qt.astype(BF16)
        qaug_ref[0:NSA_HD, hcols[j]] = _rope_t(qt, cos2t, sin2t).astype(BF16)

    def compressed_and_select(n_chunks):
        rows = 4 * SEL_CHUNK * n_chunks
        n_blk = SEL_CHUNK * n_chunks
        s_c = _dot(kc_ref[0:rows, :], qt_ref[...])
        blk_i = lax.broadcasted_iota(jnp.int32, (SEL_CHUNK, qb), 0)
        pieces = [(c, r) for c in range(n_chunks) for r in range(4)]

        def piece_rows(c, r):
            return slice((4 * c + r) * SEL_CHUNK, (4 * c + r + 1) * SEL_CHUNK)

        cmp_bias = {(c, r): jnp.where(SLC_LEN * (blk_i + SEL_CHUNK * c) + (CMP_STRIDE * r + CMP_LEN - 1) <= t_row,
                                      0.0, NEG) for c, r in pieces}
        has_valid = (t_row >= CMP_LEN - 1).astype(F32)
        imp = {cr: jnp.zeros((SEL_CHUNK, qb), F32) for cr in pieces}
        for j in range(NSA_GROUP):
            sb = {cr: s_c[piece_rows(*cr), hcols[j]] + cmp_bias[cr] for cr in pieces}
            m = _col_max(list(sb.values()))
            e = {cr: jnp.exp2(sb[cr] - m) for cr in pieces}
            den = functools.reduce(jnp.add, [jnp.sum(x, axis=0, keepdims=True) for x in e.values()])
            rinv = has_valid / jnp.maximum(den, TINY)
            for cr in pieces:
                p = e[cr] * rinv
                imp[cr] = imp[cr] + p
                p_ref[piece_rows(*cr), hcols[j]] = p.astype(BF16)
        ocmp_ref[...] = _dot(vct_ref[:, 0:rows], p_ref[0:rows, :])

        imp_r = [jnp.concatenate([imp[(c, r)] for c in range(n_chunks)], axis=0) for r in range(4)]
        blk = lax.broadcasted_iota(jnp.int32, (n_blk, qb), 0)
        imp3_prev = jnp.where(blk == 0, 0.0, pltpu.roll(imp_r[3], 1, axis=0))
        p_slc = ((((((imp_r[0] + imp3_prev) + imp_r[1]) + imp_r[0]) + imp_r[2]) + imp_r[1]) + imp_r[3]) + imp_r[2]
        cur = jnp.right_shift(t_row, 6)
        forced = (blk == 0) | (blk == cur) | (blk == cur - 1)
        score = jnp.where(blk > cur, NEG, jnp.where(forced, BIG, p_slc))
        blk_f = blk.astype(F32)
        sel = jnp.zeros((n_blk, qb), F32)
        for _ in range(SLC_TOPK):
            top = jnp.max(score, axis=0, keepdims=True)
            first = jnp.min(jnp.where(score == top, blk_f, float(n_blk)), axis=0, keepdims=True)
            hit = blk_f == first
            sel = jnp.where(hit, 1.0, sel)
            score = jnp.where(hit, REMOVED, score)
        sel_ref[0:n_blk, :] = sel
        if n_blk < n_sel:
            sel_ref[n_blk:n_sel, :] = jnp.zeros((n_sel - n_blk, qb), F32)

    chunks_needed = (t0 + qb - 1) // (SLC_LEN * SEL_CHUNK) + 1
    for n_chunks in range(1, n_sel // SEL_CHUNK + 1):
        pl.when(chunks_needed == n_chunks)(functools.partial(compressed_and_select, n_chunks))

    acc_ref[...] = jnp.zeros_like(acc_ref)
    m_ref[...] = jnp.full_like(m_ref, REMOVED)
    tiles_per_window = LANES * SLC_LEN // SLC_TILE
    kpos0 = lax.broadcasted_iota(jnp.int32, (SLC_TILE, qb), 0)

    def scores(kt, slot):
        first = pl.multiple_of((kt // tiles_per_window) * LANES, LANES)
        unselected = ((sel_ref[pl.ds(first, LANES), :] - 1.0) * BIG).astype(BF16)
        for j in range(NSA_GROUP):
            qaug_ref[NSA_HD:NSA_HD + LANES, hcols[j]] = unselected
        start = pl.multiple_of(kt * SLC_TILE, SLC_TILE)
        s_ref[slot] = _dot(kslc_ref[pl.ds(start, SLC_TILE), :], qaug_ref[...])

    def weights(kt, slot, causal):
        if causal:
            future = jnp.where(kpos0 + kt * SLC_TILE <= t_row, 0.0, NEG)
        alpha = []
        for j in range(NSA_GROUP):
            s = s_ref[slot, :, hcols[j]]
            if causal:
                s = s + future
            m_old = m_ref[0:1, hcols[j]]
            m_j = jnp.maximum(m_old, jnp.max(s, axis=0, keepdims=True))
            p_ref[slot * SLC_TILE:(slot + 1) * SLC_TILE, hcols[j]] = jnp.exp2(s - m_j).astype(BF16)
            m_ref[0:1, hcols[j]] = m_j
            alpha.append(jnp.exp2(m_old - m_j))
        return jnp.concatenate(alpha, axis=1)

    def weighted_values(kt, slot):
        start = pl.multiple_of(kt * SLC_TILE, SLC_TILE)
        return _dot(vslct_ref[:, pl.ds(start, SLC_TILE)], p_ref[slot * SLC_TILE:(slot + 1) * SLC_TILE, :])

    def absorb(kt, slot, causal):
        alpha = weights(kt, slot, causal)
        acc_ref[...] = acc_ref[...] * alpha + weighted_values(kt, slot)

    last = (t0 + qb - 1) // SLC_TILE
    scores(0, 0)

    def pair(i, carry):
        scores(2 * i + 1, 1)
        absorb(2 * i, 0, False)
        scores(jnp.minimum(2 * i + 2, last), 0)
        absorb(2 * i + 1, 1, False)
        return carry

    def quad(i, carry):
        return pair(2 * i + 1, pair(2 * i, carry))

    lax.fori_loop(0, last // 4, quad, 0)
    lax.fori_loop(2 * (last // 4), last // 2, pair, 0)

    @pl.when(last % 2 == 0)
    def _():
        absorb(last, 0, True)

    @pl.when(last % 2 == 1)
    def _():
        scores(last, 1)
        absorb(last - 1, 0, False)
        absorb(last, 1, True)

    acc = acc_ref[...]
    o_slc = acc[:NSA_HD] / acc[NSA_HD:NSA_HD + 1]

    span = qb + WIN
    start = pl.multiple_of(jnp.maximum(t0 - WIN, 0), qb)
    s_w = _dot(kwin_ref[pl.ds(start, span), :], qaug_ref[0:NSA_HD, :])
    spos0 = lax.broadcasted_iota(jnp.int32, (qb, qb), 0)
    win_bias = []
    for c in range(span // qb):
        spos = spos0 + (start + c * qb)
        win_bias.append(jnp.where((spos <= t_row) & (spos > t_row - WIN), 0.0, NEG))
    for j in range(NSA_GROUP):
        sb = [s_w[c * qb:(c + 1) * qb, hcols[j]] + win_bias[c] for c in range(span // qb)]
        m = _col_max(sb)
        for c in range(span // qb):
            p_ref[c * qb:(c + 1) * qb, hcols[j]] = jnp.exp2(sb[c] - m).astype(BF16)
    pw = _dot(vwint_ref[:, pl.ds(start, span)], p_ref[0:span, :])
    o_win = pw[:NSA_HD] / pw[NSA_HD:NSA_HD + 1]

    gates_t = jax.nn.sigmoid(gl_ref[...] + gb_ref[...]).T
    for j in range(NSA_GROUP):
        o_t = (gates_t[3 * j:3 * j + 1] * ocmp_ref[:, hcols[j]]
               + gates_t[3 * j + 1:3 * j + 2] * o_slc[:, hcols[j]]
               + gates_t[3 * j + 2:3 * j + 3] * o_win[:, hcols[j]])
        o_ref[:, j * NSA_HD:(j + 1) * NSA_HD] = o_t.T


def nsa_attention(proj, cos2t, sin2t, gate_logits, gate_bias, kc, vct, kslc, vslct, kwin, vwint):
    s = proj.shape[0]
    n_cmp = kc.shape[1]
    rows_aug = NSA_HD + ONES_ROWS
    width = NSA_GROUP * Q_BLOCK
    qspec = pl.BlockSpec((Q_BLOCK, GROUP_W), lambda g, i: (i, g))
    tspec = pl.BlockSpec((NSA_HD, Q_BLOCK), lambda g, i: (0, i))

    def resident(shape):
        return pl.BlockSpec((None,) + shape, lambda g, i: (g, 0, 0), pipeline_mode=pl.Buffered(1))

    return pl.pallas_call(
        _nsa_kernel,
        grid=(NSA_KV, s // Q_BLOCK),
        in_specs=[
            qspec, tspec, tspec,
            pl.BlockSpec((Q_BLOCK, LANES), lambda g, i: (i, g)),
            pl.BlockSpec((1, LANES), lambda g, i: (0, g)),
            resident((n_cmp, NSA_HD)), resident((NSA_HD, n_cmp)),
            resident((s, NSA_HD + LANES)), resident((rows_aug, s)),
            resident((s, NSA_HD)), resident((rows_aug, s)),
        ],
        out_specs=qspec,
        out_shape=jax.ShapeDtypeStruct((s, NSA_KV * GROUP_W), F32),
        scratch_shapes=[
            pltpu.VMEM((NSA_HD, width), BF16),
            pltpu.VMEM((NSA_HD + LANES, width), BF16),
            pltpu.VMEM((n_cmp // 4, Q_BLOCK), F32),
            pltpu.VMEM((max(n_cmp, Q_BLOCK + WIN, 2 * SLC_TILE), width), BF16),
            pltpu.VMEM((rows_aug, width), F32),
            pltpu.VMEM((2, SLC_TILE, width), F32),
            pltpu.VMEM((8, width), F32),
            pltpu.VMEM((NSA_HD, width), F32),
        ],
        compiler_params=_params("parallel", "arbitrary"),
        name="nsa_attention",
    )(proj, cos2t, sin2t, gate_logits, gate_bias, kc, vct, kslc, vslct, kwin, vwint)


def _row_tile(s, want):
    return want if s % want == 0 else s


def _rope_tables(seq):
    inv = ROPE_THETA ** (-jnp.arange(0, NSA_HD, 2, dtype=F32) / NSA_HD)
    ang = jnp.arange(seq, dtype=F32)[:, None] * inv[None, :]
    cos, sin = jnp.cos(ang), jnp.sin(ang)
    return jnp.concatenate([cos, cos], axis=1), jnp.concatenate([-sin, sin], axis=1)


def _pad_cols(a, width):
    return jnp.pad(a, ((0, 0), (0, width - a.shape[1])))


def _mem_kv(mem, gain, w_kv):
    kv = rms_matmul(mem, gain, w_kv.astype(BF16), tm=mem.shape[0], tn=MEM_W, out_dtype=BF16)
    return kv[:, :MEM_W].T, kv[:, MEM_W:]


def _mlstm_layer_heads(x, gain, w_in, gate_bias, head_norm, mem_kt, mem_v):
    s = x.shape[0]
    qk_w = ML_HEADS * ML_DQK
    w_q, w_k, w_v, w_o, w_g, w_mq = jnp.split(
        w_in, [qk_w, 2 * qk_w, 2 * qk_w + MAIN_W, 2 * qk_w + 2 * MAIN_W, 2 * qk_w + 2 * MAIN_W + 2 * ML_HEADS], axis=1)
    def head_slots(w):
        w = w.reshape(w.shape[0], ML_HEADS, ML_DQK)
        return jnp.pad(w, ((0, 0), (0, 0), (0, ML_DQK_PAD - ML_DQK))).reshape(w.shape[0], ML_HEADS * ML_DQK_PAD)

    w_main = jnp.concatenate([w_v, w_o, head_slots(w_q), head_slots(w_k), w_mq], axis=1).astype(BF16)
    qk_pad_w = ML_HEADS * ML_DQK_PAD
    q_block = 2 * MAIN_W // qk_pad_w
    mq_block = (2 * MAIN_W + 2 * qk_pad_w) // MEM_W
    assert 2 * MAIN_W % qk_pad_w == 0 and (2 * MAIN_W + 2 * qk_pad_w) % MEM_W == 0
    tm = _row_tile(s, 1024)
    proj, gates = rms_matmul(x, gain, w_main, _pad_cols(w_g, LANES).astype(BF16), tm=tm, tn=512)
    gates = gates[:, :2 * ML_HEADS]
    h = mlstm_heads(proj, q_block, q_block + 1, 0, 1, gates.T, gates, gate_bias, head_norm)
    mo = mem_attention(proj, mq_block, mem_kt, mem_v, tm=_row_tile(s, 512))
    return h, mo


def _with_ones_rows(v):
    g, s, _ = v.shape
    return jnp.concatenate([v.transpose(0, 2, 1), jnp.ones((g, ONES_ROWS, s), v.dtype)], axis=1)


def _shared_kv(x, kv_norm, w_kv, cmp_pos, cmp_w1, cmp_w2, cos2, sin2):
    s = x.shape[0]
    n_seg = s // CMP_STRIDE
    n_sel = s // SLC_LEN
    seg_w = CMP_STRIDE * NSA_HD
    kv = rms_matmul(x, kv_norm, w_kv.astype(BF16), tm=_row_tile(s, 1024), tn=512)
    kslc, vslc, kwin, vwin = kv_prep(kv, cos2, sin2, tm=_row_tile(s, 512))
    cmp = compress(kv, cmp_pos, cmp_w1.astype(BF16), cmp_w2.astype(BF16))
    cmp = cmp.reshape(2 * NSA_KV, n_sel // SEL_CHUNK, SEL_CHUNK, 4, NSA_HD).transpose(0, 1, 3, 2, 4)
    cmp = cmp.reshape(2 * NSA_KV, n_seg, NSA_HD)
    cmp = cmp.astype(BF16)
    block_in_window = (jnp.arange(s, dtype=jnp.int32) // SLC_LEN) % LANES
    onehot = (block_in_window[:, None] == jnp.arange(LANES, dtype=jnp.int32)[None, :]).astype(BF16)
    kslc_aug = jnp.concatenate([kslc, jnp.broadcast_to(onehot[None], (NSA_KV, s, LANES))], axis=2)
    return dict(kc=cmp[:NSA_KV], vct=cmp[NSA_KV:].transpose(0, 2, 1), kslc=kslc_aug, vslct=_with_ones_rows(vslc),
                kwin=kwin, vwint=_with_ones_rows(vwin))


def _nsa_layer_heads(x, gain, w_in, gate_bias, shared, cos2t, sin2t, mem_kt, mem_v):
    s = x.shape[0]
    n_gate = 3 * NSA_GROUP
    w_q, w_g, w_mq = jnp.split(w_in, [MAIN_W, MAIN_W + 3 * NSA_HEADS], axis=1)
    w_main = jnp.concatenate([w_q, w_mq], axis=1).astype(BF16)
    w_gate = jnp.concatenate([_pad_cols(w_g[:, g * n_gate:(g + 1) * n_gate], LANES) for g in range(NSA_KV)], axis=1)
    b_gate = jnp.concatenate(
        [_pad_cols(gate_bias[None, g * n_gate:(g + 1) * n_gate], LANES) for g in range(NSA_KV)], axis=1)
    tm = _row_tile(s, 1024)
    proj, gate_logits = rms_matmul(x, gain, w_main, w_gate.astype(BF16), tm=tm, tn=512)
    o = nsa_attention(proj, cos2t, sin2t, gate_logits, b_gate, shared["kc"], shared["vct"], shared["kslc"],
                      shared["vslct"], shared["kwin"], shared["vwint"])
    mo = mem_attention(proj, MAIN_W // MEM_W, mem_kt, mem_v, tm=_row_tile(s, 512))
    return o, mo


def kernel(x, mem, norm_gains, ffn_w_gate, ffn_w_up, ffn_w_down, mem_norm, mem_w_kv, w_out, a_w_in, a_gate_bias,
           a_head_norm, kv_norm, w_kv, cmp_pos, cmp_w1, cmp_w2, b_w_in, b_gate_bias):
    batch, seq, d_model = x.shape
    depth = norm_gains.shape[0]
    n_a = a_w_in.shape[0]
    assert seq % (SLC_LEN * LANES) == 0 and seq % ML_CHUNK == 0, "sequence must tile the selection-block lanes"
    cos2, sin2 = _rope_tables(seq)
    cos2t, sin2t = cos2.T, sin2.T
    tm_ffn = _row_tile(seq, 512)

    w_gate16, w_up16, w_down16 = ffn_w_gate.astype(BF16), ffn_w_up.astype(BF16), ffn_w_down.astype(BF16)
    w_out16 = w_out.astype(BF16)

    def ffn(xb, layer, half, g_pre, g_post):
        return half_ffn(xb, g_pre, g_post, w_gate16, w_up16, w_down16, layer, half, tm=tm_ffn, tf=512)

    outs = []
    for b in range(batch):
        xb = x[b]
        shared = None
        for layer in range(depth):
            if layer == n_a:
                shared = _shared_kv(xb, kv_norm, w_kv, cmp_pos, cmp_w1, cmp_w2, cos2, sin2)
            g = norm_gains[layer]
            xb = ffn(xb, layer, 0, g[0], g[1])
            mem_kt, mem_v = _mem_kv(mem[b], mem_norm[layer], mem_w_kv[layer])
            if layer < n_a:
                main, mo = _mlstm_layer_heads(xb, g[2], a_w_in[layer], a_gate_bias[layer], a_head_norm[layer],
                                              mem_kt, mem_v)
            else:
                lb = layer - n_a
                main, mo = _nsa_layer_heads(xb, g[2], b_w_in[lb], b_gate_bias[lb], shared, cos2t, sin2t, mem_kt, mem_v)
            xb = out_proj(main, mo, xb, w_out16, layer, g[3], tm=_row_tile(seq, 256))
            xb = ffn(xb, layer, 1, g[4], g[5])
        outs.append(xb)
    return jnp.stack(outs, axis=0)
```

```python
import functools

import jax
import jax.numpy as jnp
from jax import lax
from jax.experimental import pallas as pl
from jax.experimental.pallas import tpu as pltpu

F32 = jnp.float32
BF16 = jnp.bfloat16

MEM_HEADS = 4
MEM_HD = 128
MEM_W = MEM_HEADS * MEM_HD
ML_HEADS = 4
ML_DV = 384
ML_DQK = 192
MAIN_W = ML_HEADS * ML_DV
NSA_HEADS = 12
NSA_HD = 128
NSA_KV = 2
NSA_GROUP = NSA_HEADS // NSA_KV
GROUP_W = NSA_GROUP * NSA_HD
CMP_LEN = 32
CMP_STRIDE = 16
SLC_LEN = 64
SLC_TOPK = 16
WIN = 512
FFN_RES = 0.5
ROPE_THETA = 10000.0
EPS = 1e-6
NEG = -1e30
BIG = 1e30
REMOVED = -3e38
LOG2E = 1.4426950408889634
TINY = 1e-30

LANES = 128
VMEM_LIMIT_BYTES = 56 * 1024 * 1024

ML_CHUNK = 256
ML_DQK_PAD = 256
NSA_QB = 128
SLC_TILE = 512
ONES_ROWS = 16
SEL_CHUNK = 64


def _params(*semantics):
    return pltpu.CompilerParams(dimension_semantics=semantics, vmem_limit_bytes=VMEM_LIMIT_BYTES)


def _rms(x, gain):
    return x * lax.rsqrt(jnp.mean(x * x, axis=-1, keepdims=True) + EPS) * gain


def _dot(a, b):
    return jnp.dot(a, b, preferred_element_type=F32)


def _exact_dot(a, b01):
    a1 = a.astype(BF16)
    r1 = a - a1.astype(F32)
    a2 = r1.astype(BF16)
    a3 = (r1 - a2.astype(F32)).astype(BF16)
    return _dot(a1, b01) + _dot(a2, b01) + _dot(a3, b01)


def _exact_dot_left(b01, a):
    a1 = a.astype(BF16)
    r1 = a - a1.astype(F32)
    a2 = r1.astype(BF16)
    a3 = (r1 - a2.astype(F32)).astype(BF16)
    return _dot(b01, a1) + _dot(b01, a2) + _dot(b01, a3)


def _rms_matmul_kernel(x_ref, g_ref, w_ref, o_ref, xn_ref):
    @pl.when(pl.program_id(1) == 0)
    def _():
        xn_ref[...] = _rms(x_ref[...], g_ref[...]).astype(BF16)

    o_ref[...] = _dot(xn_ref[...], w_ref[...]).astype(o_ref.dtype)


def _rms_matmul_side_kernel(x_ref, g_ref, w_ref, wside_ref, o_ref, oside_ref, xn_ref):
    @pl.when(pl.program_id(1) == 0)
    def _():
        xn_ref[...] = _rms(x_ref[...], g_ref[...]).astype(BF16)
        oside_ref[...] = _dot(xn_ref[...], wside_ref[...])

    o_ref[...] = _dot(xn_ref[...], w_ref[...]).astype(o_ref.dtype)


def rms_matmul(x, gain, w, w_side=None, *, tm, tn, out_dtype=F32):
    s, k = x.shape
    n = w.shape[1]
    in_specs = [
        pl.BlockSpec((tm, k), lambda i, j: (i, 0)),
        pl.BlockSpec((1, k), lambda i, j: (0, 0)),
        pl.BlockSpec((k, tn), lambda i, j: (0, j)),
    ]
    out_specs = pl.BlockSpec((tm, tn), lambda i, j: (i, j))
    out_shape = jax.ShapeDtypeStruct((s, n), out_dtype)
    args = (x, gain.reshape(1, k), w)
    body = _rms_matmul_kernel
    if w_side is not None:
        n_side = w_side.shape[1]
        in_specs.append(pl.BlockSpec((k, n_side), lambda i, j: (0, 0)))
        out_specs = [out_specs, pl.BlockSpec((tm, n_side), lambda i, j: (i, 0))]
        out_shape = [out_shape, jax.ShapeDtypeStruct((s, n_side), F32)]
        args = args + (w_side,)
        body = _rms_matmul_side_kernel
    return pl.pallas_call(
        body,
        grid=(s // tm, n // tn),
        in_specs=in_specs,
        out_specs=out_specs,
        out_shape=out_shape,
        scratch_shapes=[pltpu.VMEM((tm, k), BF16)],
        compiler_params=_params("parallel", "arbitrary"),
        name="rms_matmul",
    )(*args)


def _ffn_kernel(x_ref, gpre_ref, gpost_ref, wg_ref, wu_ref, wd_ref, o_ref, xn_ref):
    j = pl.program_id(1)

    @pl.when(j == 0)
    def _():
        xn_ref[...] = _rms(x_ref[...], gpre_ref[...]).astype(BF16)
        o_ref[...] = jnp.zeros_like(o_ref)

    xn = xn_ref[...]
    gate = _dot(xn, wg_ref[...])
    up = _dot(xn, wu_ref[...])
    hidden = (gate * jax.nn.sigmoid(gate) * up).astype(BF16)
    o_ref[...] += _dot(hidden, wd_ref[...])

    @pl.when(j == pl.num_programs(1) - 1)
    def _():
        o_ref[...] = x_ref[...] + FFN_RES * _rms(o_ref[...], gpost_ref[...])


def half_ffn(x, g_pre, g_post, w_gate, w_up, w_down, layer, half, *, tm, tf):
    s, d = x.shape
    d_ff = w_gate.shape[3]
    return pl.pallas_call(
        _ffn_kernel,
        grid=(s // tm, d_ff // tf),
        in_specs=[
            pl.BlockSpec((tm, d), lambda i, j: (i, 0)),
            pl.BlockSpec((1, d), lambda i, j: (0, 0)),
            pl.BlockSpec((1, d), lambda i, j: (0, 0)),
            pl.BlockSpec((None, None, d, tf), lambda i, j: (layer, half, 0, j)),
            pl.BlockSpec((None, None, d, tf), lambda i, j: (layer, half, 0, j)),
            pl.BlockSpec((None, None, tf, d), lambda i, j: (layer, half, j, 0)),
        ],
        out_specs=pl.BlockSpec((tm, d), lambda i, j: (i, 0)),
        out_shape=jax.ShapeDtypeStruct((s, d), F32),
        scratch_shapes=[pltpu.VMEM((tm, d), BF16)],
        compiler_params=_params("parallel", "arbitrary"),
        name="half_ffn",
    )(x, g_pre.reshape(1, d), g_post.reshape(1, d), w_gate, w_up, w_down)


def _out_proj_kernel(h_ref, mq_ref, kt_ref, v_ref, x_ref, wh_ref, wm_ref, g_ref, o_ref):
    scale = MEM_HD ** -0.5
    heads = []
    for h in range(MEM_HEADS):
        cols = slice(h * MEM_HD, (h + 1) * MEM_HD)
        s = _dot(mq_ref[:, cols].astype(BF16), kt_ref[cols, :]) * scale
        e = jnp.exp(s - jnp.max(s, axis=-1, keepdims=True))
        p = e / jnp.sum(e, axis=-1, keepdims=True)
        heads.append(_dot(p.astype(BF16), v_ref[:, cols]).astype(BF16))
    y = _dot(h_ref[...].astype(BF16), wh_ref[...]) + _dot(jnp.concatenate(heads, axis=1), wm_ref[...])
    o_ref[...] = x_ref[...] + _rms(y, g_ref[...])


def out_proj(main, proj, mq_block, mem_kt, mem_v, x, w, layer, gain, *, tm):
    s, d = x.shape
    km = main.shape[1]
    m = mem_v.shape[0]
    assert km % MEM_W == 0
    return pl.pallas_call(
        _out_proj_kernel,
        grid=(s // tm,),
        in_specs=[
            pl.BlockSpec((tm, km), lambda i: (i, 0)),
            pl.BlockSpec((tm, MEM_W), lambda i: (i, mq_block)),
            pl.BlockSpec((MEM_W, m), lambda i: (0, 0)),
            pl.BlockSpec((m, MEM_W), lambda i: (0, 0)),
            pl.BlockSpec((tm, d), lambda i: (i, 0)),
            pl.BlockSpec((None, km, d), lambda i: (layer, 0, 0)),
            pl.BlockSpec((None, MEM_W, d), lambda i: (layer, km // MEM_W, 0)),
            pl.BlockSpec((1, d), lambda i: (0, 0)),
        ],
        out_specs=pl.BlockSpec((tm, d), lambda i: (i, 0)),
        out_shape=jax.ShapeDtypeStruct((s, d), F32),
        compiler_params=_params("parallel"),
        name="out_proj",
    )(main, proj, mem_kt, mem_v, x, w, w, gain.reshape(1, d))


def _log_sigmoid(x):
    return jnp.minimum(x, 0.0) - jnp.log1p(jnp.exp(-jnp.abs(x)))


def _mlstm_kernel(q_ref, k_ref, v_ref, o_ref, grow_ref, gcol_ref, brow_ref, bcol_ref, hn_ref,
                  out_ref, c_ref, m_ref):
    chunk = grow_ref.shape[1]

    @pl.when(pl.program_id(0) == 0)
    def _():
        c_ref[...] = jnp.zeros_like(c_ref)
        m_ref[...] = jnp.zeros_like(m_ref)

    grow = grow_ref[...] + brow_ref[...]
    gcol = gcol_ref[...] + bcol_ref[...]
    r_idx = lax.broadcasted_iota(jnp.int32, (chunk, chunk), 0)
    c_idx = lax.broadcasted_iota(jnp.int32, (chunk, chunk), 1)
    causal = c_idx <= r_idx
    b_rows = _exact_dot(_log_sigmoid(grow), (r_idx <= c_idx).astype(BF16))
    b_cols = _exact_dot_left(causal.astype(BF16), _log_sigmoid(gcol))
    ones_col = (lax.broadcasted_iota(jnp.int32, (chunk, LANES), 1) == 0).astype(BF16)

    for h in range(ML_HEADS):
        b_r = b_rows[ML_HEADS + h:ML_HEADS + h + 1, :]
        i_r = grow[h:h + 1, :]
        b_c = b_cols[:, ML_HEADS + h:ML_HEADS + h + 1]
        m_prev = m_ref[h:h + 1, 0:1]
        dmat = jnp.where(causal, b_c - b_r + i_r, -jnp.inf)
        m_inter = b_c + m_prev
        m_vec = jnp.maximum(jnp.max(dmat, axis=-1, keepdims=True), m_inter)
        qk_cols = slice(h * ML_DQK_PAD, (h + 1) * ML_DQK_PAD)
        q = q_ref[:, qk_cols].astype(BF16)
        kt = (k_ref[:, qk_cols] * (ML_DQK ** -0.5)).T
        smat = _dot(q, kt.astype(BF16)) * jnp.exp(dmat - m_vec)
        inter = jnp.exp(m_inter - m_vec)
        vcols = slice(h * ML_DV, (h + 1) * ML_DV)
        v_aug = jnp.concatenate([v_ref[:, vcols].astype(BF16), ones_col], axis=1)
        c_prev = c_ref[h]
        num_aug = _dot(smat.astype(BF16), v_aug) + inter * _dot(q, c_prev.astype(BF16))
        den = num_aug[:, ML_DV:ML_DV + 1]
        hval = num_aug[:, :ML_DV] / jnp.maximum(jnp.abs(den), jnp.exp(-m_vec))
        m_last = m_vec[chunk - 1:chunk, :]
        b_last = b_c[chunk - 1:chunk, :]
        w_r = jnp.exp(b_last - b_r + i_r - m_last)
        decay = jnp.exp(b_last + m_prev - m_last)
        c_ref[h] = decay * c_prev + _dot((kt * w_r).astype(BF16), v_aug)
        m_ref[h:h + 1, :] = jnp.broadcast_to(m_last, (1, LANES))
        out_ref[:, vcols] = _rms(hval, hn_ref[:, vcols]) * jax.nn.sigmoid(o_ref[:, vcols])


def mlstm_heads(proj, q_block, k_block, v_block, o_block, gates_row, gates_col, bias, head_norm):
    s = proj.shape[0]
    chunk = ML_CHUNK
    h2 = 2 * ML_HEADS
    qk_w = ML_HEADS * ML_DQK_PAD
    return pl.pallas_call(
        _mlstm_kernel,
        grid=(s // chunk,),
        in_specs=[
            pl.BlockSpec((chunk, qk_w), lambda c: (c, q_block)),
            pl.BlockSpec((chunk, qk_w), lambda c: (c, k_block)),
            pl.BlockSpec((chunk, MAIN_W), lambda c: (c, v_block)),
            pl.BlockSpec((chunk, MAIN_W), lambda c: (c, o_block)),
            pl.BlockSpec((h2, chunk), lambda c: (0, c)),
            pl.BlockSpec((chunk, h2), lambda c: (c, 0)),
            pl.BlockSpec((h2, 1), lambda c: (0, 0)),
            pl.BlockSpec((1, h2), lambda c: (0, 0)),
            pl.BlockSpec((1, MAIN_W), lambda c: (0, 0)),
        ],
        out_specs=pl.BlockSpec((chunk, MAIN_W), lambda c: (c, 0)),
        out_shape=jax.ShapeDtypeStruct((s, MAIN_W), F32),
        scratch_shapes=[pltpu.VMEM((ML_HEADS, ML_DQK_PAD, ML_DV + LANES), F32), pltpu.VMEM((8, LANES), F32)],
        compiler_params=_params("arbitrary"),
        name="mlstm",
    )(proj, proj, proj, proj, gates_row, gates_col, bias.reshape(h2, 1), bias.reshape(1, h2),
      head_norm.reshape(1, MAIN_W))


def _rope(x, cos2, sin2):
    return x * cos2 + pltpu.roll(x, NSA_HD // 2, axis=1) * sin2


def _kv_prep_kernel(kv_ref, cos_ref, sin_ref, kslc_ref, vslc_ref, kwin_ref, vwin_ref):
    cos2 = cos_ref[...]
    sin2 = sin_ref[...]
    for g in range(NSA_KV):
        def col(c):
            start = (c * NSA_KV + g) * NSA_HD
            return kv_ref[:, start:start + NSA_HD]
        kslc_ref[g] = _rope(col(2), cos2, sin2).astype(BF16)
        vslc_ref[g] = col(3).astype(BF16)
        kwin_ref[g] = _rope(col(4), cos2, sin2).astype(BF16)
        vwin_ref[g] = col(5).astype(BF16)


def kv_prep(kv, cos2, sin2, *, tm):
    s, w = kv.shape
    out = jax.ShapeDtypeStruct((NSA_KV, s, NSA_HD), BF16)
    ospec = pl.BlockSpec((NSA_KV, tm, NSA_HD), lambda i: (0, i, 0))
    return pl.pallas_call(
        _kv_prep_kernel,
        grid=(s // tm,),
        in_specs=[
            pl.BlockSpec((tm, w), lambda i: (i, 0)),
            pl.BlockSpec((tm, NSA_HD), lambda i: (i, 0)),
            pl.BlockSpec((tm, NSA_HD), lambda i: (i, 0)),
        ],
        out_specs=[ospec] * 4,
        out_shape=[out] * 4,
        compiler_params=_params("parallel"),
        name="kv_prep",
    )(kv, cos2, sin2)


def _compress_kernel(seg_ref, pos_ref, w1_ref, w2_ref, o_ref):
    n_seg = seg_ref.shape[0]
    first, second = None, None
    for tok in range(CMP_STRIDE):
        x = seg_ref[:, tok, :]
        for half in range(2):
            row = half * CMP_STRIDE + tok
            part = _dot((x + pos_ref[row:row + 1, :]).astype(BF16), w1_ref[row * NSA_HD:(row + 1) * NSA_HD, :])
            if half == 0:
                first = part if first is None else first + part
            else:
                second = part if second is None else second + part
    hidden = first + pltpu.roll(second, n_seg - 1, axis=0)
    act = jax.nn.gelu(hidden, approximate=True)
    o_ref[...] = _dot(act.astype(BF16), w2_ref[...])


def compress(kv, pos, w1, w2):
    s, w = kv.shape
    n_seg = s // CMP_STRIDE
    hid = w1.shape[2]
    wmap = lambda cg: (cg // NSA_KV, 0, 0)
    return pl.pallas_call(
        _compress_kernel,
        grid=(2 * NSA_KV,),
        in_specs=[
            pl.BlockSpec((n_seg, CMP_STRIDE, NSA_HD), lambda cg: (0, 0, cg)),
            pl.BlockSpec((None, CMP_LEN, NSA_HD), wmap),
            pl.BlockSpec((None, CMP_LEN * NSA_HD, hid), wmap),
            pl.BlockSpec((None, hid, NSA_HD), wmap),
        ],
        out_specs=pl.BlockSpec((None, n_seg, NSA_HD), lambda cg: (cg, 0, 0)),
        out_shape=jax.ShapeDtypeStruct((2 * NSA_KV, n_seg, NSA_HD), F32),
        compiler_params=_params("parallel"),
        name="compress",
    )(kv.reshape(n_seg, CMP_STRIDE, w), pos, w1, w2)


def _rope_t(x, cos2t, sin2t):
    half = NSA_HD // 2
    return x * cos2t + jnp.concatenate([x[half:], x[:half]], axis=0) * sin2t


def _col_max(chunks):
    return jnp.max(functools.reduce(jnp.maximum, chunks), axis=0, keepdims=True)


def _nsa_kernel(q_ref, cos_ref, sin_ref, gl_ref, gb_ref, kc_ref, vct_ref, kslc_ref, vslct_ref, kwin_ref, vwint_ref,
                o_ref, qt_ref, qaug_ref, sel_ref, p_ref, acc_ref, s_ref, m_ref, ocmp_ref, owin_ref):
    qi = pl.program_id(1)
    qb = NSA_QB
    n_cmp = kc_ref.shape[0]
    n_sel = n_cmp // 4
    t0 = qi * qb
    t_row = t0 + lax.broadcasted_iota(jnp.int32, (1, qb), 1)
    hcols = [slice(j * qb, (j + 1) * qb) for j in range(NSA_GROUP)]

    cos2t = cos_ref[...]
    sin2t = sin_ref[...]
    for j in range(NSA_GROUP):
        qt = q_ref[:, j * NSA_HD:(j + 1) * NSA_HD].T * (NSA_HD ** -0.5 * LOG2E)
        qt_ref[:, hcols[j]] = qt.astype(BF16)
        qaug_ref[0:NSA_HD, hcols[j]] = _rope_t(qt, cos2t, sin2t).astype(BF16)

    def compressed_and_select(n_chunks):
        rows = 4 * SEL_CHUNK * n_chunks
        n_blk = SEL_CHUNK * n_chunks
        s_c = _dot(kc_ref[0:rows, :], qt_ref[...])
        blk_i = lax.broadcasted_iota(jnp.int32, (SEL_CHUNK, qb), 0)
        pieces = [(c, r) for c in range(n_chunks) for r in range(4)]

        def piece_rows(c, r):
            return slice((4 * c + r) * SEL_CHUNK, (4 * c + r + 1) * SEL_CHUNK)

        cmp_bias = {(c, r): jnp.where(SLC_LEN * (blk_i + SEL_CHUNK * c) + (CMP_STRIDE * r + CMP_LEN - 1) <= t_row,
                                      0.0, NEG) for c, r in pieces}
        has_valid = (t_row >= CMP_LEN - 1).astype(F32)
        imp = {cr: jnp.zeros((SEL_CHUNK, qb), F32) for cr in pieces}
        for j in range(NSA_GROUP):
            sb = {cr: s_c[piece_rows(*cr), hcols[j]] + cmp_bias[cr] for cr in pieces}
            m = _col_max(list(sb.values()))
            e = {cr: jnp.exp2(sb[cr] - m) for cr in pieces}
            den = functools.reduce(jnp.add, [jnp.sum(x, axis=0, keepdims=True) for x in e.values()])
            rinv = has_valid / jnp.maximum(den, TINY)
            for cr in pieces:
                p = e[cr] * rinv
                imp[cr] = imp[cr] + p
                p_ref[piece_rows(*cr), hcols[j]] = p.astype(BF16)
        ocmp_ref[...] = _dot(vct_ref[:, 0:rows], p_ref[0:rows, :])

        imp_r = [jnp.concatenate([imp[(c, r)] for c in range(n_chunks)], axis=0) for r in range(4)]
        blk = lax.broadcasted_iota(jnp.int32, (n_blk, qb), 0)
        imp3_prev = jnp.where(blk == 0, 0.0, pltpu.roll(imp_r[3], 1, axis=0))
        p_slc = ((((((imp_r[0] + imp3_prev) + imp_r[1]) + imp_r[0]) + imp_r[2]) + imp_r[1]) + imp_r[3]) + imp_r[2]
        cur = jnp.right_shift(t_row, 6)
        forced = (blk == 0) | (blk == cur) | (blk == cur - 1)
        score = jnp.where(blk > cur, NEG, jnp.where(forced, BIG, p_slc))
        blk_f = blk.astype(F32)
        sel = jnp.zeros((n_blk, qb), F32)
        for _ in range(SLC_TOPK):
            top = jnp.max(score, axis=0, keepdims=True)
            first = jnp.min(jnp.where(score == top, blk_f, float(n_blk)), axis=0, keepdims=True)
            hit = blk_f == first
            sel = jnp.where(hit, 1.0, sel)
            score = jnp.where(hit, REMOVED, score)
        sel_ref[0:n_blk, :] = sel
        if n_blk < n_sel:
            sel_ref[n_blk:n_sel, :] = jnp.zeros((n_sel - n_blk, qb), F32)

    chunks_needed = (t0 + qb - 1) // (SLC_LEN * SEL_CHUNK) + 1
    for n_chunks in range(1, n_sel // SEL_CHUNK + 1):
        pl.when(chunks_needed == n_chunks)(functools.partial(compressed_and_select, n_chunks))

    acc_ref[...] = jnp.zeros_like(acc_ref)
    m_ref[...] = jnp.full_like(m_ref, REMOVED)
    tiles_per_window = LANES * SLC_LEN // SLC_TILE
    kpos0 = lax.broadcasted_iota(jnp.int32, (SLC_TILE, qb), 0)

    def scores(kt, slot):
        first = pl.multiple_of((kt // tiles_per_window) * LANES, LANES)
        unselected = ((sel_ref[pl.ds(first, LANES), :] - 1.0) * BIG).astype(BF16)
        for j in range(NSA_GROUP):
            qaug_ref[NSA_HD:NSA_HD + LANES, hcols[j]] = unselected
        start = pl.multiple_of(kt * SLC_TILE, SLC_TILE)
        s_ref[slot] = _dot(kslc_ref[pl.ds(start, SLC_TILE), :], qaug_ref[...])

    def weights(kt, slot, causal):
        if causal:
            future = jnp.where(kpos0 + kt * SLC_TILE <= t_row, 0.0, NEG)
        alpha = []
        for j in range(NSA_GROUP):
            s = s_ref[slot, :, hcols[j]]
            if causal:
                s = s + future
            m_old = m_ref[0:1, hcols[j]]
            m_j = jnp.maximum(m_old, jnp.max(s, axis=0, keepdims=True))
            p_ref[slot * SLC_TILE:(slot + 1) * SLC_TILE, hcols[j]] = jnp.exp2(s - m_j).astype(BF16)
            m_ref[0:1, hcols[j]] = m_j
            alpha.append(jnp.exp2(m_old - m_j))
        return jnp.concatenate(alpha, axis=1)

    def weighted_values(kt, slot):
        start = pl.multiple_of(kt * SLC_TILE, SLC_TILE)
        return _dot(vslct_ref[:, pl.ds(start, SLC_TILE)], p_ref[slot * SLC_TILE:(slot + 1) * SLC_TILE, :])

    def absorb(kt, slot, causal):
        alpha = weights(kt, slot, causal)
        acc_ref[...] = acc_ref[...] * alpha + weighted_values(kt, slot)

    last = (t0 + qb - 1) // SLC_TILE
    scores(0, 0)

    span = qb + WIN
    start = pl.multiple_of(jnp.maximum(t0 - WIN, 0), qb)
    s_w = _dot(kwin_ref[pl.ds(start, span), :], qaug_ref[0:NSA_HD, :])
    spos0 = lax.broadcasted_iota(jnp.int32, (qb, qb), 0)
    win_bias = []
    for c in range(span // qb):
        spos = spos0 + (start + c * qb)
        win_bias.append(jnp.where((spos <= t_row) & (spos > t_row - WIN), 0.0, NEG))
    for j in range(NSA_GROUP):
        sb = [s_w[c * qb:(c + 1) * qb, hcols[j]] + win_bias[c] for c in range(span // qb)]
        m = _col_max(sb)
        for c in range(span // qb):
            p_ref[c * qb:(c + 1) * qb, hcols[j]] = jnp.exp2(sb[c] - m).astype(BF16)
    pw = _dot(vwint_ref[:, pl.ds(start, span)], p_ref[0:span, :])
    owin_ref[...] = pw[:NSA_HD] / pw[NSA_HD:NSA_HD + 1]

    def pair(i, carry):
        scores(2 * i + 1, 1)
        absorb(2 * i, 0, False)
        scores(jnp.minimum(2 * i + 2, last), 0)
        absorb(2 * i + 1, 1, False)
        return carry

    def quad(i, carry):
        return pair(2 * i + 1, pair(2 * i, carry))

    lax.fori_loop(0, last // 4, quad, 0)
    lax.fori_loop(2 * (last // 4), last // 2, pair, 0)

    @pl.when(last % 2 == 0)
    def _():
        absorb(last, 0, True)

    @pl.when(last % 2 == 1)
    def _():
        scores(last, 1)
        absorb(last - 1, 0, False)
        absorb(last, 1, True)

    acc = acc_ref[...]
    o_slc = acc[:NSA_HD] / acc[NSA_HD:NSA_HD + 1]

    gates_t = jax.nn.sigmoid(gl_ref[...] + gb_ref[...]).T
    for j in range(NSA_GROUP):
        o_t = (gates_t[3 * j:3 * j + 1] * ocmp_ref[:, hcols[j]]
               + gates_t[3 * j + 1:3 * j + 2] * o_slc[:, hcols[j]]
               + gates_t[3 * j + 2:3 * j + 3] * owin_ref[:, hcols[j]])
        o_ref[:, j * NSA_HD:(j + 1) * NSA_HD] = o_t.T


def nsa_attention(proj, cos2t, sin2t, gate_logits, gate_bias, kc, vct, kslc, vslct, kwin, vwint):
    s = proj.shape[0]
    n_cmp = kc.shape[1]
    rows_aug = NSA_HD + ONES_ROWS
    width = NSA_GROUP * NSA_QB
    qspec = pl.BlockSpec((NSA_QB, GROUP_W), lambda g, i: (i, g))
    tspec = pl.BlockSpec((NSA_HD, NSA_QB), lambda g, i: (0, i))

    def resident(shape):
        return pl.BlockSpec((None,) + shape, lambda g, i: (g, 0, 0), pipeline_mode=pl.Buffered(1))

    return pl.pallas_call(
        _nsa_kernel,
        grid=(NSA_KV, s // NSA_QB),
        in_specs=[
            qspec, tspec, tspec,
            pl.BlockSpec((NSA_QB, LANES), lambda g, i: (i, g)),
            pl.BlockSpec((1, LANES), lambda g, i: (0, g)),
            resident((n_cmp, NSA_HD)), resident((NSA_HD, n_cmp)),
            resident((s, NSA_HD + LANES)), resident((rows_aug, s)),
            resident((s, NSA_HD)), resident((rows_aug, s)),
        ],
        out_specs=qspec,
        out_shape=jax.ShapeDtypeStruct((s, NSA_KV * GROUP_W), F32),
        scratch_shapes=[
            pltpu.VMEM((NSA_HD, width), BF16),
            pltpu.VMEM((NSA_HD + LANES, width), BF16),
            pltpu.VMEM((n_cmp // 4, NSA_QB), F32),
            pltpu.VMEM((max(n_cmp, NSA_QB + WIN, 2 * SLC_TILE), width), BF16),
            pltpu.VMEM((rows_aug, width), F32),
            pltpu.VMEM((2, SLC_TILE, width), F32),
            pltpu.VMEM((8, width), F32),
            pltpu.VMEM((NSA_HD, width), F32),
            pltpu.VMEM((NSA_HD, width), F32),
        ],
        compiler_params=_params("parallel", "arbitrary"),
        name="nsa_attention",
    )(proj, cos2t, sin2t, gate_logits, gate_bias, kc, vct, kslc, vslct, kwin, vwint)


def _row_tile(s, want):
    return want if s % want == 0 else s


def _rope_tables(seq):
    inv = ROPE_THETA ** (-jnp.arange(0, NSA_HD, 2, dtype=F32) / NSA_HD)
    ang = jnp.arange(seq, dtype=F32)[:, None] * inv[None, :]
    cos, sin = jnp.cos(ang), jnp.sin(ang)
    return jnp.concatenate([cos, cos], axis=1), jnp.concatenate([-sin, sin], axis=1)


def _pad_cols(a, width):
    return jnp.pad(a, ((0, 0), (0, width - a.shape[1])))


def _mem_kv(mem, gain, w_kv):
    kv = rms_matmul(mem, gain, w_kv.astype(BF16), tm=mem.shape[0], tn=MEM_W, out_dtype=BF16)
    return kv[:, :MEM_W].T, kv[:, MEM_W:]


def _mlstm_layer_heads(x, gain, w_in, gate_bias, head_norm):
    s = x.shape[0]
    qk_w = ML_HEADS * ML_DQK
    w_q, w_k, w_v, w_o, w_g, w_mq = jnp.split(
        w_in, [qk_w, 2 * qk_w, 2 * qk_w + MAIN_W, 2 * qk_w + 2 * MAIN_W, 2 * qk_w + 2 * MAIN_W + 2 * ML_HEADS], axis=1)

    def head_slots(w):
        w = w.reshape(w.shape[0], ML_HEADS, ML_DQK)
        return jnp.pad(w, ((0, 0), (0, 0), (0, ML_DQK_PAD - ML_DQK))).reshape(w.shape[0], ML_HEADS * ML_DQK_PAD)

    w_main = jnp.concatenate([w_v, w_o, head_slots(w_q), head_slots(w_k), w_mq], axis=1).astype(BF16)
    qk_pad_w = ML_HEADS * ML_DQK_PAD
    q_block = 2 * MAIN_W // qk_pad_w
    mq_block = (2 * MAIN_W + 2 * qk_pad_w) // MEM_W
    assert 2 * MAIN_W % qk_pad_w == 0 and (2 * MAIN_W + 2 * qk_pad_w) % MEM_W == 0
    tm = _row_tile(s, 1024)
    proj, gates = rms_matmul(x, gain, w_main, _pad_cols(w_g, LANES).astype(BF16), tm=tm, tn=512)
    gates = gates[:, :2 * ML_HEADS]
    h = mlstm_heads(proj, q_block, q_block + 1, 0, 1, gates.T, gates, gate_bias, head_norm)
    return h, proj, mq_block


def _with_ones_rows(v):
    g, s, _ = v.shape
    return jnp.concatenate([v.transpose(0, 2, 1), jnp.ones((g, ONES_ROWS, s), v.dtype)], axis=1)


def _shared_kv(x, kv_norm, w_kv, cmp_pos, cmp_w1, cmp_w2, cos2, sin2):
    s = x.shape[0]
    n_seg = s // CMP_STRIDE
    n_sel = s // SLC_LEN
    kv = rms_matmul(x, kv_norm, w_kv.astype(BF16), tm=_row_tile(s, 1024), tn=512)
    kslc, vslc, kwin, vwin = kv_prep(kv, cos2, sin2, tm=_row_tile(s, 512))
    cmp = compress(kv, cmp_pos, cmp_w1.astype(BF16), cmp_w2.astype(BF16))
    cmp = cmp.reshape(2 * NSA_KV, n_sel // SEL_CHUNK, SEL_CHUNK, 4, NSA_HD).transpose(0, 1, 3, 2, 4)
    cmp = cmp.reshape(2 * NSA_KV, n_seg, NSA_HD)
    cmp = cmp.astype(BF16)
    block_in_window = (jnp.arange(s, dtype=jnp.int32) // SLC_LEN) % LANES
    onehot = (block_in_window[:, None] == jnp.arange(LANES, dtype=jnp.int32)[None, :]).astype(BF16)
    kslc_aug = jnp.concatenate([kslc, jnp.broadcast_to(onehot[None], (NSA_KV, s, LANES))], axis=2)
    return dict(kc=cmp[:NSA_KV], vct=cmp[NSA_KV:].transpose(0, 2, 1), kslc=kslc_aug, vslct=_with_ones_rows(vslc),
                kwin=kwin, vwint=_with_ones_rows(vwin))


def _nsa_layer_heads(x, gain, w_in, gate_bias, shared, cos2t, sin2t):
    s = x.shape[0]
    n_gate = 3 * NSA_GROUP
    w_q, w_g, w_mq = jnp.split(w_in, [MAIN_W, MAIN_W + 3 * NSA_HEADS], axis=1)
    w_main = jnp.concatenate([w_q, w_mq], axis=1).astype(BF16)
    w_gate = jnp.concatenate([_pad_cols(w_g[:, g * n_gate:(g + 1) * n_gate], LANES) for g in range(NSA_KV)], axis=1)
    b_gate = jnp.concatenate(
        [_pad_cols(gate_bias[None, g * n_gate:(g + 1) * n_gate], LANES) for g in range(NSA_KV)], axis=1)
    tm = _row_tile(s, 1024)
    proj, gate_logits = rms_matmul(x, gain, w_main, w_gate.astype(BF16), tm=tm, tn=512)
    o = nsa_attention(proj, cos2t, sin2t, gate_logits, b_gate, shared["kc"], shared["vct"], shared["kslc"],
                      shared["vslct"], shared["kwin"], shared["vwint"])
    return o, proj, MAIN_W // MEM_W


def kernel(x, mem, norm_gains, ffn_w_gate, ffn_w_up, ffn_w_down, mem_norm, mem_w_kv, w_out, a_w_in, a_gate_bias,
           a_head_norm, kv_norm, w_kv, cmp_pos, cmp_w1, cmp_w2, b_w_in, b_gate_bias):
    batch, seq, d_model = x.shape
    depth = norm_gains.shape[0]
    n_a = a_w_in.shape[0]
    assert seq % (SLC_LEN * LANES) == 0 and seq % ML_CHUNK == 0, "sequence must tile the selection-block lanes"
    cos2, sin2 = _rope_tables(seq)
    cos2t, sin2t = cos2.T, sin2.T
    tm_ffn = _row_tile(seq, 512)

    w_gate16, w_up16, w_down16 = ffn_w_gate.astype(BF16), ffn_w_up.astype(BF16), ffn_w_down.astype(BF16)
    w_out16 = w_out.astype(BF16)

    def ffn(xb, layer, half, g_pre, g_post):
        return half_ffn(xb, g_pre, g_post, w_gate16, w_up16, w_down16, layer, half, tm=tm_ffn, tf=512)

    outs = []
    for b in range(batch):
        xb = x[b]
        shared = None
        for layer in range(depth):
            if layer == n_a:
                shared = _shared_kv(xb, kv_norm, w_kv, cmp_pos, cmp_w1, cmp_w2, cos2, sin2)
            g = norm_gains[layer]
            xb = ffn(xb, layer, 0, g[0], g[1])
            mem_kt, mem_v = _mem_kv(mem[b], mem_norm[layer], mem_w_kv[layer])
            if layer < n_a:
                main, proj, mq_block = _mlstm_layer_heads(xb, g[2], a_w_in[layer], a_gate_bias[layer],
                                                          a_head_norm[layer])
            else:
                lb = layer - n_a
                main, proj, mq_block = _nsa_layer_heads(xb, g[2], b_w_in[lb], b_gate_bias[lb], shared, cos2t, sin2t)
            xb = out_proj(main, proj, mq_block, mem_kt, mem_v, xb, w_out16, layer, g[3], tm=_row_tile(seq, 256))
            xb = ffn(xb, layer, 1, g[4], g[5])
        outs.append(xb)
    return jnp.stack(outs, axis=0)
```

```python
import functools

import jax
import jax.numpy as jnp
from jax import lax
from jax.experimental import pallas as pl
from jax.experimental.pallas import tpu as pltpu

F32 = jnp.float32
BF16 = jnp.bfloat16

MEM_HEADS = 4
MEM_HD = 128
MEM_W = MEM_HEADS * MEM_HD
ML_HEADS = 4
ML_DV = 384
ML_DQK = 192
MAIN_W = ML_HEADS * ML_DV
NSA_HEADS = 12
NSA_HD = 128
NSA_KV = 2
NSA_GROUP = NSA_HEADS // NSA_KV
GROUP_W = NSA_GROUP * NSA_HD
CMP_LEN = 32
CMP_STRIDE = 16
SLC_LEN = 64
SLC_TOPK = 16
WIN = 512
FFN_RES = 0.5
ROPE_THETA = 10000.0
EPS = 1e-6
NEG = -1e30
BIG = 1e30
REMOVED = -3e38
LOG2E = 1.4426950408889634
TINY = 1e-30

LANES = 128
VMEM_LIMIT_BYTES = 56 * 1024 * 1024

ML_CHUNK = 256
ML_DQK_PAD = 256
NSA_QB = 128
SLC_TILE = 1024
ONES_ROWS = 16
SEL_CHUNK = 64


def _params(*semantics):
    return pltpu.CompilerParams(dimension_semantics=semantics, vmem_limit_bytes=VMEM_LIMIT_BYTES)


def _rms(x, gain):
    return x * lax.rsqrt(jnp.mean(x * x, axis=-1, keepdims=True) + EPS) * gain


def _dot(a, b):
    return jnp.dot(a, b, preferred_element_type=F32)


def _exact_dot(a, b01):
    a1 = a.astype(BF16)
    r1 = a - a1.astype(F32)
    a2 = r1.astype(BF16)
    a3 = (r1 - a2.astype(F32)).astype(BF16)
    return _dot(a1, b01) + _dot(a2, b01) + _dot(a3, b01)


def _exact_dot_left(b01, a):
    a1 = a.astype(BF16)
    r1 = a - a1.astype(F32)
    a2 = r1.astype(BF16)
    a3 = (r1 - a2.astype(F32)).astype(BF16)
    return _dot(b01, a1) + _dot(b01, a2) + _dot(b01, a3)


def _rms_matmul_kernel(x_ref, g_ref, w_ref, o_ref, xn_ref):
    @pl.when(pl.program_id(1) == 0)
    def _():
        xn_ref[...] = _rms(x_ref[...], g_ref[...]).astype(BF16)

    o_ref[...] = _dot(xn_ref[...], w_ref[...]).astype(o_ref.dtype)


def _rms_matmul_side_kernel(x_ref, g_ref, w_ref, wside_ref, o_ref, oside_ref, xn_ref):
    @pl.when(pl.program_id(1) == 0)
    def _():
        xn_ref[...] = _rms(x_ref[...], g_ref[...]).astype(BF16)
        oside_ref[...] = _dot(xn_ref[...], wside_ref[...])

    o_ref[...] = _dot(xn_ref[...], w_ref[...]).astype(o_ref.dtype)


def rms_matmul(x, gain, w, w_side=None, *, tm, tn, out_dtype=F32):
    s, k = x.shape
    n = w.shape[1]
    in_specs = [
        pl.BlockSpec((tm, k), lambda i, j: (i, 0)),
        pl.BlockSpec((1, k), lambda i, j: (0, 0)),
        pl.BlockSpec((k, tn), lambda i, j: (0, j)),
    ]
    out_specs = pl.BlockSpec((tm, tn), lambda i, j: (i, j))
    out_shape = jax.ShapeDtypeStruct((s, n), out_dtype)
    args = (x, gain.reshape(1, k), w)
    body = _rms_matmul_kernel
    if w_side is not None:
        n_side = w_side.shape[1]
        in_specs.append(pl.BlockSpec((k, n_side), lambda i, j: (0, 0)))
        out_specs = [out_specs, pl.BlockSpec((tm, n_side), lambda i, j: (i, 0))]
        out_shape = [out_shape, jax.ShapeDtypeStruct((s, n_side), F32)]
        args = args + (w_side,)
        body = _rms_matmul_side_kernel
    return pl.pallas_call(
        body,
        grid=(s // tm, n // tn),
        in_specs=in_specs,
        out_specs=out_specs,
        out_shape=out_shape,
        scratch_shapes=[pltpu.VMEM((tm, k), BF16)],
        compiler_params=_params("parallel", "arbitrary"),
        name="rms_matmul",
    )(*args)


def _ffn_kernel(x_ref, gpre_ref, gpost_ref, wg_ref, wu_ref, wd_ref, o_ref, xn_ref):
    j = pl.program_id(1)

    @pl.when(j == 0)
    def _():
        xn_ref[...] = _rms(x_ref[...], gpre_ref[...]).astype(BF16)
        o_ref[...] = jnp.zeros_like(o_ref)

    xn = xn_ref[...]
    gate = _dot(xn, wg_ref[...])
    up = _dot(xn, wu_ref[...])
    hidden = (gate * jax.nn.sigmoid(gate) * up).astype(BF16)
    o_ref[...] += _dot(hidden, wd_ref[...])

    @pl.when(j == pl.num_programs(1) - 1)
    def _():
        o_ref[...] = x_ref[...] + FFN_RES * _rms(o_ref[...], gpost_ref[...])


def half_ffn(x, g_pre, g_post, w_gate, w_up, w_down, layer, half, *, tm, tf):
    s, d = x.shape
    d_ff = w_gate.shape[3]
    return pl.pallas_call(
        _ffn_kernel,
        grid=(s // tm, d_ff // tf),
        in_specs=[
            pl.BlockSpec((tm, d), lambda i, j: (i, 0)),
            pl.BlockSpec((1, d), lambda i, j: (0, 0)),
            pl.BlockSpec((1, d), lambda i, j: (0, 0)),
            pl.BlockSpec((None, None, d, tf), lambda i, j: (layer, half, 0, j)),
            pl.BlockSpec((None, None, d, tf), lambda i, j: (layer, half, 0, j)),
            pl.BlockSpec((None, None, tf, d), lambda i, j: (layer, half, j, 0)),
        ],
        out_specs=pl.BlockSpec((tm, d), lambda i, j: (i, 0)),
        out_shape=jax.ShapeDtypeStruct((s, d), F32),
        scratch_shapes=[pltpu.VMEM((tm, d), BF16)],
        compiler_params=_params("parallel", "arbitrary"),
        name="half_ffn",
    )(x, g_pre.reshape(1, d), g_post.reshape(1, d), w_gate, w_up, w_down)


def _out_proj_kernel(h_ref, mq_ref, kt_ref, v_ref, x_ref, wh_ref, wm_ref, g_ref, o_ref):
    scale = MEM_HD ** -0.5
    heads = []
    for h in range(MEM_HEADS):
        cols = slice(h * MEM_HD, (h + 1) * MEM_HD)
        s = _dot(mq_ref[:, cols], kt_ref[cols, :]) * scale
        e = jnp.exp(s - jnp.max(s, axis=-1, keepdims=True))
        p = e / jnp.sum(e, axis=-1, keepdims=True)
        heads.append(_dot(p.astype(BF16), v_ref[:, cols]).astype(BF16))
    y = _dot(h_ref[...].astype(BF16), wh_ref[...]) + _dot(jnp.concatenate(heads, axis=1), wm_ref[...])
    o_ref[...] = x_ref[...] + _rms(y, g_ref[...])


def out_proj(main, proj, mq_block, mem_kt, mem_v, x, w, layer, gain, *, tm):
    s, d = x.shape
    km = main.shape[1]
    m = mem_v.shape[0]
    assert km % MEM_W == 0
    return pl.pallas_call(
        _out_proj_kernel,
        grid=(s // tm,),
        in_specs=[
            pl.BlockSpec((tm, km), lambda i: (i, 0)),
            pl.BlockSpec((tm, MEM_W), lambda i: (i, mq_block)),
            pl.BlockSpec((MEM_W, m), lambda i: (0, 0)),
            pl.BlockSpec((m, MEM_W), lambda i: (0, 0)),
            pl.BlockSpec((tm, d), lambda i: (i, 0)),
            pl.BlockSpec((None, km, d), lambda i: (layer, 0, 0)),
            pl.BlockSpec((None, MEM_W, d), lambda i: (layer, km // MEM_W, 0)),
            pl.BlockSpec((1, d), lambda i: (0, 0)),
        ],
        out_specs=pl.BlockSpec((tm, d), lambda i: (i, 0)),
        out_shape=jax.ShapeDtypeStruct((s, d), F32),
        compiler_params=_params("parallel"),
        name="out_proj",
    )(main, proj, mem_kt, mem_v, x, w, w, gain.reshape(1, d))


def _log_sigmoid(x):
    return jnp.minimum(x, 0.0) - jnp.log1p(jnp.exp(-jnp.abs(x)))


def _mlstm_kernel(q_ref, k_ref, v_ref, o_ref, grow_ref, gcol_ref, brow_ref, bcol_ref, hn_ref,
                  out_ref, c_ref, m_ref):
    chunk = grow_ref.shape[1]

    @pl.when(pl.program_id(0) == 0)
    def _():
        c_ref[...] = jnp.zeros_like(c_ref)
        m_ref[...] = jnp.zeros_like(m_ref)

    grow = grow_ref[...] + brow_ref[...]
    gcol = gcol_ref[...] + bcol_ref[...]
    r_idx = lax.broadcasted_iota(jnp.int32, (chunk, chunk), 0)
    c_idx = lax.broadcasted_iota(jnp.int32, (chunk, chunk), 1)
    causal = c_idx <= r_idx
    b_rows = _exact_dot(_log_sigmoid(grow), (r_idx <= c_idx).astype(BF16))
    b_cols = _exact_dot_left(causal.astype(BF16), _log_sigmoid(gcol))
    ones_col = (lax.broadcasted_iota(jnp.int32, (chunk, LANES), 1) == 0).astype(BF16)

    for h in range(ML_HEADS):
        b_r = b_rows[ML_HEADS + h:ML_HEADS + h + 1, :]
        i_r = grow[h:h + 1, :]
        b_c = b_cols[:, ML_HEADS + h:ML_HEADS + h + 1]
        m_prev = m_ref[h:h + 1, 0:1]
        dmat = jnp.where(causal, b_c - b_r + i_r, -jnp.inf)
        m_inter = b_c + m_prev
        m_vec = jnp.maximum(jnp.max(dmat, axis=-1, keepdims=True), m_inter)
        qk_cols = slice(h * ML_DQK_PAD, (h + 1) * ML_DQK_PAD)
        q = q_ref[:, qk_cols]
        kt = (k_ref[:, qk_cols].astype(F32) * (ML_DQK ** -0.5)).T
        smat = _dot(q, kt.astype(BF16)) * jnp.exp(dmat - m_vec)
        inter = jnp.exp(m_inter - m_vec)
        vcols = slice(h * ML_DV, (h + 1) * ML_DV)
        v_aug = jnp.concatenate([v_ref[:, vcols], ones_col], axis=1)
        c_prev = c_ref[h]
        num_aug = _dot(smat.astype(BF16), v_aug) + inter * _dot(q, c_prev.astype(BF16))
        den = num_aug[:, ML_DV:ML_DV + 1]
        hval = num_aug[:, :ML_DV] / jnp.maximum(jnp.abs(den), jnp.exp(-m_vec))
        m_last = m_vec[chunk - 1:chunk, :]
        b_last = b_c[chunk - 1:chunk, :]
        w_r = jnp.exp(b_last - b_r + i_r - m_last)
        decay = jnp.exp(b_last + m_prev - m_last)
        c_ref[h] = decay * c_prev + _dot((kt * w_r).astype(BF16), v_aug)
        m_ref[h:h + 1, :] = jnp.broadcast_to(m_last, (1, LANES))
        out_ref[:, vcols] = _rms(hval, hn_ref[:, vcols]) * jax.nn.sigmoid(o_ref[:, vcols].astype(F32))


def mlstm_heads(proj, q_block, k_block, v_block, o_block, gates_row, gates_col, bias, head_norm):
    s = proj.shape[0]
    chunk = ML_CHUNK
    h2 = 2 * ML_HEADS
    qk_w = ML_HEADS * ML_DQK_PAD
    return pl.pallas_call(
        _mlstm_kernel,
        grid=(s // chunk,),
        in_specs=[
            pl.BlockSpec((chunk, qk_w), lambda c: (c, q_block)),
            pl.BlockSpec((chunk, qk_w), lambda c: (c, k_block)),
            pl.BlockSpec((chunk, MAIN_W), lambda c: (c, v_block)),
            pl.BlockSpec((chunk, MAIN_W), lambda c: (c, o_block)),
            pl.BlockSpec((h2, chunk), lambda c: (0, c)),
            pl.BlockSpec((chunk, h2), lambda c: (c, 0)),
            pl.BlockSpec((h2, 1), lambda c: (0, 0)),
            pl.BlockSpec((1, h2), lambda c: (0, 0)),
            pl.BlockSpec((1, MAIN_W), lambda c: (0, 0)),
        ],
        out_specs=pl.BlockSpec((chunk, MAIN_W), lambda c: (c, 0)),
        out_shape=jax.ShapeDtypeStruct((s, MAIN_W), F32),
        scratch_shapes=[pltpu.VMEM((ML_HEADS, ML_DQK_PAD, ML_DV + LANES), F32), pltpu.VMEM((8, LANES), F32)],
        compiler_params=_params("arbitrary"),
        name="mlstm",
    )(proj, proj, proj, proj, gates_row, gates_col, bias.reshape(h2, 1), bias.reshape(1, h2),
      head_norm.reshape(1, MAIN_W))


def _rope(x, cos2, sin2):
    return x * cos2 + pltpu.roll(x, NSA_HD // 2, axis=1) * sin2


def _kv_prep_kernel(kv_ref, cos_ref, sin_ref, kslc_ref, vslc_ref, kwin_ref, vwin_ref):
    cos2 = cos_ref[...]
    sin2 = sin_ref[...]
    for g in range(NSA_KV):
        def col(c):
            start = (c * NSA_KV + g) * NSA_HD
            return kv_ref[:, start:start + NSA_HD]
        kslc_ref[g] = _rope(col(2), cos2, sin2).astype(BF16)
        vslc_ref[g] = col(3).astype(BF16)
        kwin_ref[g] = _rope(col(4), cos2, sin2).astype(BF16)
        vwin_ref[g] = col(5).astype(BF16)


def kv_prep(kv, cos2, sin2, *, tm):
    s, w = kv.shape
    out = jax.ShapeDtypeStruct((NSA_KV, s, NSA_HD), BF16)
    ospec = pl.BlockSpec((NSA_KV, tm, NSA_HD), lambda i: (0, i, 0))
    return pl.pallas_call(
        _kv_prep_kernel,
        grid=(s // tm,),
        in_specs=[
            pl.BlockSpec((tm, w), lambda i: (i, 0)),
            pl.BlockSpec((tm, NSA_HD), lambda i: (i, 0)),
            pl.BlockSpec((tm, NSA_HD), lambda i: (i, 0)),
        ],
        out_specs=[ospec] * 4,
        out_shape=[out] * 4,
        compiler_params=_params("parallel"),
        name="kv_prep",
    )(kv, cos2, sin2)


def _compress_kernel(seg_ref, pos_ref, w1_ref, w2_ref, o_ref):
    n_seg = seg_ref.shape[0]
    first, second = None, None
    for tok in range(CMP_STRIDE):
        x = seg_ref[:, tok, :]
        for half in range(2):
            row = half * CMP_STRIDE + tok
            part = _dot((x + pos_ref[row:row + 1, :]).astype(BF16), w1_ref[row * NSA_HD:(row + 1) * NSA_HD, :])
            if half == 0:
                first = part if first is None else first + part
            else:
                second = part if second is None else second + part
    hidden = first + pltpu.roll(second, n_seg - 1, axis=0)
    act = jax.nn.gelu(hidden, approximate=True)
    o_ref[...] = _dot(act.astype(BF16), w2_ref[...])


def compress(kv, pos, w1, w2):
    s, w = kv.shape
    n_seg = s // CMP_STRIDE
    hid = w1.shape[2]
    wmap = lambda cg: (cg // NSA_KV, 0, 0)
    return pl.pallas_call(
        _compress_kernel,
        grid=(2 * NSA_KV,),
        in_specs=[
            pl.BlockSpec((n_seg, CMP_STRIDE, NSA_HD), lambda cg: (0, 0, cg)),
            pl.BlockSpec((None, CMP_LEN, NSA_HD), wmap),
            pl.BlockSpec((None, CMP_LEN * NSA_HD, hid), wmap),
            pl.BlockSpec((None, hid, NSA_HD), wmap),
        ],
        out_specs=pl.BlockSpec((None, n_seg, NSA_HD), lambda cg: (cg, 0, 0)),
        out_shape=jax.ShapeDtypeStruct((2 * NSA_KV, n_seg, NSA_HD), F32),
        compiler_params=_params("parallel"),
        name="compress",
    )(kv.reshape(n_seg, CMP_STRIDE, w), pos, w1, w2)


def _rope_t(x, cos2t, sin2t):
    half = NSA_HD // 2
    return x * cos2t + jnp.concatenate([x[half:], x[:half]], axis=0) * sin2t


def _col_max(chunks):
    return jnp.max(functools.reduce(jnp.maximum, chunks), axis=0, keepdims=True)


def _nsa_kernel(q_ref, cos_ref, sin_ref, gl_ref, gb_ref, kc_ref, vct_ref, kslc_ref, vslct_ref, kwin_ref, vwint_ref,
                o_ref, qt_ref, qaug_ref, sel_ref, p_ref, acc_ref, s_ref, m_ref, ocmp_ref, owin_ref):
    qi = pl.program_id(1)
    qb = NSA_QB
    n_cmp = kc_ref.shape[0]
    n_sel = n_cmp // 4
    t0 = qi * qb
    t_row = t0 + lax.broadcasted_iota(jnp.int32, (1, qb), 1)
    hcols = [slice(j * qb, (j + 1) * qb) for j in range(NSA_GROUP)]

    cos2t = cos_ref[...]
    sin2t = sin_ref[...]
    for j in range(NSA_GROUP):
        qt = q_ref[:, j * NSA_HD:(j + 1) * NSA_HD].astype(F32).T * (NSA_HD ** -0.5 * LOG2E)
        qt_ref[:, hcols[j]] = qt.astype(BF16)
        qaug_ref[0:NSA_HD, hcols[j]] = _rope_t(qt, cos2t, sin2t).astype(BF16)

    def compressed_and_select(n_chunks):
        rows = 4 * SEL_CHUNK * n_chunks
        n_blk = SEL_CHUNK * n_chunks
        s_c = _dot(kc_ref[0:rows, :], qt_ref[...])
        blk_i = lax.broadcasted_iota(jnp.int32, (SEL_CHUNK, qb), 0)
        pieces = [(c, r) for c in range(n_chunks) for r in range(4)]

        def piece_rows(c, r):
            return slice((4 * c + r) * SEL_CHUNK, (4 * c + r + 1) * SEL_CHUNK)

        cmp_bias = {(c, r): jnp.where(SLC_LEN * (blk_i + SEL_CHUNK * c) + (CMP_STRIDE * r + CMP_LEN - 1) <= t_row,
                                      0.0, NEG) for c, r in pieces}
        has_valid = (t_row >= CMP_LEN - 1).astype(F32)
        imp = {cr: jnp.zeros((SEL_CHUNK, qb), F32) for cr in pieces}
        for j in range(NSA_GROUP):
            sb = {cr: s_c[piece_rows(*cr), hcols[j]] + cmp_bias[cr] for cr in pieces}
            m = _col_max(list(sb.values()))
            e = {cr: jnp.exp2(sb[cr] - m) for cr in pieces}
            den = functools.reduce(jnp.add, [jnp.sum(x, axis=0, keepdims=True) for x in e.values()])
            rinv = has_valid / jnp.maximum(den, TINY)
            for cr in pieces:
                p = e[cr] * rinv
                imp[cr] = imp[cr] + p
                p_ref[piece_rows(*cr), hcols[j]] = p.astype(BF16)
        ocmp_ref[...] = _dot(vct_ref[:, 0:rows], p_ref[0:rows, :])

        imp_r = [jnp.concatenate([imp[(c, r)] for c in range(n_chunks)], axis=0) for r in range(4)]
        blk = lax.broadcasted_iota(jnp.int32, (n_blk, qb), 0)
        imp3_prev = jnp.where(blk == 0, 0.0, pltpu.roll(imp_r[3], 1, axis=0))
        p_slc = ((((((imp_r[0] + imp3_prev) + imp_r[1]) + imp_r[0]) + imp_r[2]) + imp_r[1]) + imp_r[3]) + imp_r[2]
        cur = jnp.right_shift(t_row, 6)
        forced = (blk == 0) | (blk == cur) | (blk == cur - 1)
        score = jnp.where(blk > cur, NEG, jnp.where(forced, BIG, p_slc))
        blk_f = blk.astype(F32)
        sel = jnp.zeros((n_blk, qb), F32)
        for _ in range(SLC_TOPK):
            top = jnp.max(score, axis=0, keepdims=True)
            first = jnp.min(jnp.where(score == top, blk_f, float(n_blk)), axis=0, keepdims=True)
            hit = blk_f == first
            sel = jnp.where(hit, 1.0, sel)
            score = jnp.where(hit, REMOVED, score)
        sel_ref[0:n_blk, :] = sel
        if n_blk < n_sel:
            sel_ref[n_blk:n_sel, :] = jnp.zeros((n_sel - n_blk, qb), F32)

    chunks_needed = (t0 + qb - 1) // (SLC_LEN * SEL_CHUNK) + 1
    for n_chunks in range(1, n_sel // SEL_CHUNK + 1):
        pl.when(chunks_needed == n_chunks)(functools.partial(compressed_and_select, n_chunks))

    acc_ref[...] = jnp.zeros_like(acc_ref)
    m_ref[...] = jnp.full_like(m_ref, REMOVED)
    tiles_per_window = LANES * SLC_LEN // SLC_TILE
    kpos0 = lax.broadcasted_iota(jnp.int32, (SLC_TILE, qb), 0)

    def scores(kt, slot):
        first = pl.multiple_of((kt // tiles_per_window) * LANES, LANES)
        unselected = ((sel_ref[pl.ds(first, LANES), :] - 1.0) * BIG).astype(BF16)
        for j in range(NSA_GROUP):
            qaug_ref[NSA_HD:NSA_HD + LANES, hcols[j]] = unselected
        start = pl.multiple_of(kt * SLC_TILE, SLC_TILE)
        s_ref[slot] = _dot(kslc_ref[pl.ds(start, SLC_TILE), :], qaug_ref[...])

    def weights(kt, slot, causal):
        if causal:
            future = jnp.where(kpos0 + kt * SLC_TILE <= t_row, 0.0, NEG)
        alpha = []
        for j in range(NSA_GROUP):
            s = s_ref[slot, :, hcols[j]]
            if causal:
                s = s + future
            m_old = m_ref[0:1, hcols[j]]
            m_j = jnp.maximum(m_old, jnp.max(s, axis=0, keepdims=True))
            p_ref[slot * SLC_TILE:(slot + 1) * SLC_TILE, hcols[j]] = jnp.exp2(s - m_j).astype(BF16)
            m_ref[0:1, hcols[j]] = m_j
            alpha.append(jnp.exp2(m_old - m_j))
        return jnp.concatenate(alpha, axis=1)

    def weighted_values(kt, slot):
        start = pl.multiple_of(kt * SLC_TILE, SLC_TILE)
        return _dot(vslct_ref[:, pl.ds(start, SLC_TILE)], p_ref[slot * SLC_TILE:(slot + 1) * SLC_TILE, :])

    def absorb(kt, slot, causal):
        alpha = weights(kt, slot, causal)
        acc_ref[...] = acc_ref[...] * alpha + weighted_values(kt, slot)

    last = (t0 + qb - 1) // SLC_TILE
    scores(0, 0)

    span = qb + WIN
    start = pl.multiple_of(jnp.maximum(t0 - WIN, 0), qb)
    s_w = _dot(kwin_ref[pl.ds(start, span), :], qaug_ref[0:NSA_HD, :])
    spos0 = lax.broadcasted_iota(jnp.int32, (qb, qb), 0)
    win_bias = []
    for c in range(span // qb):
        spos = spos0 + (start + c * qb)
        win_bias.append(jnp.where((spos <= t_row) & (spos > t_row - WIN), 0.0, NEG))
    for j in range(NSA_GROUP):
        sb = [s_w[c * qb:(c + 1) * qb, hcols[j]] + win_bias[c] for c in range(span // qb)]
        m = _col_max(sb)
        for c in range(span // qb):
            p_ref[c * qb:(c + 1) * qb, hcols[j]] = jnp.exp2(sb[c] - m).astype(BF16)
    pw = _dot(vwint_ref[:, pl.ds(start, span)], p_ref[0:span, :])
    owin_ref[...] = pw[:NSA_HD] / pw[NSA_HD:NSA_HD + 1]

    def pair(i, carry):
        scores(2 * i + 1, 1)
        absorb(2 * i, 0, False)
        scores(jnp.minimum(2 * i + 2, last), 0)
        absorb(2 * i + 1, 1, False)
        return carry

    def quad(i, carry):
        return pair(2 * i + 1, pair(2 * i, carry))

    lax.fori_loop(0, last // 4, quad, 0)
    lax.fori_loop(2 * (last // 4), last // 2, pair, 0)

    @pl.when(last % 2 == 0)
    def _():
        absorb(last, 0, True)

    @pl.when(last % 2 == 1)
    def _():
        scores(last, 1)
        absorb(last - 1, 0, False)
        absorb(last, 1, True)

    acc = acc_ref[...]
    o_slc = acc[:NSA_HD] / acc[NSA_HD:NSA_HD + 1]

    gates_t = jax.nn.sigmoid(gl_ref[...] + gb_ref[...]).T
    for j in range(NSA_GROUP):
        o_t = (gates_t[3 * j:3 * j + 1] * ocmp_ref[:, hcols[j]]
               + gates_t[3 * j + 1:3 * j + 2] * o_slc[:, hcols[j]]
               + gates_t[3 * j + 2:3 * j + 3] * owin_ref[:, hcols[j]])
        o_ref[:, j * NSA_HD:(j + 1) * NSA_HD] = o_t.T


def nsa_attention(proj, cos2t, sin2t, gate_logits, gate_bias, kc, vct, kslc, vslct, kwin, vwint):
    s = proj.shape[0]
    n_cmp = kc.shape[1]
    rows_aug = NSA_HD + ONES_ROWS
    width = NSA_GROUP * NSA_QB
    qspec = pl.BlockSpec((NSA_QB, GROUP_W), lambda g, i: (i, g))
    tspec = pl.BlockSpec((NSA_HD, NSA_QB), lambda g, i: (0, i))

    def resident(shape):
        return pl.BlockSpec((None,) + shape, lambda g, i: (g, 0, 0), pipeline_mode=pl.Buffered(1))

    return pl.pallas_call(
        _nsa_kernel,
        grid=(NSA_KV, s // NSA_QB),
        in_specs=[
            qspec, tspec, tspec,
            pl.BlockSpec((NSA_QB, LANES), lambda g, i: (i, g)),
            pl.BlockSpec((1, LANES), lambda g, i: (0, g)),
            resident((n_cmp, NSA_HD)), resident((NSA_HD, n_cmp)),
            resident((s, NSA_HD + LANES)), resident((rows_aug, s)),
            resident((s, NSA_HD)), resident((rows_aug, s)),
        ],
        out_specs=qspec,
        out_shape=jax.ShapeDtypeStruct((s, NSA_KV * GROUP_W), F32),
        scratch_shapes=[
            pltpu.VMEM((NSA_HD, width), BF16),
            pltpu.VMEM((NSA_HD + LANES, width), BF16),
            pltpu.VMEM((n_cmp // 4, NSA_QB), F32),
            pltpu.VMEM((max(n_cmp, NSA_QB + WIN, 2 * SLC_TILE), width), BF16),
            pltpu.VMEM((rows_aug, width), F32),
            pltpu.VMEM((2, SLC_TILE, width), F32),
            pltpu.VMEM((8, width), F32),
            pltpu.VMEM((NSA_HD, width), F32),
            pltpu.VMEM((NSA_HD, width), F32),
        ],
        compiler_params=_params("parallel", "arbitrary"),
        name="nsa_attention",
    )(proj, cos2t, sin2t, gate_logits, gate_bias, kc, vct, kslc, vslct, kwin, vwint)


def _row_tile(s, want):
    return want if s % want == 0 else s


def _rope_tables(seq):
    inv = ROPE_THETA ** (-jnp.arange(0, NSA_HD, 2, dtype=F32) / NSA_HD)
    ang = jnp.arange(seq, dtype=F32)[:, None] * inv[None, :]
    cos, sin = jnp.cos(ang), jnp.sin(ang)
    return jnp.concatenate([cos, cos], axis=1), jnp.concatenate([-sin, sin], axis=1)


def _pad_cols(a, width):
    return jnp.pad(a, ((0, 0), (0, width - a.shape[1])))


def _mem_kv(mem, gain, w_kv):
    kv = rms_matmul(mem, gain, w_kv.astype(BF16), tm=mem.shape[0], tn=MEM_W, out_dtype=BF16)
    return kv[:, :MEM_W].T, kv[:, MEM_W:]


def _mlstm_layer_heads(x, gain, w_in, gate_bias, head_norm):
    s = x.shape[0]
    qk_w = ML_HEADS * ML_DQK
    w_q, w_k, w_v, w_o, w_g, w_mq = jnp.split(
        w_in, [qk_w, 2 * qk_w, 2 * qk_w + MAIN_W, 2 * qk_w + 2 * MAIN_W, 2 * qk_w + 2 * MAIN_W + 2 * ML_HEADS], axis=1)

    def head_slots(w):
        w = w.reshape(w.shape[0], ML_HEADS, ML_DQK)
        return jnp.pad(w, ((0, 0), (0, 0), (0, ML_DQK_PAD - ML_DQK))).reshape(w.shape[0], ML_HEADS * ML_DQK_PAD)

    w_main = jnp.concatenate([w_v, w_o, head_slots(w_q), head_slots(w_k), w_mq], axis=1).astype(BF16)
    qk_pad_w = ML_HEADS * ML_DQK_PAD
    q_block = 2 * MAIN_W // qk_pad_w
    mq_block = (2 * MAIN_W + 2 * qk_pad_w) // MEM_W
    assert 2 * MAIN_W % qk_pad_w == 0 and (2 * MAIN_W + 2 * qk_pad_w) % MEM_W == 0
    tm = _row_tile(s, 1024)
    proj, gates = rms_matmul(x, gain, w_main, _pad_cols(w_g, LANES).astype(BF16), tm=tm, tn=512, out_dtype=BF16)
    gates = gates[:, :2 * ML_HEADS]
    h = mlstm_heads(proj, q_block, q_block + 1, 0, 1, gates.T, gates, gate_bias, head_norm)
    return h, proj, mq_block


def _with_ones_rows(v):
    g, s, _ = v.shape
    return jnp.concatenate([v.transpose(0, 2, 1), jnp.ones((g, ONES_ROWS, s), v.dtype)], axis=1)


def _shared_kv(x, kv_norm, w_kv, cmp_pos, cmp_w1, cmp_w2, cos2, sin2):
    s = x.shape[0]
    n_seg = s // CMP_STRIDE
    n_sel = s // SLC_LEN
    kv = rms_matmul(x, kv_norm, w_kv.astype(BF16), tm=_row_tile(s, 1024), tn=512)
    kslc, vslc, kwin, vwin = kv_prep(kv, cos2, sin2, tm=_row_tile(s, 512))
    cmp = compress(kv, cmp_pos, cmp_w1.astype(BF16), cmp_w2.astype(BF16))
    cmp = cmp.reshape(2 * NSA_KV, n_sel // SEL_CHUNK, SEL_CHUNK, 4, NSA_HD).transpose(0, 1, 3, 2, 4)
    cmp = cmp.reshape(2 * NSA_KV, n_seg, NSA_HD)
    cmp = cmp.astype(BF16)
    block_in_window = (jnp.arange(s, dtype=jnp.int32) // SLC_LEN) % LANES
    onehot = (block_in_window[:, None] == jnp.arange(LANES, dtype=jnp.int32)[None, :]).astype(BF16)
    kslc_aug = jnp.concatenate([kslc, jnp.broadcast_to(onehot[None], (NSA_KV, s, LANES))], axis=2)
    return dict(kc=cmp[:NSA_KV], vct=cmp[NSA_KV:].transpose(0, 2, 1), kslc=kslc_aug, vslct=_with_ones_rows(vslc),
                kwin=kwin, vwint=_with_ones_rows(vwin))


def _nsa_layer_heads(x, gain, w_in, gate_bias, shared, cos2t, sin2t):
    s = x.shape[0]
    n_gate = 3 * NSA_GROUP
    w_q, w_g, w_mq = jnp.split(w_in, [MAIN_W, MAIN_W + 3 * NSA_HEADS], axis=1)
    w_main = jnp.concatenate([w_q, w_mq], axis=1).astype(BF16)
    w_gate = jnp.concatenate([_pad_cols(w_g[:, g * n_gate:(g + 1) * n_gate], LANES) for g in range(NSA_KV)], axis=1)
    b_gate = jnp.concatenate(
        [_pad_cols(gate_bias[None, g * n_gate:(g + 1) * n_gate], LANES) for g in range(NSA_KV)], axis=1)
    tm = _row_tile(s, 1024)
    proj, gate_logits = rms_matmul(x, gain, w_main, w_gate.astype(BF16), tm=tm, tn=512, out_dtype=BF16)
    o = nsa_attention(proj, cos2t, sin2t, gate_logits, b_gate, shared["kc"], shared["vct"], shared["kslc"],
                      shared["vslct"], shared["kwin"], shared["vwint"])
    return o, proj, MAIN_W // MEM_W


def kernel(x, mem, norm_gains, ffn_w_gate, ffn_w_up, ffn_w_down, mem_norm, mem_w_kv, w_out, a_w_in, a_gate_bias,
           a_head_norm, kv_norm, w_kv, cmp_pos, cmp_w1, cmp_w2, b_w_in, b_gate_bias):
    batch, seq, d_model = x.shape
    depth = norm_gains.shape[0]
    n_a = a_w_in.shape[0]
    assert seq % (SLC_LEN * LANES) == 0 and seq % ML_CHUNK == 0, "sequence must tile the selection-block lanes"
    cos2, sin2 = _rope_tables(seq)
    cos2t, sin2t = cos2.T, sin2.T
    tm_ffn = _row_tile(seq, 512)

    w_gate16, w_up16, w_down16 = ffn_w_gate.astype(BF16), ffn_w_up.astype(BF16), ffn_w_down.astype(BF16)
    w_out16 = w_out.astype(BF16)

    def ffn(xb, layer, half, g_pre, g_post):
        return half_ffn(xb, g_pre, g_post, w_gate16, w_up16, w_down16, layer, half, tm=tm_ffn, tf=512)

    outs = []
    for b in range(batch):
        xb = x[b]
        shared = None
        for layer in range(depth):
            if layer == n_a:
                shared = _shared_kv(xb, kv_norm, w_kv, cmp_pos, cmp_w1, cmp_w2, cos2, sin2)
            g = norm_gains[layer]
            xb = ffn(xb, layer, 0, g[0], g[1])
            mem_kt, mem_v = _mem_kv(mem[b], mem_norm[layer], mem_w_kv[layer])
            if layer < n_a:
                main, proj, mq_block = _mlstm_layer_heads(xb, g[2], a_w_in[layer], a_gate_bias[layer],
                                                          a_head_norm[layer])
            else:
                lb = layer - n_a
                main, proj, mq_block = _nsa_layer_heads(xb, g[2], b_w_in[lb], b_gate_bias[lb], shared, cos2t, sin2t)
            xb = out_proj(main, proj, mq_block, mem_kt, mem_v, xb, w_out16, layer, g[3], tm=_row_tile(seq, 512))
            xb = ffn(xb, layer, 1, g[4], g[5])
        outs.append(xb)
    return jnp.stack(outs, axis=0)
```

```python
import functools

import jax
import jax.numpy as jnp
from jax import lax
from jax.experimental import pallas as pl
from jax.experimental.pallas import tpu as pltpu

F32 = jnp.float32
BF16 = jnp.bfloat16

MEM_HEADS = 4
MEM_HD = 128
MEM_W = MEM_HEADS * MEM_HD
ML_HEADS = 4
ML_DV = 384
ML_DQK = 192
MAIN_W = ML_HEADS * ML_DV
NSA_HEADS = 12
NSA_HD = 128
NSA_KV = 2
NSA_GROUP = NSA_HEADS // NSA_KV
GROUP_W = NSA_GROUP * NSA_HD
CMP_LEN = 32
CMP_STRIDE = 16
SLC_LEN = 64
SLC_TOPK = 16
WIN = 512
FFN_RES = 0.5
ROPE_THETA = 10000.0
EPS = 1e-6
NEG = -1e30
BIG = 1e30
REMOVED = -3e38
LOG2E = 1.4426950408889634
TINY = 1e-30

LANES = 128
VMEM_LIMIT_BYTES = 56 * 1024 * 1024

PROJ_ROWS = 1024
PROJ_COLS = 512
FFN_ROWS = 512
FFN_HIDDEN = 512
OUT_PROJ_ROWS = 512
KV_PREP_ROWS = 512

ML_CHUNK = 256
ML_DQK_PAD = 256
NSA_QB = 128
SLC_TILE = 1024
ONES_ROWS = 16
SEL_CHUNK = 64


def _params(*semantics):
    return pltpu.CompilerParams(dimension_semantics=semantics, vmem_limit_bytes=VMEM_LIMIT_BYTES)


def _rms(x, gain):
    return x * lax.rsqrt(jnp.mean(x * x, axis=-1, keepdims=True) + EPS) * gain


def _dot(a, b):
    return jnp.dot(a, b, preferred_element_type=F32)


def _exact_dot(a, b01):
    a1 = a.astype(BF16)
    r1 = a - a1.astype(F32)
    a2 = r1.astype(BF16)
    a3 = (r1 - a2.astype(F32)).astype(BF16)
    return _dot(a1, b01) + _dot(a2, b01) + _dot(a3, b01)


def _exact_dot_left(b01, a):
    a1 = a.astype(BF16)
    r1 = a - a1.astype(F32)
    a2 = r1.astype(BF16)
    a3 = (r1 - a2.astype(F32)).astype(BF16)
    return _dot(b01, a1) + _dot(b01, a2) + _dot(b01, a3)


def _rms_matmul_kernel(*refs, has_side):
    if has_side:
        x_ref, g_ref, w_ref, wside_ref, o_ref, oside_ref, xn_ref = refs
    else:
        x_ref, g_ref, w_ref, o_ref, xn_ref = refs

    @pl.when(pl.program_id(1) == 0)
    def _():
        xn_ref[...] = _rms(x_ref[...], g_ref[...]).astype(BF16)
        if has_side:
            oside_ref[...] = _dot(xn_ref[...], wside_ref[...])

    o_ref[...] = _dot(xn_ref[...], w_ref[...]).astype(o_ref.dtype)


def rms_matmul(x, gain, w, w_side=None, *, tm, tn, out_dtype=F32):
    s, k = x.shape
    n = w.shape[1]
    in_specs = [
        pl.BlockSpec((tm, k), lambda i, j: (i, 0)),
        pl.BlockSpec((1, k), lambda i, j: (0, 0)),
        pl.BlockSpec((k, tn), lambda i, j: (0, j)),
    ]
    out_specs = pl.BlockSpec((tm, tn), lambda i, j: (i, j))
    out_shape = jax.ShapeDtypeStruct((s, n), out_dtype)
    args = (x, gain.reshape(1, k), w)
    if w_side is not None:
        n_side = w_side.shape[1]
        in_specs.append(pl.BlockSpec((k, n_side), lambda i, j: (0, 0)))
        out_specs = [out_specs, pl.BlockSpec((tm, n_side), lambda i, j: (i, 0))]
        out_shape = [out_shape, jax.ShapeDtypeStruct((s, n_side), F32)]
        args = args + (w_side,)
    return pl.pallas_call(
        functools.partial(_rms_matmul_kernel, has_side=w_side is not None),
        grid=(s // tm, n // tn),
        in_specs=in_specs,
        out_specs=out_specs,
        out_shape=out_shape,
        scratch_shapes=[pltpu.VMEM((tm, k), BF16)],
        compiler_params=_params("parallel", "arbitrary"),
        name="rms_matmul",
    )(*args)


def _ffn_kernel(x_ref, gpre_ref, gpost_ref, wg_ref, wu_ref, wd_ref, o_ref, xn_ref):
    j = pl.program_id(1)

    @pl.when(j == 0)
    def _():
        xn_ref[...] = _rms(x_ref[...], gpre_ref[...]).astype(BF16)
        o_ref[...] = jnp.zeros_like(o_ref)

    xn = xn_ref[...]
    gate = _dot(xn, wg_ref[...])
    up = _dot(xn, wu_ref[...])
    hidden = (gate * jax.nn.sigmoid(gate) * up).astype(BF16)
    o_ref[...] += _dot(hidden, wd_ref[...])

    @pl.when(j == pl.num_programs(1) - 1)
    def _():
        o_ref[...] = x_ref[...] + FFN_RES * _rms(o_ref[...], gpost_ref[...])


def half_ffn(x, g_pre, g_post, w_gate, w_up, w_down, layer, half, *, tm, tf):
    s, d = x.shape
    d_ff = w_gate.shape[3]
    return pl.pallas_call(
        _ffn_kernel,
        grid=(s // tm, d_ff // tf),
        in_specs=[
            pl.BlockSpec((tm, d), lambda i, j: (i, 0)),
            pl.BlockSpec((1, d), lambda i, j: (0, 0)),
            pl.BlockSpec((1, d), lambda i, j: (0, 0)),
            pl.BlockSpec((None, None, d, tf), lambda i, j: (layer, half, 0, j)),
            pl.BlockSpec((None, None, d, tf), lambda i, j: (layer, half, 0, j)),
            pl.BlockSpec((None, None, tf, d), lambda i, j: (layer, half, j, 0)),
        ],
        out_specs=pl.BlockSpec((tm, d), lambda i, j: (i, 0)),
        out_shape=jax.ShapeDtypeStruct((s, d), F32),
        scratch_shapes=[pltpu.VMEM((tm, d), BF16)],
        compiler_params=_params("parallel", "arbitrary"),
        name="half_ffn",
    )(x, g_pre.reshape(1, d), g_post.reshape(1, d), w_gate, w_up, w_down)


def _out_proj_kernel(h_ref, mq_ref, kt_ref, v_ref, x_ref, wh_ref, wm_ref, g_ref, o_ref):
    scale = MEM_HD ** -0.5
    heads = []
    for h in range(MEM_HEADS):
        cols = slice(h * MEM_HD, (h + 1) * MEM_HD)
        s = _dot(mq_ref[:, cols], kt_ref[cols, :]) * scale
        e = jnp.exp(s - jnp.max(s, axis=-1, keepdims=True))
        p = e / jnp.sum(e, axis=-1, keepdims=True)
        heads.append(_dot(p.astype(BF16), v_ref[:, cols]).astype(BF16))
    y = _dot(h_ref[...].astype(BF16), wh_ref[...]) + _dot(jnp.concatenate(heads, axis=1), wm_ref[...])
    o_ref[...] = x_ref[...] + _rms(y, g_ref[...])


def out_proj(main, proj, mq_block, mem_kt, mem_v, x, w, layer, gain, *, tm):
    s, d = x.shape
    km = main.shape[1]
    m = mem_v.shape[0]
    assert km % MEM_W == 0
    return pl.pallas_call(
        _out_proj_kernel,
        grid=(s // tm,),
        in_specs=[
            pl.BlockSpec((tm, km), lambda i: (i, 0)),
            pl.BlockSpec((tm, MEM_W), lambda i: (i, mq_block)),
            pl.BlockSpec((MEM_W, m), lambda i: (0, 0)),
            pl.BlockSpec((m, MEM_W), lambda i: (0, 0)),
            pl.BlockSpec((tm, d), lambda i: (i, 0)),
            pl.BlockSpec((None, km, d), lambda i: (layer, 0, 0)),
            pl.BlockSpec((None, MEM_W, d), lambda i: (layer, km // MEM_W, 0)),
            pl.BlockSpec((1, d), lambda i: (0, 0)),
        ],
        out_specs=pl.BlockSpec((tm, d), lambda i: (i, 0)),
        out_shape=jax.ShapeDtypeStruct((s, d), F32),
        compiler_params=_params("parallel"),
        name="out_proj",
    )(main, proj, mem_kt, mem_v, x, w, w, gain.reshape(1, d))


def _log_sigmoid(x):
    return jnp.minimum(x, 0.0) - jnp.log1p(jnp.exp(-jnp.abs(x)))


def _mlstm_kernel(q_ref, k_ref, v_ref, o_ref, grow_ref, gcol_ref, brow_ref, bcol_ref, hn_ref,
                  out_ref, c_ref, m_ref):
    chunk = grow_ref.shape[1]

    @pl.when(pl.program_id(0) == 0)
    def _():
        c_ref[...] = jnp.zeros_like(c_ref)
        m_ref[...] = jnp.zeros_like(m_ref)

    grow = grow_ref[...] + brow_ref[...]
    gcol = gcol_ref[...] + bcol_ref[...]
    r_idx = lax.broadcasted_iota(jnp.int32, (chunk, chunk), 0)
    c_idx = lax.broadcasted_iota(jnp.int32, (chunk, chunk), 1)
    causal = c_idx <= r_idx
    b_rows = _exact_dot(_log_sigmoid(grow), (r_idx <= c_idx).astype(BF16))
    b_cols = _exact_dot_left(causal.astype(BF16), _log_sigmoid(gcol))
    ones_col = (lax.broadcasted_iota(jnp.int32, (chunk, LANES), 1) == 0).astype(BF16)

    for h in range(ML_HEADS):
        b_r = b_rows[ML_HEADS + h:ML_HEADS + h + 1, :]
        i_r = grow[h:h + 1, :]
        b_c = b_cols[:, ML_HEADS + h:ML_HEADS + h + 1]
        m_prev = m_ref[h:h + 1, 0:1]
        dmat = jnp.where(causal, b_c - b_r + i_r, -jnp.inf)
        m_inter = b_c + m_prev
        m_vec = jnp.maximum(jnp.max(dmat, axis=-1, keepdims=True), m_inter)
        qk_cols = slice(h * ML_DQK_PAD, (h + 1) * ML_DQK_PAD)
        q = q_ref[:, qk_cols]
        kt = (k_ref[:, qk_cols].astype(F32) * (ML_DQK ** -0.5)).T
        smat = _dot(q, kt.astype(BF16)) * jnp.exp(dmat - m_vec)
        inter = jnp.exp(m_inter - m_vec)
        vcols = slice(h * ML_DV, (h + 1) * ML_DV)
        v_aug = jnp.concatenate([v_ref[:, vcols], ones_col], axis=1)
        c_prev = c_ref[h]
        num_aug = _dot(smat.astype(BF16), v_aug) + inter * _dot(q, c_prev.astype(BF16))
        den = num_aug[:, ML_DV:ML_DV + 1]
        hval = num_aug[:, :ML_DV] / jnp.maximum(jnp.abs(den), jnp.exp(-m_vec))
        m_last = m_vec[chunk - 1:chunk, :]
        b_last = b_c[chunk - 1:chunk, :]
        w_r = jnp.exp(b_last - b_r + i_r - m_last)
        decay = jnp.exp(b_last + m_prev - m_last)
        c_ref[h] = decay * c_prev + _dot((kt * w_r).astype(BF16), v_aug)
        m_ref[h:h + 1, :] = jnp.broadcast_to(m_last, (1, LANES))
        out_ref[:, vcols] = _rms(hval, hn_ref[:, vcols]) * jax.nn.sigmoid(o_ref[:, vcols].astype(F32))


def mlstm_heads(proj, q_block, k_block, v_block, o_block, gates_row, gates_col, bias, head_norm):
    s = proj.shape[0]
    chunk = ML_CHUNK
    h2 = 2 * ML_HEADS
    qk_w = ML_HEADS * ML_DQK_PAD
    return pl.pallas_call(
        _mlstm_kernel,
        grid=(s // chunk,),
        in_specs=[
            pl.BlockSpec((chunk, qk_w), lambda c: (c, q_block)),
            pl.BlockSpec((chunk, qk_w), lambda c: (c, k_block)),
            pl.BlockSpec((chunk, MAIN_W), lambda c: (c, v_block)),
            pl.BlockSpec((chunk, MAIN_W), lambda c: (c, o_block)),
            pl.BlockSpec((h2, chunk), lambda c: (0, c)),
            pl.BlockSpec((chunk, h2), lambda c: (c, 0)),
            pl.BlockSpec((h2, 1), lambda c: (0, 0)),
            pl.BlockSpec((1, h2), lambda c: (0, 0)),
            pl.BlockSpec((1, MAIN_W), lambda c: (0, 0)),
        ],
        out_specs=pl.BlockSpec((chunk, MAIN_W), lambda c: (c, 0)),
        out_shape=jax.ShapeDtypeStruct((s, MAIN_W), F32),
        scratch_shapes=[pltpu.VMEM((ML_HEADS, ML_DQK_PAD, ML_DV + LANES), F32), pltpu.VMEM((8, LANES), F32)],
        compiler_params=_params("arbitrary"),
        name="mlstm",
    )(proj, proj, proj, proj, gates_row, gates_col, bias.reshape(h2, 1), bias.reshape(1, h2),
      head_norm.reshape(1, MAIN_W))


def _rope(x, cos2, sin2):
    return x * cos2 + pltpu.roll(x, NSA_HD // 2, axis=1) * sin2


def _kv_prep_kernel(kv_ref, cos_ref, sin_ref, kslc_ref, vslc_ref, kwin_ref, vwin_ref):
    cos2 = cos_ref[...]
    sin2 = sin_ref[...]
    for g in range(NSA_KV):
        def col(c):
            start = (c * NSA_KV + g) * NSA_HD
            return kv_ref[:, start:start + NSA_HD]
        kslc_ref[g] = _rope(col(2), cos2, sin2).astype(BF16)
        vslc_ref[g] = col(3).astype(BF16)
        kwin_ref[g] = _rope(col(4), cos2, sin2).astype(BF16)
        vwin_ref[g] = col(5).astype(BF16)


def kv_prep(kv, cos2, sin2, *, tm):
    s, w = kv.shape
    out = jax.ShapeDtypeStruct((NSA_KV, s, NSA_HD), BF16)
    ospec = pl.BlockSpec((NSA_KV, tm, NSA_HD), lambda i: (0, i, 0))
    return pl.pallas_call(
        _kv_prep_kernel,
        grid=(s // tm,),
        in_specs=[
            pl.BlockSpec((tm, w), lambda i: (i, 0)),
            pl.BlockSpec((tm, NSA_HD), lambda i: (i, 0)),
            pl.BlockSpec((tm, NSA_HD), lambda i: (i, 0)),
        ],
        out_specs=[ospec] * 4,
        out_shape=[out] * 4,
        compiler_params=_params("parallel"),
        name="kv_prep",
    )(kv, cos2, sin2)


def _compress_kernel(seg_ref, pos_ref, w1_ref, w2_ref, o_ref):
    n_seg = seg_ref.shape[0]
    first, second = None, None
    for tok in range(CMP_STRIDE):
        x = seg_ref[:, tok, :]
        for half in range(2):
            row = half * CMP_STRIDE + tok
            part = _dot((x + pos_ref[row:row + 1, :]).astype(BF16), w1_ref[row * NSA_HD:(row + 1) * NSA_HD, :])
            if half == 0:
                first = part if first is None else first + part
            else:
                second = part if second is None else second + part
    hidden = first + pltpu.roll(second, n_seg - 1, axis=0)
    act = jax.nn.gelu(hidden, approximate=True)
    o_ref[...] = _dot(act.astype(BF16), w2_ref[...])


def compress(kv, pos, w1, w2):
    s, w = kv.shape
    n_seg = s // CMP_STRIDE
    hid = w1.shape[2]
    wmap = lambda cg: (cg // NSA_KV, 0, 0)
    return pl.pallas_call(
        _compress_kernel,
        grid=(2 * NSA_KV,),
        in_specs=[
            pl.BlockSpec((n_seg, CMP_STRIDE, NSA_HD), lambda cg: (0, 0, cg)),
            pl.BlockSpec((None, CMP_LEN, NSA_HD), wmap),
            pl.BlockSpec((None, CMP_LEN * NSA_HD, hid), wmap),
            pl.BlockSpec((None, hid, NSA_HD), wmap),
        ],
        out_specs=pl.BlockSpec((None, n_seg, NSA_HD), lambda cg: (cg, 0, 0)),
        out_shape=jax.ShapeDtypeStruct((2 * NSA_KV, n_seg, NSA_HD), F32),
        compiler_params=_params("parallel"),
        name="compress",
    )(kv.reshape(n_seg, CMP_STRIDE, w), pos, w1, w2)


def _rope_t(x, cos2t, sin2t):
    half = NSA_HD // 2
    return x * cos2t + jnp.concatenate([x[half:], x[:half]], axis=0) * sin2t


def _col_max(chunks):
    return jnp.max(functools.reduce(jnp.maximum, chunks), axis=0, keepdims=True)


def _nsa_kernel(q_ref, cos_ref, sin_ref, gl_ref, gb_ref, kc_ref, vct_ref, kslc_ref, vslct_ref, kwin_ref, vwint_ref,
                o_ref, qt_ref, qaug_ref, sel_ref, p_ref, acc_ref, s_ref, m_ref, ocmp_ref, owin_ref):
    qi = pl.program_id(1)
    qb = NSA_QB
    n_cmp = kc_ref.shape[0]
    n_sel = n_cmp // 4
    t0 = qi * qb
    t_row = t0 + lax.broadcasted_iota(jnp.int32, (1, qb), 1)
    hcols = [slice(j * qb, (j + 1) * qb) for j in range(NSA_GROUP)]

    cos2t = cos_ref[...]
    sin2t = sin_ref[...]
    for j in range(NSA_GROUP):
        qt = q_ref[:, j * NSA_HD:(j + 1) * NSA_HD].astype(F32).T * (NSA_HD ** -0.5 * LOG2E)
        qt_ref[:, hcols[j]] = qt.astype(BF16)
        qaug_ref[0:NSA_HD, hcols[j]] = _rope_t(qt, cos2t, sin2t).astype(BF16)

    def compressed_and_select(n_chunks):
        rows = 4 * SEL_CHUNK * n_chunks
        n_blk = SEL_CHUNK * n_chunks
        s_c = _dot(kc_ref[0:rows, :], qt_ref[...])
        blk_i = lax.broadcasted_iota(jnp.int32, (SEL_CHUNK, qb), 0)
        pieces = [(c, r) for c in range(n_chunks) for r in range(4)]

        def piece_rows(c, r):
            return slice((4 * c + r) * SEL_CHUNK, (4 * c + r + 1) * SEL_CHUNK)

        cmp_bias = {(c, r): jnp.where(SLC_LEN * (blk_i + SEL_CHUNK * c) + (CMP_STRIDE * r + CMP_LEN - 1) <= t_row,
                                      0.0, NEG) for c, r in pieces}
        has_valid = (t_row >= CMP_LEN - 1).astype(F32)
        imp = {cr: jnp.zeros((SEL_CHUNK, qb), F32) for cr in pieces}
        for j in range(NSA_GROUP):
            sb = {cr: s_c[piece_rows(*cr), hcols[j]] + cmp_bias[cr] for cr in pieces}
            m = _col_max(list(sb.values()))
            e = {cr: jnp.exp2(sb[cr] - m) for cr in pieces}
            den = functools.reduce(jnp.add, [jnp.sum(x, axis=0, keepdims=True) for x in e.values()])
            rinv = has_valid / jnp.maximum(den, TINY)
            for cr in pieces:
                p = e[cr] * rinv
                imp[cr] = imp[cr] + p
                p_ref[piece_rows(*cr), hcols[j]] = p.astype(BF16)
        ocmp_ref[...] = _dot(vct_ref[:, 0:rows], p_ref[0:rows, :])

        imp_r = [jnp.concatenate([imp[(c, r)] for c in range(n_chunks)], axis=0) for r in range(4)]
        blk = lax.broadcasted_iota(jnp.int32, (n_blk, qb), 0)
        imp3_prev = jnp.where(blk == 0, 0.0, pltpu.roll(imp_r[3], 1, axis=0))
        p_slc = ((((((imp_r[0] + imp3_prev) + imp_r[1]) + imp_r[0]) + imp_r[2]) + imp_r[1]) + imp_r[3]) + imp_r[2]
        cur = jnp.right_shift(t_row, SLC_LEN.bit_length() - 1)
        forced = (blk == 0) | (blk == cur) | (blk == cur - 1)
        score = jnp.where(blk > cur, NEG, jnp.where(forced, BIG, p_slc))
        blk_f = blk.astype(F32)
        sel = jnp.zeros((n_blk, qb), F32)
        for _ in range(SLC_TOPK):
            top = jnp.max(score, axis=0, keepdims=True)
            first = jnp.min(jnp.where(score == top, blk_f, float(n_blk)), axis=0, keepdims=True)
            hit = blk_f == first
            sel = jnp.where(hit, 1.0, sel)
            score = jnp.where(hit, REMOVED, score)
        sel_ref[0:n_blk, :] = sel
        if n_blk < n_sel:
            sel_ref[n_blk:n_sel, :] = jnp.zeros((n_sel - n_blk, qb), F32)

    chunks_needed = (t0 + qb - 1) // (SLC_LEN * SEL_CHUNK) + 1
    for n_chunks in range(1, n_sel // SEL_CHUNK + 1):
        pl.when(chunks_needed == n_chunks)(functools.partial(compressed_and_select, n_chunks))

    acc_ref[...] = jnp.zeros_like(acc_ref)
    m_ref[...] = jnp.full_like(m_ref, REMOVED)
    tiles_per_window = LANES * SLC_LEN // SLC_TILE
    kpos0 = lax.broadcasted_iota(jnp.int32, (SLC_TILE, qb), 0)

    def scores(kt, slot):
        first = pl.multiple_of((kt // tiles_per_window) * LANES, LANES)
        unselected = ((sel_ref[pl.ds(first, LANES), :] - 1.0) * BIG).astype(BF16)
        for j in range(NSA_GROUP):
            qaug_ref[NSA_HD:NSA_HD + LANES, hcols[j]] = unselected
        start = pl.multiple_of(kt * SLC_TILE, SLC_TILE)
        s_ref[slot] = _dot(kslc_ref[pl.ds(start, SLC_TILE), :], qaug_ref[...])

    def weights(kt, slot, causal):
        if causal:
            future = jnp.where(kpos0 + kt * SLC_TILE <= t_row, 0.0, NEG)
        alpha = []
        for j in range(NSA_GROUP):
            s = s_ref[slot, :, hcols[j]]
            if causal:
                s = s + future
            m_old = m_ref[0:1, hcols[j]]
            m_j = jnp.maximum(m_old, jnp.max(s, axis=0, keepdims=True))
            p_ref[slot * SLC_TILE:(slot + 1) * SLC_TILE, hcols[j]] = jnp.exp2(s - m_j).astype(BF16)
            m_ref[0:1, hcols[j]] = m_j
            alpha.append(jnp.exp2(m_old - m_j))
        return jnp.concatenate(alpha, axis=1)

    def weighted_values(kt, slot):
        start = pl.multiple_of(kt * SLC_TILE, SLC_TILE)
        return _dot(vslct_ref[:, pl.ds(start, SLC_TILE)], p_ref[slot * SLC_TILE:(slot + 1) * SLC_TILE, :])

    def absorb(kt, slot, causal):
        alpha = weights(kt, slot, causal)
        acc_ref[...] = acc_ref[...] * alpha + weighted_values(kt, slot)

    last = (t0 + qb - 1) // SLC_TILE
    scores(0, 0)

    span = qb + WIN
    start = pl.multiple_of(jnp.maximum(t0 - WIN, 0), qb)
    s_w = _dot(kwin_ref[pl.ds(start, span), :], qaug_ref[0:NSA_HD, :])
    spos0 = lax.broadcasted_iota(jnp.int32, (qb, qb), 0)
    win_bias = []
    for c in range(span // qb):
        spos = spos0 + (start + c * qb)
        win_bias.append(jnp.where((spos <= t_row) & (spos > t_row - WIN), 0.0, NEG))
    for j in range(NSA_GROUP):
        sb = [s_w[c * qb:(c + 1) * qb, hcols[j]] + win_bias[c] for c in range(span // qb)]
        m = _col_max(sb)
        for c in range(span // qb):
            p_ref[c * qb:(c + 1) * qb, hcols[j]] = jnp.exp2(sb[c] - m).astype(BF16)
    pw = _dot(vwint_ref[:, pl.ds(start, span)], p_ref[0:span, :])
    owin_ref[...] = pw[:NSA_HD] / pw[NSA_HD:NSA_HD + 1]

    def pair(i, carry):
        scores(2 * i + 1, 1)
        absorb(2 * i, 0, False)
        scores(jnp.minimum(2 * i + 2, last), 0)
        absorb(2 * i + 1, 1, False)
        return carry

    def quad(i, carry):
        return pair(2 * i + 1, pair(2 * i, carry))

    lax.fori_loop(0, last // 4, quad, 0)
    lax.fori_loop(2 * (last // 4), last // 2, pair, 0)

    @pl.when(last % 2 == 0)
    def _():
        absorb(last, 0, True)

    @pl.when(last % 2 == 1)
    def _():
        scores(last, 1)
        absorb(last - 1, 0, False)
        absorb(last, 1, True)

    acc = acc_ref[...]
    o_slc = acc[:NSA_HD] / acc[NSA_HD:NSA_HD + 1]

    gates_t = jax.nn.sigmoid(gl_ref[...] + gb_ref[...]).T
    for j in range(NSA_GROUP):
        o_t = (gates_t[3 * j:3 * j + 1] * ocmp_ref[:, hcols[j]]
               + gates_t[3 * j + 1:3 * j + 2] * o_slc[:, hcols[j]]
               + gates_t[3 * j + 2:3 * j + 3] * owin_ref[:, hcols[j]])
        o_ref[:, j * NSA_HD:(j + 1) * NSA_HD] = o_t.T


def nsa_attention(proj, cos2t, sin2t, gate_logits, gate_bias, kc, vct, kslc, vslct, kwin, vwint):
    s = proj.shape[0]
    n_cmp = kc.shape[1]
    rows_aug = NSA_HD + ONES_ROWS
    width = NSA_GROUP * NSA_QB
    qspec = pl.BlockSpec((NSA_QB, GROUP_W), lambda g, i: (i, g))
    tspec = pl.BlockSpec((NSA_HD, NSA_QB), lambda g, i: (0, i))

    def resident(shape):
        return pl.BlockSpec((None,) + shape, lambda g, i: (g, 0, 0), pipeline_mode=pl.Buffered(1))

    return pl.pallas_call(
        _nsa_kernel,
        grid=(NSA_KV, s // NSA_QB),
        in_specs=[
            qspec, tspec, tspec,
            pl.BlockSpec((NSA_QB, LANES), lambda g, i: (i, g)),
            pl.BlockSpec((1, LANES), lambda g, i: (0, g)),
            resident((n_cmp, NSA_HD)), resident((NSA_HD, n_cmp)),
            resident((s, NSA_HD + LANES)), resident((rows_aug, s)),
            resident((s, NSA_HD)), resident((rows_aug, s)),
        ],
        out_specs=qspec,
        out_shape=jax.ShapeDtypeStruct((s, NSA_KV * GROUP_W), F32),
        scratch_shapes=[
            pltpu.VMEM((NSA_HD, width), BF16),
            pltpu.VMEM((NSA_HD + LANES, width), BF16),
            pltpu.VMEM((n_cmp // 4, NSA_QB), F32),
            pltpu.VMEM((max(n_cmp, NSA_QB + WIN, 2 * SLC_TILE), width), BF16),
            pltpu.VMEM((rows_aug, width), F32),
            pltpu.VMEM((2, SLC_TILE, width), F32),
            pltpu.VMEM((8, width), F32),
            pltpu.VMEM((NSA_HD, width), F32),
            pltpu.VMEM((NSA_HD, width), F32),
        ],
        compiler_params=_params("parallel", "arbitrary"),
        name="nsa_attention",
    )(proj, cos2t, sin2t, gate_logits, gate_bias, kc, vct, kslc, vslct, kwin, vwint)


def _row_tile(s, want):
    return want if s % want == 0 else s


def _rope_tables(seq):
    inv = ROPE_THETA ** (-jnp.arange(0, NSA_HD, 2, dtype=F32) / NSA_HD)
    ang = jnp.arange(seq, dtype=F32)[:, None] * inv[None, :]
    cos, sin = jnp.cos(ang), jnp.sin(ang)
    return jnp.concatenate([cos, cos], axis=1), jnp.concatenate([-sin, sin], axis=1)


def _pad_cols(a, width):
    return jnp.pad(a, ((0, 0), (0, width - a.shape[1])))


def _mem_kv(mem, gain, w_kv):
    kv = rms_matmul(mem, gain, w_kv.astype(BF16), tm=mem.shape[0], tn=MEM_W, out_dtype=BF16)
    return kv[:, :MEM_W].T, kv[:, MEM_W:]


def _mlstm_layer_heads(x, gain, w_in, gate_bias, head_norm):
    s = x.shape[0]
    qk_w = ML_HEADS * ML_DQK
    w_q, w_k, w_v, w_o, w_g, w_mq = jnp.split(
        w_in, [qk_w, 2 * qk_w, 2 * qk_w + MAIN_W, 2 * qk_w + 2 * MAIN_W, 2 * qk_w + 2 * MAIN_W + 2 * ML_HEADS], axis=1)

    def head_slots(w):
        w = w.reshape(w.shape[0], ML_HEADS, ML_DQK)
        return jnp.pad(w, ((0, 0), (0, 0), (0, ML_DQK_PAD - ML_DQK))).reshape(w.shape[0], ML_HEADS * ML_DQK_PAD)

    w_main = jnp.concatenate([w_v, w_o, head_slots(w_q), head_slots(w_k), w_mq], axis=1).astype(BF16)
    qk_pad_w = ML_HEADS * ML_DQK_PAD
    q_block = 2 * MAIN_W // qk_pad_w
    mq_block = (2 * MAIN_W + 2 * qk_pad_w) // MEM_W
    assert 2 * MAIN_W % qk_pad_w == 0 and (2 * MAIN_W + 2 * qk_pad_w) % MEM_W == 0
    proj, gates = rms_matmul(x, gain, w_main, _pad_cols(w_g, LANES).astype(BF16), tm=_row_tile(s, PROJ_ROWS),
                             tn=PROJ_COLS, out_dtype=BF16)
    gates = gates[:, :2 * ML_HEADS]
    h = mlstm_heads(proj, q_block, q_block + 1, 0, 1, gates.T, gates, gate_bias, head_norm)
    return h, proj, mq_block


def _with_ones_rows(v):
    g, s, _ = v.shape
    return jnp.concatenate([v.transpose(0, 2, 1), jnp.ones((g, ONES_ROWS, s), v.dtype)], axis=1)


def _shared_kv(x, kv_norm, w_kv, cmp_pos, cmp_w1, cmp_w2, cos2, sin2):
    s = x.shape[0]
    n_seg = s // CMP_STRIDE
    n_sel = s // SLC_LEN
    kv = rms_matmul(x, kv_norm, w_kv.astype(BF16), tm=_row_tile(s, PROJ_ROWS), tn=PROJ_COLS)
    kslc, vslc, kwin, vwin = kv_prep(kv, cos2, sin2, tm=_row_tile(s, KV_PREP_ROWS))
    cmp = compress(kv, cmp_pos, cmp_w1.astype(BF16), cmp_w2.astype(BF16))
    cmp = cmp.reshape(2 * NSA_KV, n_sel // SEL_CHUNK, SEL_CHUNK, 4, NSA_HD).transpose(0, 1, 3, 2, 4)
    cmp = cmp.reshape(2 * NSA_KV, n_seg, NSA_HD)
    cmp = cmp.astype(BF16)
    block_in_window = (jnp.arange(s, dtype=jnp.int32) // SLC_LEN) % LANES
    onehot = (block_in_window[:, None] == jnp.arange(LANES, dtype=jnp.int32)[None, :]).astype(BF16)
    kslc_aug = jnp.concatenate([kslc, jnp.broadcast_to(onehot[None], (NSA_KV, s, LANES))], axis=2)
    return dict(kc=cmp[:NSA_KV], vct=cmp[NSA_KV:].transpose(0, 2, 1), kslc=kslc_aug, vslct=_with_ones_rows(vslc),
                kwin=kwin, vwint=_with_ones_rows(vwin))


def _nsa_layer_heads(x, gain, w_in, gate_bias, shared, cos2t, sin2t):
    s = x.shape[0]
    n_gate = 3 * NSA_GROUP
    w_q, w_g, w_mq = jnp.split(w_in, [MAIN_W, MAIN_W + 3 * NSA_HEADS], axis=1)
    w_main = jnp.concatenate([w_q, w_mq], axis=1).astype(BF16)
    w_gate = jnp.concatenate([_pad_cols(w_g[:, g * n_gate:(g + 1) * n_gate], LANES) for g in range(NSA_KV)], axis=1)
    b_gate = jnp.concatenate(
        [_pad_cols(gate_bias[None, g * n_gate:(g + 1) * n_gate], LANES) for g in range(NSA_KV)], axis=1)
    proj, gate_logits = rms_matmul(x, gain, w_main, w_gate.astype(BF16), tm=_row_tile(s, PROJ_ROWS), tn=PROJ_COLS,
                                   out_dtype=BF16)
    o = nsa_attention(proj, cos2t, sin2t, gate_logits, b_gate, shared["kc"], shared["vct"], shared["kslc"],
                      shared["vslct"], shared["kwin"], shared["vwint"])
    return o, proj, MAIN_W // MEM_W


def kernel(x, mem, norm_gains, ffn_w_gate, ffn_w_up, ffn_w_down, mem_norm, mem_w_kv, w_out, a_w_in, a_gate_bias,
           a_head_norm, kv_norm, w_kv, cmp_pos, cmp_w1, cmp_w2, b_w_in, b_gate_bias):
    batch, seq, d_model = x.shape
    depth = norm_gains.shape[0]
    n_a = a_w_in.shape[0]
    assert seq % (SLC_LEN * LANES) == 0 and seq % ML_CHUNK == 0, "sequence must tile the selection-block lanes"
    cos2, sin2 = _rope_tables(seq)
    cos2t, sin2t = cos2.T, sin2.T
    w_gate16, w_up16, w_down16 = ffn_w_gate.astype(BF16), ffn_w_up.astype(BF16), ffn_w_down.astype(BF16)
    w_out16 = w_out.astype(BF16)

    def ffn(xb, layer, half, g_pre, g_post):
        return half_ffn(xb, g_pre, g_post, w_gate16, w_up16, w_down16, layer, half, tm=_row_tile(seq, FFN_ROWS),
                        tf=FFN_HIDDEN)

    outs = []
    for b in range(batch):
        xb = x[b]
        shared = None
        for layer in range(depth):
            if layer == n_a:
                shared = _shared_kv(xb, kv_norm, w_kv, cmp_pos, cmp_w1, cmp_w2, cos2, sin2)
            g = norm_gains[layer]
            xb = ffn(xb, layer, 0, g[0], g[1])
            mem_kt, mem_v = _mem_kv(mem[b], mem_norm[layer], mem_w_kv[layer])
            if layer < n_a:
                main, proj, mq_block = _mlstm_layer_heads(xb, g[2], a_w_in[layer], a_gate_bias[layer],
                                                          a_head_norm[layer])
            else:
                lb = layer - n_a
                main, proj, mq_block = _nsa_layer_heads(xb, g[2], b_w_in[lb], b_gate_bias[lb], shared, cos2t, sin2t)
            xb = out_proj(main, proj, mq_block, mem_kt, mem_v, xb, w_out16, layer, g[3],
                          tm=_row_tile(seq, OUT_PROJ_ROWS))
            xb = ffn(xb, layer, 1, g[4], g[5])
        outs.append(xb)
    return jnp.stack(outs, axis=0)
```

```python
import functools

import jax
import jax.numpy as jnp
from jax import lax
from jax.experimental import pallas as pl
from jax.experimental.pallas import tpu as pltpu

F32 = jnp.float32
BF16 = jnp.bfloat16

MEM_HEADS = 4
MEM_HD = 128
MEM_W = MEM_HEADS * MEM_HD
ML_HEADS = 4
ML_DV = 384
ML_DQK = 192
MAIN_W = ML_HEADS * ML_DV
NSA_HEADS = 12
NSA_HD = 128
NSA_KV = 2
NSA_GROUP = NSA_HEADS // NSA_KV
GROUP_W = NSA_GROUP * NSA_HD
CMP_LEN = 32
CMP_STRIDE = 16
SLC_LEN = 64
SLC_TOPK = 16
WIN = 512
FFN_RES = 0.5
ROPE_THETA = 10000.0
EPS = 1e-6
NEG = -1e30
BIG = 1e30
REMOVED = -3e38
LOG2E = 1.4426950408889634
TINY = 1e-30

LANES = 128
VMEM_LIMIT_BYTES = 56 * 1024 * 1024

PROJ_ROWS = 1024
PROJ_COLS = 1024
FFN_ROWS = 512
FFN_HIDDEN = 512
OUT_PROJ_ROWS = 512
KV_PREP_ROWS = 512

ML_CHUNK = 256
ML_DQK_PAD = 256
NSA_QB = 128
SLC_TILE = 1024
ONES_ROWS = 16
SEL_CHUNK = 64


def _params(*semantics):
    return pltpu.CompilerParams(dimension_semantics=semantics, vmem_limit_bytes=VMEM_LIMIT_BYTES)


def _rms(x, gain):
    return x * lax.rsqrt(jnp.mean(x * x, axis=-1, keepdims=True) + EPS) * gain


def _dot(a, b):
    return jnp.dot(a, b, preferred_element_type=F32)


def _exact_dot(a, b01):
    a1 = a.astype(BF16)
    r1 = a - a1.astype(F32)
    a2 = r1.astype(BF16)
    a3 = (r1 - a2.astype(F32)).astype(BF16)
    return _dot(a1, b01) + _dot(a2, b01) + _dot(a3, b01)


def _exact_dot_left(b01, a):
    a1 = a.astype(BF16)
    r1 = a - a1.astype(F32)
    a2 = r1.astype(BF16)
    a3 = (r1 - a2.astype(F32)).astype(BF16)
    return _dot(b01, a1) + _dot(b01, a2) + _dot(b01, a3)


def _rms_matmul_kernel(*refs, has_side):
    if has_side:
        x_ref, g_ref, w_ref, wside_ref, o_ref, oside_ref, xn_ref = refs
    else:
        x_ref, g_ref, w_ref, o_ref, xn_ref = refs

    @pl.when(pl.program_id(1) == 0)
    def _():
        xn_ref[...] = _rms(x_ref[...], g_ref[...]).astype(BF16)
        if has_side:
            oside_ref[...] = _dot(xn_ref[...], wside_ref[...])

    o_ref[...] = _dot(xn_ref[...], w_ref[...]).astype(o_ref.dtype)


def rms_matmul(x, gain, w, w_side=None, *, tm, tn, out_dtype=F32):
    s, k = x.shape
    n = w.shape[1]
    in_specs = [
        pl.BlockSpec((tm, k), lambda i, j: (i, 0)),
        pl.BlockSpec((1, k), lambda i, j: (0, 0)),
        pl.BlockSpec((k, tn), lambda i, j: (0, j)),
    ]
    out_specs = pl.BlockSpec((tm, tn), lambda i, j: (i, j))
    out_shape = jax.ShapeDtypeStruct((s, n), out_dtype)
    args = (x, gain.reshape(1, k), w)
    if w_side is not None:
        n_side = w_side.shape[1]
        in_specs.append(pl.BlockSpec((k, n_side), lambda i, j: (0, 0)))
        out_specs = [out_specs, pl.BlockSpec((tm, n_side), lambda i, j: (i, 0))]
        out_shape = [out_shape, jax.ShapeDtypeStruct((s, n_side), F32)]
        args = args + (w_side,)
    return pl.pallas_call(
        functools.partial(_rms_matmul_kernel, has_side=w_side is not None),
        grid=(s // tm, n // tn),
        in_specs=in_specs,
        out_specs=out_specs,
        out_shape=out_shape,
        scratch_shapes=[pltpu.VMEM((tm, k), BF16)],
        compiler_params=_params("parallel", "arbitrary"),
        name="rms_matmul",
    )(*args)


def _ffn_kernel(x_ref, gpre_ref, gpost_ref, wg_ref, wu_ref, wd_ref, o_ref, xn_ref):
    j = pl.program_id(1)

    @pl.when(j == 0)
    def _():
        xn_ref[...] = _rms(x_ref[...], gpre_ref[...]).astype(BF16)
        o_ref[...] = jnp.zeros_like(o_ref)

    xn = xn_ref[...]
    gate = _dot(xn, wg_ref[...])
    up = _dot(xn, wu_ref[...])
    hidden = (gate * jax.nn.sigmoid(gate) * up).astype(BF16)
    o_ref[...] += _dot(hidden, wd_ref[...])

    @pl.when(j == pl.num_programs(1) - 1)
    def _():
        o_ref[...] = x_ref[...] + FFN_RES * _rms(o_ref[...], gpost_ref[...])


def half_ffn(x, g_pre, g_post, w_gate, w_up, w_down, layer, half, *, tm, tf):
    s, d = x.shape
    d_ff = w_gate.shape[3]
    return pl.pallas_call(
        _ffn_kernel,
        grid=(s // tm, d_ff // tf),
        in_specs=[
            pl.BlockSpec((tm, d), lambda i, j: (i, 0)),
            pl.BlockSpec((1, d), lambda i, j: (0, 0)),
            pl.BlockSpec((1, d), lambda i, j: (0, 0)),
            pl.BlockSpec((None, None, d, tf), lambda i, j: (layer, half, 0, j)),
            pl.BlockSpec((None, None, d, tf), lambda i, j: (layer, half, 0, j)),
            pl.BlockSpec((None, None, tf, d), lambda i, j: (layer, half, j, 0)),
        ],
        out_specs=pl.BlockSpec((tm, d), lambda i, j: (i, 0)),
        out_shape=jax.ShapeDtypeStruct((s, d), F32),
        scratch_shapes=[pltpu.VMEM((tm, d), BF16)],
        compiler_params=_params("parallel", "arbitrary"),
        name="half_ffn",
    )(x, g_pre.reshape(1, d), g_post.reshape(1, d), w_gate, w_up, w_down)


def _out_proj_kernel(h_ref, mq_ref, kt_ref, v_ref, x_ref, wh_ref, wm_ref, g_ref, o_ref):
    scale = MEM_HD ** -0.5
    heads = []
    for h in range(MEM_HEADS):
        cols = slice(h * MEM_HD, (h + 1) * MEM_HD)
        s = _dot(mq_ref[:, cols], kt_ref[cols, :]) * scale
        e = jnp.exp(s - jnp.max(s, axis=-1, keepdims=True))
        p = e / jnp.sum(e, axis=-1, keepdims=True)
        heads.append(_dot(p.astype(BF16), v_ref[:, cols]).astype(BF16))
    y = _dot(h_ref[...].astype(BF16), wh_ref[...]) + _dot(jnp.concatenate(heads, axis=1), wm_ref[...])
    o_ref[...] = x_ref[...] + _rms(y, g_ref[...])


def out_proj(main, proj, mq_block, mem_kt, mem_v, x, w, layer, gain, *, tm):
    s, d = x.shape
    km = main.shape[1]
    m = mem_v.shape[0]
    assert km % MEM_W == 0
    return pl.pallas_call(
        _out_proj_kernel,
        grid=(s // tm,),
        in_specs=[
            pl.BlockSpec((tm, km), lambda i: (i, 0)),
            pl.BlockSpec((tm, MEM_W), lambda i: (i, mq_block)),
            pl.BlockSpec((MEM_W, m), lambda i: (0, 0)),
            pl.BlockSpec((m, MEM_W), lambda i: (0, 0)),
            pl.BlockSpec((tm, d), lambda i: (i, 0)),
            pl.BlockSpec((None, km, d), lambda i: (layer, 0, 0)),
            pl.BlockSpec((None, MEM_W, d), lambda i: (layer, km // MEM_W, 0)),
            pl.BlockSpec((1, d), lambda i: (0, 0)),
        ],
        out_specs=pl.BlockSpec((tm, d), lambda i: (i, 0)),
        out_shape=jax.ShapeDtypeStruct((s, d), F32),
        compiler_params=_params("parallel"),
        name="out_proj",
    )(main, proj, mem_kt, mem_v, x, w, w, gain.reshape(1, d))


def _log_sigmoid(x):
    return jnp.minimum(x, 0.0) - jnp.log1p(jnp.exp(-jnp.abs(x)))


def _mlstm_kernel(q_ref, k_ref, v_ref, o_ref, grow_ref, gcol_ref, brow_ref, bcol_ref, hn_ref,
                  out_ref, c_ref, m_ref):
    chunk = grow_ref.shape[1]

    @pl.when(pl.program_id(0) == 0)
    def _():
        c_ref[...] = jnp.zeros_like(c_ref)
        m_ref[...] = jnp.zeros_like(m_ref)

    grow = grow_ref[...] + brow_ref[...]
    gcol = gcol_ref[...] + bcol_ref[...]
    r_idx = lax.broadcasted_iota(jnp.int32, (chunk, chunk), 0)
    c_idx = lax.broadcasted_iota(jnp.int32, (chunk, chunk), 1)
    causal = c_idx <= r_idx
    b_rows = _exact_dot(_log_sigmoid(grow), (r_idx <= c_idx).astype(BF16))
    b_cols = _exact_dot_left(causal.astype(BF16), _log_sigmoid(gcol))
    ones_col = (lax.broadcasted_iota(jnp.int32, (chunk, LANES), 1) == 0).astype(BF16)

    for h in range(ML_HEADS):
        b_r = b_rows[ML_HEADS + h:ML_HEADS + h + 1, :]
        i_r = grow[h:h + 1, :]
        b_c = b_cols[:, ML_HEADS + h:ML_HEADS + h + 1]
        m_prev = m_ref[h:h + 1, 0:1]
        dmat = jnp.where(causal, b_c - b_r + i_r, -jnp.inf)
        m_inter = b_c + m_prev
        m_vec = jnp.maximum(jnp.max(dmat, axis=-1, keepdims=True), m_inter)
        qk_cols = slice(h * ML_DQK_PAD, (h + 1) * ML_DQK_PAD)
        q = q_ref[:, qk_cols]
        kt = (k_ref[:, qk_cols].astype(F32) * (ML_DQK ** -0.5)).T
        smat = _dot(q, kt.astype(BF16)) * jnp.exp(dmat - m_vec)
        inter = jnp.exp(m_inter - m_vec)
        vcols = slice(h * ML_DV, (h + 1) * ML_DV)
        v_aug = jnp.concatenate([v_ref[:, vcols], ones_col], axis=1)
        c_prev = c_ref[h]
        num_aug = _dot(smat.astype(BF16), v_aug) + inter * _dot(q, c_prev.astype(BF16))
        den = num_aug[:, ML_DV:ML_DV + 1]
        hval = num_aug[:, :ML_DV] / jnp.maximum(jnp.abs(den), jnp.exp(-m_vec))
        m_last = m_vec[chunk - 1:chunk, :]
        b_last = b_c[chunk - 1:chunk, :]
        w_r = jnp.exp(b_last - b_r + i_r - m_last)
        decay = jnp.exp(b_last + m_prev - m_last)
        c_ref[h] = decay * c_prev + _dot((kt * w_r).astype(BF16), v_aug)
        m_ref[h:h + 1, :] = jnp.broadcast_to(m_last, (1, LANES))
        out_ref[:, vcols] = _rms(hval, hn_ref[:, vcols]) * jax.nn.sigmoid(o_ref[:, vcols].astype(F32))


def mlstm_heads(proj, q_block, k_block, v_block, o_block, gates_row, gates_col, bias, head_norm):
    s = proj.shape[0]
    chunk = ML_CHUNK
    h2 = 2 * ML_HEADS
    qk_w = ML_HEADS * ML_DQK_PAD
    return pl.pallas_call(
        _mlstm_kernel,
        grid=(s // chunk,),
        in_specs=[
            pl.BlockSpec((chunk, qk_w), lambda c: (c, q_block)),
            pl.BlockSpec((chunk, qk_w), lambda c: (c, k_block)),
            pl.BlockSpec((chunk, MAIN_W), lambda c: (c, v_block)),
            pl.BlockSpec((chunk, MAIN_W), lambda c: (c, o_block)),
            pl.BlockSpec((h2, chunk), lambda c: (0, c)),
            pl.BlockSpec((chunk, h2), lambda c: (c, 0)),
            pl.BlockSpec((h2, 1), lambda c: (0, 0)),
            pl.BlockSpec((1, h2), lambda c: (0, 0)),
            pl.BlockSpec((1, MAIN_W), lambda c: (0, 0)),
        ],
        out_specs=pl.BlockSpec((chunk, MAIN_W), lambda c: (c, 0)),
        out_shape=jax.ShapeDtypeStruct((s, MAIN_W), F32),
        scratch_shapes=[pltpu.VMEM((ML_HEADS, ML_DQK_PAD, ML_DV + LANES), F32), pltpu.VMEM((8, LANES), F32)],
        compiler_params=_params("arbitrary"),
        name="mlstm",
    )(proj, proj, proj, proj, gates_row, gates_col, bias.reshape(h2, 1), bias.reshape(1, h2),
      head_norm.reshape(1, MAIN_W))


def _rope(x, cos2, sin2):
    return x * cos2 + pltpu.roll(x, NSA_HD // 2, axis=1) * sin2


def _kv_prep_kernel(kv_ref, cos_ref, sin_ref, kslc_ref, vslc_ref, kwin_ref, vwin_ref):
    cos2 = cos_ref[...]
    sin2 = sin_ref[...]
    for g in range(NSA_KV):
        def col(c):
            start = (c * NSA_KV + g) * NSA_HD
            return kv_ref[:, start:start + NSA_HD]
        kslc_ref[g] = _rope(col(2), cos2, sin2).astype(BF16)
        vslc_ref[g] = col(3).astype(BF16)
        kwin_ref[g] = _rope(col(4), cos2, sin2).astype(BF16)
        vwin_ref[g] = col(5).astype(BF16)


def kv_prep(kv, cos2, sin2, *, tm):
    s, w = kv.shape
    out = jax.ShapeDtypeStruct((NSA_KV, s, NSA_HD), BF16)
    ospec = pl.BlockSpec((NSA_KV, tm, NSA_HD), lambda i: (0, i, 0))
    return pl.pallas_call(
        _kv_prep_kernel,
        grid=(s // tm,),
        in_specs=[
            pl.BlockSpec((tm, w), lambda i: (i, 0)),
            pl.BlockSpec((tm, NSA_HD), lambda i: (i, 0)),
            pl.BlockSpec((tm, NSA_HD), lambda i: (i, 0)),
        ],
        out_specs=[ospec] * 4,
        out_shape=[out] * 4,
        compiler_params=_params("parallel"),
        name="kv_prep",
    )(kv, cos2, sin2)


def _compress_kernel(seg_ref, pos_ref, w1_ref, w2_ref, o_ref):
    n_seg = seg_ref.shape[0]
    first, second = None, None
    for tok in range(CMP_STRIDE):
        x = seg_ref[:, tok, :]
        for half in range(2):
            row = half * CMP_STRIDE + tok
            part = _dot((x + pos_ref[row:row + 1, :]).astype(BF16), w1_ref[row * NSA_HD:(row + 1) * NSA_HD, :])
            if half == 0:
                first = part if first is None else first + part
            else:
                second = part if second is None else second + part
    hidden = first + pltpu.roll(second, n_seg - 1, axis=0)
    act = jax.nn.gelu(hidden, approximate=True)
    o_ref[...] = _dot(act.astype(BF16), w2_ref[...])


def compress(kv, pos, w1, w2):
    s, w = kv.shape
    n_seg = s // CMP_STRIDE
    hid = w1.shape[2]
    wmap = lambda cg: (cg // NSA_KV, 0, 0)
    return pl.pallas_call(
        _compress_kernel,
        grid=(2 * NSA_KV,),
        in_specs=[
            pl.BlockSpec((n_seg, CMP_STRIDE, NSA_HD), lambda cg: (0, 0, cg)),
            pl.BlockSpec((None, CMP_LEN, NSA_HD), wmap),
            pl.BlockSpec((None, CMP_LEN * NSA_HD, hid), wmap),
            pl.BlockSpec((None, hid, NSA_HD), wmap),
        ],
        out_specs=pl.BlockSpec((None, n_seg, NSA_HD), lambda cg: (cg, 0, 0)),
        out_shape=jax.ShapeDtypeStruct((2 * NSA_KV, n_seg, NSA_HD), F32),
        compiler_params=_params("parallel"),
        name="compress",
    )(kv.reshape(n_seg, CMP_STRIDE, w), pos, w1, w2)


def _rope_t(x, cos2t, sin2t):
    half = NSA_HD // 2
    return x * cos2t + jnp.concatenate([x[half:], x[:half]], axis=0) * sin2t


def _col_max(chunks):
    return jnp.max(functools.reduce(jnp.maximum, chunks), axis=0, keepdims=True)


def _nsa_kernel(q_ref, cos_ref, sin_ref, gl_ref, gb_ref, kc_ref, vct_ref, kslc_ref, vslct_ref, kwin_ref, vwint_ref,
                o_ref, qt_ref, qr_ref, mask_ref, p_ref, acc_ref, s_ref, m_ref, ocmp_ref, owin_ref):
    qi = pl.program_id(1)
    qb = NSA_QB
    n_cmp = kc_ref.shape[0]
    n_sel = n_cmp // 4
    t0 = qi * qb
    t_row = t0 + lax.broadcasted_iota(jnp.int32, (1, qb), 1)
    hcols = [slice(j * qb, (j + 1) * qb) for j in range(NSA_GROUP)]

    cos2t = cos_ref[...]
    sin2t = sin_ref[...]
    for j in range(NSA_GROUP):
        qt = q_ref[:, j * NSA_HD:(j + 1) * NSA_HD].astype(F32).T * (NSA_HD ** -0.5 * LOG2E)
        qt_ref[:, hcols[j]] = qt.astype(BF16)
        qr_ref[:, hcols[j]] = _rope_t(qt, cos2t, sin2t).astype(BF16)

    def compressed_and_select(n_chunks):
        rows = 4 * SEL_CHUNK * n_chunks
        n_blk = SEL_CHUNK * n_chunks
        s_c = _dot(kc_ref[0:rows, :], qt_ref[...])
        blk_i = lax.broadcasted_iota(jnp.int32, (SEL_CHUNK, qb), 0)
        pieces = [(c, r) for c in range(n_chunks) for r in range(4)]

        def piece_rows(c, r):
            return slice((4 * c + r) * SEL_CHUNK, (4 * c + r + 1) * SEL_CHUNK)

        cmp_bias = {(c, r): jnp.where(SLC_LEN * (blk_i + SEL_CHUNK * c) + (CMP_STRIDE * r + CMP_LEN - 1) <= t_row,
                                      0.0, NEG) for c, r in pieces}
        has_valid = (t_row >= CMP_LEN - 1).astype(F32)
        imp = {cr: jnp.zeros((SEL_CHUNK, qb), F32) for cr in pieces}
        for j in range(NSA_GROUP):
            sb = {cr: s_c[piece_rows(*cr), hcols[j]] + cmp_bias[cr] for cr in pieces}
            m = _col_max(list(sb.values()))
            e = {cr: jnp.exp2(sb[cr] - m) for cr in pieces}
            den = functools.reduce(jnp.add, [jnp.sum(x, axis=0, keepdims=True) for x in e.values()])
            rinv = has_valid / jnp.maximum(den, TINY)
            for cr in pieces:
                p = e[cr] * rinv
                imp[cr] = imp[cr] + p
                p_ref[piece_rows(*cr), hcols[j]] = p.astype(BF16)
        ocmp_ref[...] = _dot(vct_ref[:, 0:rows], p_ref[0:rows, :])

        imp_r = [jnp.concatenate([imp[(c, r)] for c in range(n_chunks)], axis=0) for r in range(4)]
        blk = lax.broadcasted_iota(jnp.int32, (n_blk, qb), 0)
        imp3_prev = jnp.where(blk == 0, 0.0, pltpu.roll(imp_r[3], 1, axis=0))
        p_slc = ((((((imp_r[0] + imp3_prev) + imp_r[1]) + imp_r[0]) + imp_r[2]) + imp_r[1]) + imp_r[3]) + imp_r[2]
        cur = jnp.right_shift(t_row, SLC_LEN.bit_length() - 1)
        forced = (blk == 0) | (blk == cur) | (blk == cur - 1)
        score = jnp.where(blk > cur, NEG, jnp.where(forced, BIG, p_slc))
        blk_f = blk.astype(F32)
        sel = jnp.zeros((n_blk, qb), F32)
        for _ in range(SLC_TOPK):
            top = jnp.max(score, axis=0, keepdims=True)
            first = jnp.min(jnp.where(score == top, blk_f, float(n_blk)), axis=0, keepdims=True)
            hit = blk_f == first
            sel = jnp.where(hit, 1.0, sel)
            score = jnp.where(hit, REMOVED, score)
        unselected = ((sel - 1.0) * BIG).astype(BF16)
        for j in range(NSA_GROUP):
            mask_ref[0:n_blk, hcols[j]] = unselected
        if n_blk < n_sel:
            mask_ref[n_blk:n_sel, :] = jnp.full((n_sel - n_blk, NSA_GROUP * qb), -BIG, BF16)

    chunks_needed = (t0 + qb - 1) // (SLC_LEN * SEL_CHUNK) + 1
    for n_chunks in range(1, n_sel // SEL_CHUNK + 1):
        pl.when(chunks_needed == n_chunks)(functools.partial(compressed_and_select, n_chunks))

    acc_ref[...] = jnp.zeros_like(acc_ref)
    m_ref[...] = jnp.full_like(m_ref, REMOVED)
    tiles_per_window = LANES * SLC_LEN // SLC_TILE
    kpos0 = lax.broadcasted_iota(jnp.int32, (SLC_TILE, qb), 0)

    def scores(kt, slot):
        first = pl.multiple_of((kt // tiles_per_window) * LANES, LANES)
        weights_aug = jnp.concatenate([qr_ref[...], mask_ref[pl.ds(first, LANES), :]], axis=0)
        start = pl.multiple_of(kt * SLC_TILE, SLC_TILE)
        s_ref[slot] = _dot(kslc_ref[pl.ds(start, SLC_TILE), :], weights_aug)

    def weights(kt, slot, causal):
        if causal:
            future = jnp.where(kpos0 + kt * SLC_TILE <= t_row, 0.0, NEG)
        alpha = []
        for j in range(NSA_GROUP):
            s = s_ref[slot, :, hcols[j]]
            if causal:
                s = s + future
            m_old = m_ref[0:1, hcols[j]]
            m_j = jnp.maximum(m_old, jnp.max(s, axis=0, keepdims=True))
            p_ref[slot * SLC_TILE:(slot + 1) * SLC_TILE, hcols[j]] = jnp.exp2(s - m_j).astype(BF16)
            m_ref[0:1, hcols[j]] = m_j
            alpha.append(jnp.exp2(m_old - m_j))
        return jnp.concatenate(alpha, axis=1)

    def weighted_values(kt, slot):
        start = pl.multiple_of(kt * SLC_TILE, SLC_TILE)
        return _dot(vslct_ref[:, pl.ds(start, SLC_TILE)], p_ref[slot * SLC_TILE:(slot + 1) * SLC_TILE, :])

    def absorb(kt, slot, causal):
        alpha = weights(kt, slot, causal)
        acc_ref[...] = acc_ref[...] * alpha + weighted_values(kt, slot)

    last = (t0 + qb - 1) // SLC_TILE
    scores(0, 0)

    span = qb + WIN
    start = pl.multiple_of(jnp.maximum(t0 - WIN, 0), qb)
    s_w = _dot(kwin_ref[pl.ds(start, span), :], qr_ref[...])
    spos0 = lax.broadcasted_iota(jnp.int32, (qb, qb), 0)
    win_bias = []
    for c in range(span // qb):
        spos = spos0 + (start + c * qb)
        win_bias.append(jnp.where((spos <= t_row) & (spos > t_row - WIN), 0.0, NEG))
    for j in range(NSA_GROUP):
        sb = [s_w[c * qb:(c + 1) * qb, hcols[j]] + win_bias[c] for c in range(span // qb)]
        m = _col_max(sb)
        for c in range(span // qb):
            p_ref[c * qb:(c + 1) * qb, hcols[j]] = jnp.exp2(sb[c] - m).astype(BF16)
    pw = _dot(vwint_ref[:, pl.ds(start, span)], p_ref[0:span, :])
    owin_ref[...] = pw[:NSA_HD] / pw[NSA_HD:NSA_HD + 1]

    def pair(i, carry):
        scores(2 * i + 1, 1)
        absorb(2 * i, 0, False)
        scores(jnp.minimum(2 * i + 2, last), 0)
        absorb(2 * i + 1, 1, False)
        return carry

    def quad(i, carry):
        return pair(2 * i + 1, pair(2 * i, carry))

    def octet(i, carry):
        return quad(2 * i + 1, quad(2 * i, carry))

    lax.fori_loop(0, last // 8, octet, 0)
    lax.fori_loop(2 * (last // 8), last // 4, quad, 0)
    lax.fori_loop(2 * (last // 4), last // 2, pair, 0)

    @pl.when(last % 2 == 0)
    def _():
        absorb(last, 0, True)

    @pl.when(last % 2 == 1)
    def _():
        scores(last, 1)
        absorb(last - 1, 0, False)
        absorb(last, 1, True)

    acc = acc_ref[...]
    o_slc = acc[:NSA_HD] / acc[NSA_HD:NSA_HD + 1]

    gates_t = jax.nn.sigmoid(gl_ref[...] + gb_ref[...]).T
    for j in range(NSA_GROUP):
        o_t = (gates_t[3 * j:3 * j + 1] * ocmp_ref[:, hcols[j]]
               + gates_t[3 * j + 1:3 * j + 2] * o_slc[:, hcols[j]]
               + gates_t[3 * j + 2:3 * j + 3] * owin_ref[:, hcols[j]])
        o_ref[:, j * NSA_HD:(j + 1) * NSA_HD] = o_t.T


def nsa_attention(proj, cos2t, sin2t, gate_logits, gate_bias, kc, vct, kslc, vslct, kwin, vwint):
    s = proj.shape[0]
    n_cmp = kc.shape[1]
    rows_aug = NSA_HD + ONES_ROWS
    width = NSA_GROUP * NSA_QB
    qspec = pl.BlockSpec((NSA_QB, GROUP_W), lambda g, i: (i, g))
    tspec = pl.BlockSpec((NSA_HD, NSA_QB), lambda g, i: (0, i))

    def resident(shape):
        return pl.BlockSpec((None,) + shape, lambda g, i: (g, 0, 0), pipeline_mode=pl.Buffered(1))

    return pl.pallas_call(
        _nsa_kernel,
        grid=(NSA_KV, s // NSA_QB),
        in_specs=[
            qspec, tspec, tspec,
            pl.BlockSpec((NSA_QB, LANES), lambda g, i: (i, g)),
            pl.BlockSpec((1, LANES), lambda g, i: (0, g)),
            resident((n_cmp, NSA_HD)), resident((NSA_HD, n_cmp)),
            resident((s, NSA_HD + LANES)), resident((rows_aug, s)),
            resident((s, NSA_HD)), resident((rows_aug, s)),
        ],
        out_specs=qspec,
        out_shape=jax.ShapeDtypeStruct((s, NSA_KV * GROUP_W), F32),
        scratch_shapes=[
            pltpu.VMEM((NSA_HD, width), BF16),
            pltpu.VMEM((NSA_HD, width), BF16),
            pltpu.VMEM((n_cmp // 4, width), BF16),
            pltpu.VMEM((max(n_cmp, NSA_QB + WIN, 2 * SLC_TILE), width), BF16),
            pltpu.VMEM((rows_aug, width), F32),
            pltpu.VMEM((2, SLC_TILE, width), F32),
            pltpu.VMEM((8, width), F32),
            pltpu.VMEM((NSA_HD, width), F32),
            pltpu.VMEM((NSA_HD, width), F32),
        ],
        compiler_params=_params("parallel", "arbitrary"),
        name="nsa_attention",
    )(proj, cos2t, sin2t, gate_logits, gate_bias, kc, vct, kslc, vslct, kwin, vwint)


def _row_tile(s, want):
    return want if s % want == 0 else s


def _col_tile(n):
    return max(t for t in range(LANES, PROJ_COLS + 1, LANES) if n % t == 0)


def _rope_tables(seq):
    inv = ROPE_THETA ** (-jnp.arange(0, NSA_HD, 2, dtype=F32) / NSA_HD)
    ang = jnp.arange(seq, dtype=F32)[:, None] * inv[None, :]
    cos, sin = jnp.cos(ang), jnp.sin(ang)
    return jnp.concatenate([cos, cos], axis=1), jnp.concatenate([-sin, sin], axis=1)


def _pad_cols(a, width):
    return jnp.pad(a, ((0, 0), (0, width - a.shape[1])))


def _mem_kv(mem, gain, w_kv):
    kv = rms_matmul(mem, gain, w_kv.astype(BF16), tm=mem.shape[0], tn=MEM_W, out_dtype=BF16)
    return kv[:, :MEM_W].T, kv[:, MEM_W:]


def _mlstm_layer_heads(x, gain, w_in, gate_bias, head_norm):
    s = x.shape[0]
    qk_w = ML_HEADS * ML_DQK
    w_q, w_k, w_v, w_o, w_g, w_mq = jnp.split(
        w_in, [qk_w, 2 * qk_w, 2 * qk_w + MAIN_W, 2 * qk_w + 2 * MAIN_W, 2 * qk_w + 2 * MAIN_W + 2 * ML_HEADS], axis=1)

    def head_slots(w):
        w = w.reshape(w.shape[0], ML_HEADS, ML_DQK)
        return jnp.pad(w, ((0, 0), (0, 0), (0, ML_DQK_PAD - ML_DQK))).reshape(w.shape[0], ML_HEADS * ML_DQK_PAD)

    w_main = jnp.concatenate([w_v, w_o, head_slots(w_q), head_slots(w_k), w_mq], axis=1).astype(BF16)
    qk_pad_w = ML_HEADS * ML_DQK_PAD
    q_block = 2 * MAIN_W // qk_pad_w
    mq_block = (2 * MAIN_W + 2 * qk_pad_w) // MEM_W
    assert 2 * MAIN_W % qk_pad_w == 0 and (2 * MAIN_W + 2 * qk_pad_w) % MEM_W == 0
    proj, gates = rms_matmul(x, gain, w_main, _pad_cols(w_g, LANES).astype(BF16), tm=_row_tile(s, PROJ_ROWS),
                             tn=_col_tile(w_main.shape[1]), out_dtype=BF16)
    gates = gates[:, :2 * ML_HEADS]
    h = mlstm_heads(proj, q_block, q_block + 1, 0, 1, gates.T, gates, gate_bias, head_norm)
    return h, proj, mq_block


def _with_ones_rows(v):
    g, s, _ = v.shape
    return jnp.concatenate([v.transpose(0, 2, 1), jnp.ones((g, ONES_ROWS, s), v.dtype)], axis=1)


def _shared_kv(x, kv_norm, w_kv, cmp_pos, cmp_w1, cmp_w2, cos2, sin2):
    s = x.shape[0]
    n_seg = s // CMP_STRIDE
    n_sel = s // SLC_LEN
    kv = rms_matmul(x, kv_norm, w_kv.astype(BF16), tm=_row_tile(s, PROJ_ROWS), tn=_col_tile(w_kv.shape[1]))
    kslc, vslc, kwin, vwin = kv_prep(kv, cos2, sin2, tm=_row_tile(s, KV_PREP_ROWS))
    cmp = compress(kv, cmp_pos, cmp_w1.astype(BF16), cmp_w2.astype(BF16))
    cmp = cmp.reshape(2 * NSA_KV, n_sel // SEL_CHUNK, SEL_CHUNK, 4, NSA_HD).transpose(0, 1, 3, 2, 4)
    cmp = cmp.reshape(2 * NSA_KV, n_seg, NSA_HD)
    cmp = cmp.astype(BF16)
    block_in_window = (jnp.arange(s, dtype=jnp.int32) // SLC_LEN) % LANES
    onehot = (block_in_window[:, None] == jnp.arange(LANES, dtype=jnp.int32)[None, :]).astype(BF16)
    kslc_aug = jnp.concatenate([kslc, jnp.broadcast_to(onehot[None], (NSA_KV, s, LANES))], axis=2)
    return dict(kc=cmp[:NSA_KV], vct=cmp[NSA_KV:].transpose(0, 2, 1), kslc=kslc_aug, vslct=_with_ones_rows(vslc),
                kwin=kwin, vwint=_with_ones_rows(vwin))


def _nsa_layer_heads(x, gain, w_in, gate_bias, shared, cos2t, sin2t):
    s = x.shape[0]
    n_gate = 3 * NSA_GROUP
    w_q, w_g, w_mq = jnp.split(w_in, [MAIN_W, MAIN_W + 3 * NSA_HEADS], axis=1)
    w_main = jnp.concatenate([w_q, w_mq], axis=1).astype(BF16)
    w_gate = jnp.concatenate([_pad_cols(w_g[:, g * n_gate:(g + 1) * n_gate], LANES) for g in range(NSA_KV)], axis=1)
    b_gate = jnp.concatenate(
        [_pad_cols(gate_bias[None, g * n_gate:(g + 1) * n_gate], LANES) for g in range(NSA_KV)], axis=1)
    proj, gate_logits = rms_matmul(x, gain, w_main, w_gate.astype(BF16), tm=_row_tile(s, PROJ_ROWS),
                                   tn=_col_tile(w_main.shape[1]), out_dtype=BF16)
    o = nsa_attention(proj, cos2t, sin2t, gate_logits, b_gate, shared["kc"], shared["vct"], shared["kslc"],
                      shared["vslct"], shared["kwin"], shared["vwint"])
    return o, proj, MAIN_W // MEM_W


def kernel(x, mem, norm_gains, ffn_w_gate, ffn_w_up, ffn_w_down, mem_norm, mem_w_kv, w_out, a_w_in, a_gate_bias,
           a_head_norm, kv_norm, w_kv, cmp_pos, cmp_w1, cmp_w2, b_w_in, b_gate_bias):
    batch, seq, d_model = x.shape
    depth = norm_gains.shape[0]
    n_a = a_w_in.shape[0]
    assert seq % (SLC_LEN * LANES) == 0 and seq % ML_CHUNK == 0, "sequence must tile the selection-block lanes"
    cos2, sin2 = _rope_tables(seq)
    cos2t, sin2t = cos2.T, sin2.T
    w_gate16, w_up16, w_down16 = ffn_w_gate.astype(BF16), ffn_w_up.astype(BF16), ffn_w_down.astype(BF16)
    w_out16 = w_out.astype(BF16)

    def ffn(xb, layer, half, g_pre, g_post):
        return half_ffn(xb, g_pre, g_post, w_gate16, w_up16, w_down16, layer, half, tm=_row_tile(seq, FFN_ROWS),
                        tf=FFN_HIDDEN)

    outs = []
    for b in range(batch):
        xb = x[b]
        shared = None
        for layer in range(depth):
            if layer == n_a:
                shared = _shared_kv(xb, kv_norm, w_kv, cmp_pos, cmp_w1, cmp_w2, cos2, sin2)
            g = norm_gains[layer]
            xb = ffn(xb, layer, 0, g[0], g[1])
            mem_kt, mem_v = _mem_kv(mem[b], mem_norm[layer], mem_w_kv[layer])
            if layer < n_a:
                main, proj, mq_block = _mlstm_layer_heads(xb, g[2], a_w_in[layer], a_gate_bias[layer],
                                                          a_head_norm[layer])
            else:
                lb = layer - n_a
                main, proj, mq_block = _nsa_layer_heads(xb, g[2], b_w_in[lb], b_gate_bias[lb], shared, cos2t, sin2t)
            xb = out_proj(main, proj, mq_block, mem_kt, mem_v, xb, w_out16, layer, g[3],
                          tm=_row_tile(seq, OUT_PROJ_ROWS))
            xb = ffn(xb, layer, 1, g[4], g[5])
        outs.append(xb)
    return jnp.stack(outs, axis=0)
```

```python
import functools

import jax
import jax.numpy as jnp
from jax import lax
from jax.experimental import pallas as pl
from jax.experimental.pallas import tpu as pltpu

F32 = jnp.float32
BF16 = jnp.bfloat16

MEM_HEADS = 4
MEM_HD = 128
MEM_W = MEM_HEADS * MEM_HD
ML_HEADS = 4
ML_DV = 384
ML_DQK = 192
MAIN_W = ML_HEADS * ML_DV
NSA_HEADS = 12
NSA_HD = 128
NSA_KV = 2
NSA_GROUP = NSA_HEADS // NSA_KV
GROUP_W = NSA_GROUP * NSA_HD
CMP_LEN = 32
CMP_STRIDE = 16
SLC_LEN = 64
SLC_TOPK = 16
WIN = 512
FFN_RES = 0.5
ROPE_THETA = 10000.0
EPS = 1e-6
NEG = -1e30
BIG = 1e30
REMOVED = -3e38
LOG2E = 1.4426950408889634
TINY = 1e-30

LANES = 128
VMEM_LIMIT_BYTES = 56 * 1024 * 1024

PROJ_ROWS = 1024
PROJ_COLS = 1024
FFN_ROWS = 512
FFN_HIDDEN = 512
OUT_PROJ_ROWS = 512
KV_PREP_ROWS = 512

ML_CHUNK = 256
ML_DQK_PAD = 256
NSA_QB = 128
SLC_TILE = 1024
ONES_ROWS = 16
SEL_CHUNK = 64


def _params(*semantics):
    return pltpu.CompilerParams(dimension_semantics=semantics, vmem_limit_bytes=VMEM_LIMIT_BYTES)


def _rms(x, gain):
    return x * lax.rsqrt(jnp.mean(x * x, axis=-1, keepdims=True) + EPS) * gain


def _dot(a, b):
    return jnp.dot(a, b, preferred_element_type=F32)


def _exact_dot(a, b01):
    a1 = a.astype(BF16)
    r1 = a - a1.astype(F32)
    a2 = r1.astype(BF16)
    a3 = (r1 - a2.astype(F32)).astype(BF16)
    return _dot(a1, b01) + _dot(a2, b01) + _dot(a3, b01)


def _exact_dot_left(b01, a):
    a1 = a.astype(BF16)
    r1 = a - a1.astype(F32)
    a2 = r1.astype(BF16)
    a3 = (r1 - a2.astype(F32)).astype(BF16)
    return _dot(b01, a1) + _dot(b01, a2) + _dot(b01, a3)


def _rms_matmul_kernel(*refs, has_side):
    if has_side:
        x_ref, g_ref, w_ref, wside_ref, o_ref, oside_ref, xn_ref = refs
    else:
        x_ref, g_ref, w_ref, o_ref, xn_ref = refs

    @pl.when(pl.program_id(1) == 0)
    def _():
        xn_ref[...] = _rms(x_ref[...], g_ref[...]).astype(BF16)
        if has_side:
            oside_ref[...] = _dot(xn_ref[...], wside_ref[...])

    o_ref[...] = _dot(xn_ref[...], w_ref[...]).astype(o_ref.dtype)


def rms_matmul(x, gain, w, w_side=None, *, tm, tn, out_dtype=F32):
    s, k = x.shape
    n = w.shape[1]
    in_specs = [
        pl.BlockSpec((tm, k), lambda i, j: (i, 0)),
        pl.BlockSpec((1, k), lambda i, j: (0, 0)),
        pl.BlockSpec((k, tn), lambda i, j: (0, j)),
    ]
    out_specs = pl.BlockSpec((tm, tn), lambda i, j: (i, j))
    out_shape = jax.ShapeDtypeStruct((s, n), out_dtype)
    args = (x, gain.reshape(1, k), w)
    if w_side is not None:
        n_side = w_side.shape[1]
        in_specs.append(pl.BlockSpec((k, n_side), lambda i, j: (0, 0)))
        out_specs = [out_specs, pl.BlockSpec((tm, n_side), lambda i, j: (i, 0))]
        out_shape = [out_shape, jax.ShapeDtypeStruct((s, n_side), F32)]
        args = args + (w_side,)
    return pl.pallas_call(
        functools.partial(_rms_matmul_kernel, has_side=w_side is not None),
        grid=(s // tm, n // tn),
        in_specs=in_specs,
        out_specs=out_specs,
        out_shape=out_shape,
        scratch_shapes=[pltpu.VMEM((tm, k), BF16)],
        compiler_params=_params("parallel", "arbitrary"),
        name="rms_matmul",
    )(*args)


def _ffn_kernel(x_ref, gpre_ref, gpost_ref, wg_ref, wu_ref, wd_ref, o_ref, xn_ref):
    j = pl.program_id(1)

    @pl.when(j == 0)
    def _():
        xn_ref[...] = _rms(x_ref[...], gpre_ref[...]).astype(BF16)
        o_ref[...] = jnp.zeros_like(o_ref)

    xn = xn_ref[...]
    half = wg_ref.shape[1] // 2
    gates, ups = [], []
    for h in range(2):
        gates.append(_dot(xn, wg_ref[:, h * half:(h + 1) * half]))
        ups.append(_dot(xn, wu_ref[:, h * half:(h + 1) * half]))
    down = None
    for h in range(2):
        hidden = (gates[h] * jax.nn.sigmoid(gates[h]) * ups[h]).astype(BF16)
        part = _dot(hidden, wd_ref[h * half:(h + 1) * half, :])
        down = part if down is None else down + part
    o_ref[...] += down

    @pl.when(j == pl.num_programs(1) - 1)
    def _():
        o_ref[...] = x_ref[...] + FFN_RES * _rms(o_ref[...], gpost_ref[...])


def half_ffn(x, g_pre, g_post, w_gate, w_up, w_down, layer, half, *, tm, tf):
    s, d = x.shape
    d_ff = w_gate.shape[3]
    return pl.pallas_call(
        _ffn_kernel,
        grid=(s // tm, d_ff // tf),
        in_specs=[
            pl.BlockSpec((tm, d), lambda i, j: (i, 0)),
            pl.BlockSpec((1, d), lambda i, j: (0, 0)),
            pl.BlockSpec((1, d), lambda i, j: (0, 0)),
            pl.BlockSpec((None, None, d, tf), lambda i, j: (layer, half, 0, j)),
            pl.BlockSpec((None, None, d, tf), lambda i, j: (layer, half, 0, j)),
            pl.BlockSpec((None, None, tf, d), lambda i, j: (layer, half, j, 0)),
        ],
        out_specs=pl.BlockSpec((tm, d), lambda i, j: (i, 0)),
        out_shape=jax.ShapeDtypeStruct((s, d), F32),
        scratch_shapes=[pltpu.VMEM((tm, d), BF16)],
        compiler_params=_params("parallel", "arbitrary"),
        name="half_ffn",
    )(x, g_pre.reshape(1, d), g_post.reshape(1, d), w_gate, w_up, w_down)


def _out_proj_kernel(h_ref, mq_ref, kt_ref, v_ref, x_ref, wh_ref, wm_ref, g_ref, o_ref):
    scale = MEM_HD ** -0.5
    heads = []
    for h in range(MEM_HEADS):
        cols = slice(h * MEM_HD, (h + 1) * MEM_HD)
        s = _dot(mq_ref[:, cols], kt_ref[cols, :]) * scale
        e = jnp.exp(s - jnp.max(s, axis=-1, keepdims=True))
        p = e / jnp.sum(e, axis=-1, keepdims=True)
        heads.append(_dot(p.astype(BF16), v_ref[:, cols]).astype(BF16))
    y = _dot(h_ref[...].astype(BF16), wh_ref[...]) + _dot(jnp.concatenate(heads, axis=1), wm_ref[...])
    o_ref[...] = x_ref[...] + _rms(y, g_ref[...])


def out_proj(main, proj, mq_block, mem_kt, mem_v, x, w, layer, gain, *, tm):
    s, d = x.shape
    km = main.shape[1]
    m = mem_v.shape[0]
    assert km % MEM_W == 0
    return pl.pallas_call(
        _out_proj_kernel,
        grid=(s // tm,),
        in_specs=[
            pl.BlockSpec((tm, km), lambda i: (i, 0)),
            pl.BlockSpec((tm, MEM_W), lambda i: (i, mq_block)),
            pl.BlockSpec((MEM_W, m), lambda i: (0, 0)),
            pl.BlockSpec((m, MEM_W), lambda i: (0, 0)),
            pl.BlockSpec((tm, d), lambda i: (i, 0)),
            pl.BlockSpec((None, km, d), lambda i: (layer, 0, 0)),
            pl.BlockSpec((None, MEM_W, d), lambda i: (layer, km // MEM_W, 0)),
            pl.BlockSpec((1, d), lambda i: (0, 0)),
        ],
        out_specs=pl.BlockSpec((tm, d), lambda i: (i, 0)),
        out_shape=jax.ShapeDtypeStruct((s, d), F32),
        compiler_params=_params("parallel"),
        name="out_proj",
    )(main, proj, mem_kt, mem_v, x, w, w, gain.reshape(1, d))


def _log_sigmoid(x):
    return jnp.minimum(x, 0.0) - jnp.log1p(jnp.exp(-jnp.abs(x)))


def _mlstm_kernel(q_ref, k_ref, v_ref, o_ref, grow_ref, gcol_ref, brow_ref, bcol_ref, hn_ref,
                  out_ref, c_ref, m_ref):
    chunk = grow_ref.shape[1]

    @pl.when(pl.program_id(0) == 0)
    def _():
        c_ref[...] = jnp.zeros_like(c_ref)
        m_ref[...] = jnp.zeros_like(m_ref)

    grow = grow_ref[...] + brow_ref[...]
    gcol = gcol_ref[...] + bcol_ref[...]
    r_idx = lax.broadcasted_iota(jnp.int32, (chunk, chunk), 0)
    c_idx = lax.broadcasted_iota(jnp.int32, (chunk, chunk), 1)
    causal = c_idx <= r_idx
    b_rows = _exact_dot(_log_sigmoid(grow), (r_idx <= c_idx).astype(BF16))
    b_cols = _exact_dot_left(causal.astype(BF16), _log_sigmoid(gcol))
    ones_col = (lax.broadcasted_iota(jnp.int32, (chunk, LANES), 1) == 0).astype(BF16)

    for h in range(ML_HEADS):
        b_r = b_rows[ML_HEADS + h:ML_HEADS + h + 1, :]
        i_r = grow[h:h + 1, :]
        b_c = b_cols[:, ML_HEADS + h:ML_HEADS + h + 1]
        m_prev = m_ref[h:h + 1, 0:1]
        dmat = jnp.where(causal, b_c - b_r + i_r, -jnp.inf)
        m_inter = b_c + m_prev
        m_vec = jnp.maximum(jnp.max(dmat, axis=-1, keepdims=True), m_inter)
        qk_cols = slice(h * ML_DQK_PAD, (h + 1) * ML_DQK_PAD)
        q = q_ref[:, qk_cols]
        kt = (k_ref[:, qk_cols].astype(F32) * (ML_DQK ** -0.5)).T
        smat = _dot(q, kt.astype(BF16)) * jnp.exp(dmat - m_vec)
        inter = jnp.exp(m_inter - m_vec)
        vcols = slice(h * ML_DV, (h + 1) * ML_DV)
        v_aug = jnp.concatenate([v_ref[:, vcols], ones_col], axis=1)
        c_prev = c_ref[h]
        num_aug = _dot(smat.astype(BF16), v_aug) + inter * _dot(q, c_prev.astype(BF16))
        den = num_aug[:, ML_DV:ML_DV + 1]
        hval = num_aug[:, :ML_DV] / jnp.maximum(jnp.abs(den), jnp.exp(-m_vec))
        m_last = m_vec[chunk - 1:chunk, :]
        b_last = b_c[chunk - 1:chunk, :]
        w_r = jnp.exp(b_last - b_r + i_r - m_last)
        decay = jnp.exp(b_last + m_prev - m_last)
        c_ref[h] = decay * c_prev + _dot((kt * w_r).astype(BF16), v_aug)
        m_ref[h:h + 1, :] = jnp.broadcast_to(m_last, (1, LANES))
        out_ref[:, vcols] = _rms(hval, hn_ref[:, vcols]) * jax.nn.sigmoid(o_ref[:, vcols].astype(F32))


def mlstm_heads(proj, q_block, k_block, v_block, o_block, gates_row, gates_col, bias, head_norm):
    s = proj.shape[0]
    chunk = ML_CHUNK
    h2 = 2 * ML_HEADS
    qk_w = ML_HEADS * ML_DQK_PAD
    return pl.pallas_call(
        _mlstm_kernel,
        grid=(s // chunk,),
        in_specs=[
            pl.BlockSpec((chunk, qk_w), lambda c: (c, q_block)),
            pl.BlockSpec((chunk, qk_w), lambda c: (c, k_block)),
            pl.BlockSpec((chunk, MAIN_W), lambda c: (c, v_block)),
            pl.BlockSpec((chunk, MAIN_W), lambda c: (c, o_block)),
            pl.BlockSpec((h2, chunk), lambda c: (0, c)),
            pl.BlockSpec((chunk, h2), lambda c: (c, 0)),
            pl.BlockSpec((h2, 1), lambda c: (0, 0)),
            pl.BlockSpec((1, h2), lambda c: (0, 0)),
            pl.BlockSpec((1, MAIN_W), lambda c: (0, 0)),
        ],
        out_specs=pl.BlockSpec((chunk, MAIN_W), lambda c: (c, 0)),
        out_shape=jax.ShapeDtypeStruct((s, MAIN_W), F32),
        scratch_shapes=[pltpu.VMEM((ML_HEADS, ML_DQK_PAD, ML_DV + LANES), F32), pltpu.VMEM((8, LANES), F32)],
        compiler_params=_params("arbitrary"),
        name="mlstm",
    )(proj, proj, proj, proj, gates_row, gates_col, bias.reshape(h2, 1), bias.reshape(1, h2),
      head_norm.reshape(1, MAIN_W))


def _rope(x, cos2, sin2):
    return x * cos2 + pltpu.roll(x, NSA_HD // 2, axis=1) * sin2


def _kv_prep_kernel(kv_ref, cos_ref, sin_ref, kslc_ref, vslc_ref, kwin_ref, vwin_ref):
    cos2 = cos_ref[...]
    sin2 = sin_ref[...]
    for g in range(NSA_KV):
        def col(c):
            start = (c * NSA_KV + g) * NSA_HD
            return kv_ref[:, start:start + NSA_HD]
        kslc_ref[g] = _rope(col(2), cos2, sin2).astype(BF16)
        vslc_ref[g] = col(3).astype(BF16)
        kwin_ref[g] = _rope(col(4), cos2, sin2).astype(BF16)
        vwin_ref[g] = col(5).astype(BF16)


def kv_prep(kv, cos2, sin2, *, tm):
    s, w = kv.shape
    out = jax.ShapeDtypeStruct((NSA_KV, s, NSA_HD), BF16)
    ospec = pl.BlockSpec((NSA_KV, tm, NSA_HD), lambda i: (0, i, 0))
    return pl.pallas_call(
        _kv_prep_kernel,
        grid=(s // tm,),
        in_specs=[
            pl.BlockSpec((tm, w), lambda i: (i, 0)),
            pl.BlockSpec((tm, NSA_HD), lambda i: (i, 0)),
            pl.BlockSpec((tm, NSA_HD), lambda i: (i, 0)),
        ],
        out_specs=[ospec] * 4,
        out_shape=[out] * 4,
        compiler_params=_params("parallel"),
        name="kv_prep",
    )(kv, cos2, sin2)


def _compress_kernel(seg_ref, pos_ref, w1_ref, w2_ref, o_ref):
    n_seg = seg_ref.shape[0]
    first, second = None, None
    for tok in range(CMP_STRIDE):
        x = seg_ref[:, tok, :]
        for half in range(2):
            row = half * CMP_STRIDE + tok
            part = _dot((x + pos_ref[row:row + 1, :]).astype(BF16), w1_ref[row * NSA_HD:(row + 1) * NSA_HD, :])
            if half == 0:
                first = part if first is None else first + part
            else:
                second = part if second is None else second + part
    hidden = first + pltpu.roll(second, n_seg - 1, axis=0)
    act = jax.nn.gelu(hidden, approximate=True)
    o_ref[...] = _dot(act.astype(BF16), w2_ref[...])


def compress(kv, pos, w1, w2):
    s, w = kv.shape
    n_seg = s // CMP_STRIDE
    hid = w1.shape[2]
    wmap = lambda cg: (cg // NSA_KV, 0, 0)
    return pl.pallas_call(
        _compress_kernel,
        grid=(2 * NSA_KV,),
        in_specs=[
            pl.BlockSpec((n_seg, CMP_STRIDE, NSA_HD), lambda cg: (0, 0, cg)),
            pl.BlockSpec((None, CMP_LEN, NSA_HD), wmap),
            pl.BlockSpec((None, CMP_LEN * NSA_HD, hid), wmap),
            pl.BlockSpec((None, hid, NSA_HD), wmap),
        ],
        out_specs=pl.BlockSpec((None, n_seg, NSA_HD), lambda cg: (cg, 0, 0)),
        out_shape=jax.ShapeDtypeStruct((2 * NSA_KV, n_seg, NSA_HD), F32),
        compiler_params=_params("parallel"),
        name="compress",
    )(kv.reshape(n_seg, CMP_STRIDE, w), pos, w1, w2)


def _rope_t(x, cos2t, sin2t):
    half = NSA_HD // 2
    return x * cos2t + jnp.concatenate([x[half:], x[:half]], axis=0) * sin2t


def _col_max(chunks):
    return jnp.max(functools.reduce(jnp.maximum, chunks), axis=0, keepdims=True)


def _nsa_kernel(q_ref, cos_ref, sin_ref, gl_ref, gb_ref, kc_ref, vct_ref, kslc_ref, vslct_ref, kwin_ref, vwint_ref,
                o_ref, qt_ref, qr_ref, mask_ref, p_ref, acc_ref, s_ref, m_ref, ocmp_ref, owin_ref):
    qi = pl.program_id(1)
    qb = NSA_QB
    n_cmp = kc_ref.shape[0]
    n_sel = n_cmp // 4
    t0 = qi * qb
    t_row = t0 + lax.broadcasted_iota(jnp.int32, (1, qb), 1)
    hcols = [slice(j * qb, (j + 1) * qb) for j in range(NSA_GROUP)]

    cos2t = cos_ref[...]
    sin2t = sin_ref[...]
    for j in range(NSA_GROUP):
        qt = q_ref[:, j * NSA_HD:(j + 1) * NSA_HD].astype(F32).T * (NSA_HD ** -0.5 * LOG2E)
        qt_ref[:, hcols[j]] = qt.astype(BF16)
        qr_ref[:, hcols[j]] = _rope_t(qt, cos2t, sin2t).astype(BF16)

    def compressed_and_select(n_chunks):
        rows = 4 * SEL_CHUNK * n_chunks
        n_blk = SEL_CHUNK * n_chunks
        s_c = _dot(kc_ref[0:rows, :], qt_ref[...])
        blk_i = lax.broadcasted_iota(jnp.int32, (SEL_CHUNK, qb), 0)
        pieces = [(c, r) for c in range(n_chunks) for r in range(4)]

        def piece_rows(c, r):
            return slice((4 * c + r) * SEL_CHUNK, (4 * c + r + 1) * SEL_CHUNK)

        cmp_bias = {(c, r): jnp.where(SLC_LEN * (blk_i + SEL_CHUNK * c) + (CMP_STRIDE * r + CMP_LEN - 1) <= t_row,
                                      0.0, NEG) for c, r in pieces}
        has_valid = (t_row >= CMP_LEN - 1).astype(F32)
        imp = {cr: jnp.zeros((SEL_CHUNK, qb), F32) for cr in pieces}
        for j in range(NSA_GROUP):
            sb = {cr: s_c[piece_rows(*cr), hcols[j]] + cmp_bias[cr] for cr in pieces}
            m = _col_max(list(sb.values()))
            e = {cr: jnp.exp2(sb[cr] - m) for cr in pieces}
            den = functools.reduce(jnp.add, [jnp.sum(x, axis=0, keepdims=True) for x in e.values()])
            rinv = has_valid / jnp.maximum(den, TINY)
            for cr in pieces:
                p = e[cr] * rinv
                imp[cr] = imp[cr] + p
                p_ref[piece_rows(*cr), hcols[j]] = p.astype(BF16)
        ocmp_ref[...] = _dot(vct_ref[:, 0:rows], p_ref[0:rows, :])

        imp_r = [jnp.concatenate([imp[(c, r)] for c in range(n_chunks)], axis=0) for r in range(4)]
        blk = lax.broadcasted_iota(jnp.int32, (n_blk, qb), 0)
        imp3_prev = jnp.where(blk == 0, 0.0, pltpu.roll(imp_r[3], 1, axis=0))
        p_slc = ((((((imp_r[0] + imp3_prev) + imp_r[1]) + imp_r[0]) + imp_r[2]) + imp_r[1]) + imp_r[3]) + imp_r[2]
        cur = jnp.right_shift(t_row, SLC_LEN.bit_length() - 1)
        forced = (blk == 0) | (blk == cur) | (blk == cur - 1)
        score = jnp.where(blk > cur, NEG, jnp.where(forced, BIG, p_slc))
        blk_f = blk.astype(F32)
        sel = jnp.zeros((n_blk, qb), F32)
        for _ in range(SLC_TOPK):
            top = jnp.max(score, axis=0, keepdims=True)
            first = jnp.min(jnp.where(score == top, blk_f, float(n_blk)), axis=0, keepdims=True)
            hit = blk_f == first
            sel = jnp.where(hit, 1.0, sel)
            score = jnp.where(hit, REMOVED, score)
        unselected = ((sel - 1.0) * BIG).astype(BF16)
        for j in range(NSA_GROUP):
            mask_ref[0:n_blk, hcols[j]] = unselected
        if n_blk < n_sel:
            mask_ref[n_blk:n_sel, :] = jnp.full((n_sel - n_blk, NSA_GROUP * qb), -BIG, BF16)

    chunks_needed = (t0 + qb - 1) // (SLC_LEN * SEL_CHUNK) + 1
    for n_chunks in range(1, n_sel // SEL_CHUNK + 1):
        pl.when(chunks_needed == n_chunks)(functools.partial(compressed_and_select, n_chunks))

    acc_ref[...] = jnp.zeros_like(acc_ref)
    m_ref[...] = jnp.full_like(m_ref, REMOVED)
    tiles_per_window = LANES * SLC_LEN // SLC_TILE
    kpos0 = lax.broadcasted_iota(jnp.int32, (SLC_TILE, qb), 0)

    def scores(kt, slot):
        first = pl.multiple_of((kt // tiles_per_window) * LANES, LANES)
        weights_aug = jnp.concatenate([qr_ref[...], mask_ref[pl.ds(first, LANES), :]], axis=0)
        start = pl.multiple_of(kt * SLC_TILE, SLC_TILE)
        s_ref[slot] = _dot(kslc_ref[pl.ds(start, SLC_TILE), :], weights_aug)

    def weights(kt, slot, causal):
        if causal:
            future = jnp.where(kpos0 + kt * SLC_TILE <= t_row, 0.0, NEG)
        alpha = []
        for j in range(NSA_GROUP):
            s = s_ref[slot, :, hcols[j]]
            if causal:
                s = s + future
            m_old = m_ref[0:1, hcols[j]]
            m_j = jnp.maximum(m_old, jnp.max(s, axis=0, keepdims=True))
            p_ref[slot * SLC_TILE:(slot + 1) * SLC_TILE, hcols[j]] = jnp.exp2(s - m_j).astype(BF16)
            m_ref[0:1, hcols[j]] = m_j
            alpha.append(jnp.exp2(m_old - m_j))
        return jnp.concatenate(alpha, axis=1)

    def weighted_values(kt, slot):
        start = pl.multiple_of(kt * SLC_TILE, SLC_TILE)
        return _dot(vslct_ref[:, pl.ds(start, SLC_TILE)], p_ref[slot * SLC_TILE:(slot + 1) * SLC_TILE, :])

    def absorb(kt, slot, causal):
        alpha = weights(kt, slot, causal)
        acc_ref[...] = acc_ref[...] * alpha + weighted_values(kt, slot)

    last = (t0 + qb - 1) // SLC_TILE
    scores(0, 0)

    span = qb + WIN
    start = pl.multiple_of(jnp.maximum(t0 - WIN, 0), qb)
    s_w = _dot(kwin_ref[pl.ds(start, span), :], qr_ref[...])
    spos0 = lax.broadcasted_iota(jnp.int32, (qb, qb), 0)
    win_bias = []
    for c in range(span // qb):
        spos = spos0 + (start + c * qb)
        win_bias.append(jnp.where((spos <= t_row) & (spos > t_row - WIN), 0.0, NEG))
    for j in range(NSA_GROUP):
        sb = [s_w[c * qb:(c + 1) * qb, hcols[j]] + win_bias[c] for c in range(span // qb)]
        m = _col_max(sb)
        for c in range(span // qb):
            p_ref[c * qb:(c + 1) * qb, hcols[j]] = jnp.exp2(sb[c] - m).astype(BF16)
    pw = _dot(vwint_ref[:, pl.ds(start, span)], p_ref[0:span, :])
    owin_ref[...] = pw[:NSA_HD] / pw[NSA_HD:NSA_HD + 1]

    def pair(i, carry):
        scores(2 * i + 1, 1)
        absorb(2 * i, 0, False)
        scores(jnp.minimum(2 * i + 2, last), 0)
        absorb(2 * i + 1, 1, False)
        return carry

    def quad(i, carry):
        return pair(2 * i + 1, pair(2 * i, carry))

    def octet(i, carry):
        return quad(2 * i + 1, quad(2 * i, carry))

    lax.fori_loop(0, last // 8, octet, 0)
    lax.fori_loop(2 * (last // 8), last // 4, quad, 0)
    lax.fori_loop(2 * (last // 4), last // 2, pair, 0)

    @pl.when(last % 2 == 0)
    def _():
        absorb(last, 0, True)

    @pl.when(last % 2 == 1)
    def _():
        scores(last, 1)
        absorb(last - 1, 0, False)
        absorb(last, 1, True)

    acc = acc_ref[...]
    o_slc = acc[:NSA_HD] / acc[NSA_HD:NSA_HD + 1]

    gates_t = jax.nn.sigmoid(gl_ref[...] + gb_ref[...]).T
    for j in range(NSA_GROUP):
        o_t = (gates_t[3 * j:3 * j + 1] * ocmp_ref[:, hcols[j]]
               + gates_t[3 * j + 1:3 * j + 2] * o_slc[:, hcols[j]]
               + gates_t[3 * j + 2:3 * j + 3] * owin_ref[:, hcols[j]])
        o_ref[:, j * NSA_HD:(j + 1) * NSA_HD] = o_t.T


def nsa_attention(proj, cos2t, sin2t, gate_logits, gate_bias, kc, vct, kslc, vslct, kwin, vwint):
    s = proj.shape[0]
    n_cmp = kc.shape[1]
    rows_aug = NSA_HD + ONES_ROWS
    width = NSA_GROUP * NSA_QB
    qspec = pl.BlockSpec((NSA_QB, GROUP_W), lambda g, i: (i, g))
    tspec = pl.BlockSpec((NSA_HD, NSA_QB), lambda g, i: (0, i))

    def resident(shape):
        return pl.BlockSpec((None,) + shape, lambda g, i: (g, 0, 0), pipeline_mode=pl.Buffered(1))

    return pl.pallas_call(
        _nsa_kernel,
        grid=(NSA_KV, s // NSA_QB),
        in_specs=[
            qspec, tspec, tspec,
            pl.BlockSpec((NSA_QB, LANES), lambda g, i: (i, g)),
            pl.BlockSpec((1, LANES), lambda g, i: (0, g)),
            resident((n_cmp, NSA_HD)), resident((NSA_HD, n_cmp)),
            resident((s, NSA_HD + LANES)), resident((rows_aug, s)),
            resident((s, NSA_HD)), resident((rows_aug, s)),
        ],
        out_specs=qspec,
        out_shape=jax.ShapeDtypeStruct((s, NSA_KV * GROUP_W), F32),
        scratch_shapes=[
            pltpu.VMEM((NSA_HD, width), BF16),
            pltpu.VMEM((NSA_HD, width), BF16),
            pltpu.VMEM((n_cmp // 4, width), BF16),
            pltpu.VMEM((max(n_cmp, NSA_QB + WIN, 2 * SLC_TILE), width), BF16),
            pltpu.VMEM((rows_aug, width), F32),
            pltpu.VMEM((2, SLC_TILE, width), F32),
            pltpu.VMEM((8, width), F32),
            pltpu.VMEM((NSA_HD, width), F32),
            pltpu.VMEM((NSA_HD, width), F32),
        ],
        compiler_params=_params("parallel", "arbitrary"),
        name="nsa_attention",
    )(proj, cos2t, sin2t, gate_logits, gate_bias, kc, vct, kslc, vslct, kwin, vwint)


def _row_tile(s, want):
    return want if s % want == 0 else s


def _col_tile(n):
    return max(t for t in range(LANES, PROJ_COLS + 1, LANES) if n % t == 0)


def _rope_tables(seq):
    inv = ROPE_THETA ** (-jnp.arange(0, NSA_HD, 2, dtype=F32) / NSA_HD)
    ang = jnp.arange(seq, dtype=F32)[:, None] * inv[None, :]
    cos, sin = jnp.cos(ang), jnp.sin(ang)
    return jnp.concatenate([cos, cos], axis=1), jnp.concatenate([-sin, sin], axis=1)


def _pad_cols(a, width):
    return jnp.pad(a, ((0, 0), (0, width - a.shape[1])))


def _mem_kv(mem, gain, w_kv):
    kv = rms_matmul(mem, gain, w_kv.astype(BF16), tm=mem.shape[0], tn=MEM_W, out_dtype=BF16)
    return kv[:, :MEM_W].T, kv[:, MEM_W:]


def _mlstm_layer_heads(x, gain, w_in, gate_bias, head_norm):
    s = x.shape[0]
    qk_w = ML_HEADS * ML_DQK
    w_q, w_k, w_v, w_o, w_g, w_mq = jnp.split(
        w_in, [qk_w, 2 * qk_w, 2 * qk_w + MAIN_W, 2 * qk_w + 2 * MAIN_W, 2 * qk_w + 2 * MAIN_W + 2 * ML_HEADS], axis=1)

    def head_slots(w):
        w = w.reshape(w.shape[0], ML_HEADS, ML_DQK)
        return jnp.pad(w, ((0, 0), (0, 0), (0, ML_DQK_PAD - ML_DQK))).reshape(w.shape[0], ML_HEADS * ML_DQK_PAD)

    w_main = jnp.concatenate([w_v, w_o, head_slots(w_q), head_slots(w_k), w_mq], axis=1).astype(BF16)
    qk_pad_w = ML_HEADS * ML_DQK_PAD
    q_block = 2 * MAIN_W // qk_pad_w
    mq_block = (2 * MAIN_W + 2 * qk_pad_w) // MEM_W
    assert 2 * MAIN_W % qk_pad_w == 0 and (2 * MAIN_W + 2 * qk_pad_w) % MEM_W == 0
    proj, gates = rms_matmul(x, gain, w_main, _pad_cols(w_g, LANES).astype(BF16), tm=_row_tile(s, PROJ_ROWS),
                             tn=_col_tile(w_main.shape[1]), out_dtype=BF16)
    gates = gates[:, :2 * ML_HEADS]
    h = mlstm_heads(proj, q_block, q_block + 1, 0, 1, gates.T, gates, gate_bias, head_norm)
    return h, proj, mq_block


def _with_ones_rows(v):
    g, s, _ = v.shape
    return jnp.concatenate([v.transpose(0, 2, 1), jnp.ones((g, ONES_ROWS, s), v.dtype)], axis=1)


def _shared_kv(x, kv_norm, w_kv, cmp_pos, cmp_w1, cmp_w2, cos2, sin2):
    s = x.shape[0]
    n_seg = s // CMP_STRIDE
    n_sel = s // SLC_LEN
    kv = rms_matmul(x, kv_norm, w_kv.astype(BF16), tm=_row_tile(s, PROJ_ROWS), tn=_col_tile(w_kv.shape[1]))
    kslc, vslc, kwin, vwin = kv_prep(kv, cos2, sin2, tm=_row_tile(s, KV_PREP_ROWS))
    cmp = compress(kv, cmp_pos, cmp_w1.astype(BF16), cmp_w2.astype(BF16))
    cmp = cmp.reshape(2 * NSA_KV, n_sel // SEL_CHUNK, SEL_CHUNK, 4, NSA_HD).transpose(0, 1, 3, 2, 4)
    cmp = cmp.reshape(2 * NSA_KV, n_seg, NSA_HD)
    cmp = cmp.astype(BF16)
    block_in_window = (jnp.arange(s, dtype=jnp.int32) // SLC_LEN) % LANES
    onehot = (block_in_window[:, None] == jnp.arange(LANES, dtype=jnp.int32)[None, :]).astype(BF16)
    kslc_aug = jnp.concatenate([kslc, jnp.broadcast_to(onehot[None], (NSA_KV, s, LANES))], axis=2)
    return dict(kc=cmp[:NSA_KV], vct=cmp[NSA_KV:].transpose(0, 2, 1), kslc=kslc_aug, vslct=_with_ones_rows(vslc),
                kwin=kwin, vwint=_with_ones_rows(vwin))


def _nsa_layer_heads(x, gain, w_in, gate_bias, shared, cos2t, sin2t):
    s = x.shape[0]
    n_gate = 3 * NSA_GROUP
    w_q, w_g, w_mq = jnp.split(w_in, [MAIN_W, MAIN_W + 3 * NSA_HEADS], axis=1)
    w_main = jnp.concatenate([w_q, w_mq], axis=1).astype(BF16)
    w_gate = jnp.concatenate([_pad_cols(w_g[:, g * n_gate:(g + 1) * n_gate], LANES) for g in range(NSA_KV)], axis=1)
    b_gate = jnp.concatenate(
        [_pad_cols(gate_bias[None, g * n_gate:(g + 1) * n_gate], LANES) for g in range(NSA_KV)], axis=1)
    proj, gate_logits = rms_matmul(x, gain, w_main, w_gate.astype(BF16), tm=_row_tile(s, PROJ_ROWS),
                                   tn=_col_tile(w_main.shape[1]), out_dtype=BF16)
    o = nsa_attention(proj, cos2t, sin2t, gate_logits, b_gate, shared["kc"], shared["vct"], shared["kslc"],
                      shared["vslct"], shared["kwin"], shared["vwint"])
    return o, proj, MAIN_W // MEM_W


def kernel(x, mem, norm_gains, ffn_w_gate, ffn_w_up, ffn_w_down, mem_norm, mem_w_kv, w_out, a_w_in, a_gate_bias,
           a_head_norm, kv_norm, w_kv, cmp_pos, cmp_w1, cmp_w2, b_w_in, b_gate_bias):
    batch, seq, d_model = x.shape
    depth = norm_gains.shape[0]
    n_a = a_w_in.shape[0]
    assert seq % (SLC_LEN * LANES) == 0 and seq % ML_CHUNK == 0, "sequence must tile the selection-block lanes"
    cos2, sin2 = _rope_tables(seq)
    cos2t, sin2t = cos2.T, sin2.T
    w_gate16, w_up16, w_down16 = ffn_w_gate.astype(BF16), ffn_w_up.astype(BF16), ffn_w_down.astype(BF16)
    w_out16 = w_out.astype(BF16)

    def ffn(xb, layer, half, g_pre, g_post):
        return half_ffn(xb, g_pre, g_post, w_gate16, w_up16, w_down16, layer, half, tm=_row_tile(seq, FFN_ROWS),
                        tf=FFN_HIDDEN)

    outs = []
    for b in range(batch):
        xb = x[b]
        shared = None
        for layer in range(depth):
            if layer == n_a:
                shared = _shared_kv(xb, kv_norm, w_kv, cmp_pos, cmp_w1, cmp_w2, cos2, sin2)
            g = norm_gains[layer]
            xb = ffn(xb, layer, 0, g[0], g[1])
            mem_kt, mem_v = _mem_kv(mem[b], mem_norm[layer], mem_w_kv[layer])
            if layer < n_a:
                main, proj, mq_block = _mlstm_layer_heads(xb, g[2], a_w_in[layer], a_gate_bias[layer],
                                                          a_head_norm[layer])
            else:
                lb = layer - n_a
                main, proj, mq_block = _nsa_layer_heads(xb, g[2], b_w_in[lb], b_gate_bias[lb], shared, cos2t, sin2t)
            xb = out_proj(main, proj, mq_block, mem_kt, mem_v, xb, w_out16, layer, g[3],
                          tm=_row_tile(seq, OUT_PROJ_ROWS))
            xb = ffn(xb, layer, 1, g[4], g[5])
        outs.append(xb)
    return jnp.stack(outs, axis=0)
```

```python
import functools

import jax
import jax.numpy as jnp
from jax import lax
from jax.experimental import pallas as pl
from jax.experimental.pallas import tpu as pltpu

F32 = jnp.float32
BF16 = jnp.bfloat16

MEM_HEADS = 4
MEM_HD = 128
MEM_W = MEM_HEADS * MEM_HD
ML_HEADS = 4
ML_DV = 384
ML_DQK = 192
MAIN_W = ML_HEADS * ML_DV
NSA_HEADS = 12
NSA_HD = 128
NSA_KV = 2
NSA_GROUP = NSA_HEADS // NSA_KV
GROUP_W = NSA_GROUP * NSA_HD
CMP_LEN = 32
CMP_STRIDE = 16
SLC_LEN = 64
SLC_TOPK = 16
WIN = 512
FFN_RES = 0.5
ROPE_THETA = 10000.0
EPS = 1e-6
NEG = -1e30
BIG = 1e30
REMOVED = -3e38
LOG2E = 1.4426950408889634
TINY = 1e-30

LANES = 128
VMEM_LIMIT_BYTES = 56 * 1024 * 1024

PROJ_ROWS = 1024
PROJ_COLS = 1024
FFN_ROWS = 512
FFN_HIDDEN = 512
OUT_PROJ_ROWS = 512
KV_PREP_ROWS = 512

ML_CHUNK = 256
ML_DQK_PAD = 256
NSA_QB = 128
SLC_TILE = 1024
ONES_ROWS = 16
SEL_CHUNK = 64


def _params(*semantics):
    return pltpu.CompilerParams(dimension_semantics=semantics, vmem_limit_bytes=VMEM_LIMIT_BYTES)


def _rms(x, gain):
    return x * lax.rsqrt(jnp.mean(x * x, axis=-1, keepdims=True) + EPS) * gain


def _dot(a, b):
    return jnp.dot(a, b, preferred_element_type=F32)


def _exact_dot(a, b01):
    a1 = a.astype(BF16)
    r1 = a - a1.astype(F32)
    a2 = r1.astype(BF16)
    a3 = (r1 - a2.astype(F32)).astype(BF16)
    return _dot(a1, b01) + _dot(a2, b01) + _dot(a3, b01)


def _exact_dot_left(b01, a):
    a1 = a.astype(BF16)
    r1 = a - a1.astype(F32)
    a2 = r1.astype(BF16)
    a3 = (r1 - a2.astype(F32)).astype(BF16)
    return _dot(b01, a1) + _dot(b01, a2) + _dot(b01, a3)


def _rms_matmul_kernel(*refs, has_side):
    if has_side:
        x_ref, g_ref, w_ref, wside_ref, o_ref, oside_ref, xn_ref = refs
    else:
        x_ref, g_ref, w_ref, o_ref, xn_ref = refs

    @pl.when(pl.program_id(1) == 0)
    def _():
        xn_ref[...] = _rms(x_ref[...], g_ref[...]).astype(BF16)
        if has_side:
            oside_ref[...] = _dot(xn_ref[...], wside_ref[...])

    o_ref[...] = _dot(xn_ref[...], w_ref[...]).astype(o_ref.dtype)


def rms_matmul(x, gain, w, w_side=None, *, tm, tn, out_dtype=F32):
    s, k = x.shape
    n = w.shape[1]
    in_specs = [
        pl.BlockSpec((tm, k), lambda i, j: (i, 0)),
        pl.BlockSpec((1, k), lambda i, j: (0, 0)),
        pl.BlockSpec((k, tn), lambda i, j: (0, j)),
    ]
    out_specs = pl.BlockSpec((tm, tn), lambda i, j: (i, j))
    out_shape = jax.ShapeDtypeStruct((s, n), out_dtype)
    args = (x, gain.reshape(1, k), w)
    if w_side is not None:
        n_side = w_side.shape[1]
        in_specs.append(pl.BlockSpec((k, n_side), lambda i, j: (0, 0)))
        out_specs = [out_specs, pl.BlockSpec((tm, n_side), lambda i, j: (i, 0))]
        out_shape = [out_shape, jax.ShapeDtypeStruct((s, n_side), F32)]
        args = args + (w_side,)
    return pl.pallas_call(
        functools.partial(_rms_matmul_kernel, has_side=w_side is not None),
        grid=(s // tm, n // tn),
        in_specs=in_specs,
        out_specs=out_specs,
        out_shape=out_shape,
        scratch_shapes=[pltpu.VMEM((tm, k), BF16)],
        compiler_params=_params("parallel", "arbitrary"),
        name="rms_matmul",
    )(*args)


def _ffn_kernel(x_ref, gpre_ref, gpost_ref, wg_ref, wu_ref, wd_ref, o_ref, xn_ref):
    j = pl.program_id(1)

    @pl.when(j == 0)
    def _():
        xn_ref[...] = _rms(x_ref[...], gpre_ref[...]).astype(BF16)
        o_ref[...] = jnp.zeros_like(o_ref)

    xn = xn_ref[...]
    half = wg_ref.shape[1] // 2
    gates, ups = [], []
    for h in range(2):
        gates.append(_dot(xn, wg_ref[:, h * half:(h + 1) * half]))
        ups.append(_dot(xn, wu_ref[:, h * half:(h + 1) * half]))
    down = None
    for h in range(2):
        hidden = (gates[h] * jax.nn.sigmoid(gates[h]) * ups[h]).astype(BF16)
        part = _dot(hidden, wd_ref[h * half:(h + 1) * half, :])
        down = part if down is None else down + part
    o_ref[...] += down

    @pl.when(j == pl.num_programs(1) - 1)
    def _():
        o_ref[...] = x_ref[...] + FFN_RES * _rms(o_ref[...], gpost_ref[...])


def half_ffn(x, g_pre, g_post, w_gate, w_up, w_down, layer, half, *, tm, tf):
    s, d = x.shape
    d_ff = w_gate.shape[3]
    return pl.pallas_call(
        _ffn_kernel,
        grid=(s // tm, d_ff // tf),
        in_specs=[
            pl.BlockSpec((tm, d), lambda i, j: (i, 0)),
            pl.BlockSpec((1, d), lambda i, j: (0, 0)),
            pl.BlockSpec((1, d), lambda i, j: (0, 0)),
            pl.BlockSpec((None, None, d, tf), lambda i, j: (layer, half, 0, j)),
            pl.BlockSpec((None, None, d, tf), lambda i, j: (layer, half, 0, j)),
            pl.BlockSpec((None, None, tf, d), lambda i, j: (layer, half, j, 0)),
        ],
        out_specs=pl.BlockSpec((tm, d), lambda i, j: (i, 0)),
        out_shape=jax.ShapeDtypeStruct((s, d), F32),
        scratch_shapes=[pltpu.VMEM((tm, d), BF16)],
        compiler_params=_params("parallel", "arbitrary"),
        name="half_ffn",
    )(x, g_pre.reshape(1, d), g_post.reshape(1, d), w_gate, w_up, w_down)


def _out_proj_kernel(h_ref, mq_ref, kt_ref, v_ref, x_ref, wh_ref, wm_ref, g_ref, o_ref):
    scale = MEM_HD ** -0.5
    heads = []
    for h in range(MEM_HEADS):
        cols = slice(h * MEM_HD, (h + 1) * MEM_HD)
        s = _dot(mq_ref[:, cols], kt_ref[cols, :]) * scale
        e = jnp.exp(s - jnp.max(s, axis=-1, keepdims=True))
        p = e / jnp.sum(e, axis=-1, keepdims=True)
        heads.append(_dot(p.astype(BF16), v_ref[:, cols]).astype(BF16))
    y = _dot(h_ref[...].astype(BF16), wh_ref[...]) + _dot(jnp.concatenate(heads, axis=1), wm_ref[...])
    o_ref[...] = x_ref[...] + _rms(y, g_ref[...])


def out_proj(main, proj, mq_block, mem_kt, mem_v, x, w, layer, gain, *, tm):
    s, d = x.shape
    km = main.shape[1]
    m = mem_v.shape[0]
    assert km % MEM_W == 0
    return pl.pallas_call(
        _out_proj_kernel,
        grid=(s // tm,),
        in_specs=[
            pl.BlockSpec((tm, km), lambda i: (i, 0)),
            pl.BlockSpec((tm, MEM_W), lambda i: (i, mq_block)),
            pl.BlockSpec((MEM_W, m), lambda i: (0, 0)),
            pl.BlockSpec((m, MEM_W), lambda i: (0, 0)),
            pl.BlockSpec((tm, d), lambda i: (i, 0)),
            pl.BlockSpec((None, km, d), lambda i: (layer, 0, 0)),
            pl.BlockSpec((None, MEM_W, d), lambda i: (layer, km // MEM_W, 0)),
            pl.BlockSpec((1, d), lambda i: (0, 0)),
        ],
        out_specs=pl.BlockSpec((tm, d), lambda i: (i, 0)),
        out_shape=jax.ShapeDtypeStruct((s, d), F32),
        compiler_params=_params("parallel"),
        name="out_proj",
    )(main, proj, mem_kt, mem_v, x, w, w, gain.reshape(1, d))


def _log_sigmoid(x):
    return jnp.minimum(x, 0.0) - jnp.log1p(jnp.exp(-jnp.abs(x)))


def _mlstm_kernel(q_ref, k_ref, v_ref, o_ref, grow_ref, gcol_ref, brow_ref, bcol_ref, hn_ref,
                  out_ref, c_ref, m_ref):
    chunk = grow_ref.shape[1]

    @pl.when(pl.program_id(0) == 0)
    def _():
        c_ref[...] = jnp.zeros_like(c_ref)
        m_ref[...] = jnp.zeros_like(m_ref)

    grow = grow_ref[...] + brow_ref[...]
    gcol = gcol_ref[...] + bcol_ref[...]
    r_idx = lax.broadcasted_iota(jnp.int32, (chunk, chunk), 0)
    c_idx = lax.broadcasted_iota(jnp.int32, (chunk, chunk), 1)
    causal = c_idx <= r_idx
    b_rows = _exact_dot(_log_sigmoid(grow), (r_idx <= c_idx).astype(BF16))
    b_cols = _exact_dot_left(causal.astype(BF16), _log_sigmoid(gcol))
    ones_col = (lax.broadcasted_iota(jnp.int32, (chunk, LANES), 1) == 0).astype(BF16)

    for h in range(ML_HEADS):
        b_r = b_rows[ML_HEADS + h:ML_HEADS + h + 1, :]
        i_r = grow[h:h + 1, :]
        b_c = b_cols[:, ML_HEADS + h:ML_HEADS + h + 1]
        m_prev = m_ref[h:h + 1, 0:1]
        dmat = jnp.where(causal, b_c - b_r + i_r, -jnp.inf)
        m_inter = b_c + m_prev
        m_vec = jnp.maximum(jnp.max(dmat, axis=-1, keepdims=True), m_inter)
        qk_cols = slice(h * ML_DQK_PAD, (h + 1) * ML_DQK_PAD)
        q = q_ref[:, qk_cols]
        kt = (k_ref[:, qk_cols].astype(F32) * (ML_DQK ** -0.5)).T
        smat = _dot(q, kt.astype(BF16)) * jnp.exp(dmat - m_vec)
        inter = jnp.exp(m_inter - m_vec)
        vcols = slice(h * ML_DV, (h + 1) * ML_DV)
        v_aug = jnp.concatenate([v_ref[:, vcols], ones_col], axis=1)
        c_prev = c_ref[h]
        num_aug = _dot(smat.astype(BF16), v_aug) + inter * _dot(q, c_prev.astype(BF16))
        den = num_aug[:, ML_DV:ML_DV + 1]
        hval = num_aug[:, :ML_DV] / jnp.maximum(jnp.abs(den), jnp.exp(-m_vec))
        m_last = m_vec[chunk - 1:chunk, :]
        b_last = b_c[chunk - 1:chunk, :]
        w_r = jnp.exp(b_last - b_r + i_r - m_last)
        decay = jnp.exp(b_last + m_prev - m_last)
        c_ref[h] = decay * c_prev + _dot((kt * w_r).astype(BF16), v_aug)
        m_ref[h:h + 1, :] = jnp.broadcast_to(m_last, (1, LANES))
        out_ref[:, vcols] = _rms(hval, hn_ref[:, vcols]) * jax.nn.sigmoid(o_ref[:, vcols].astype(F32))


def mlstm_heads(proj, q_block, k_block, v_block, o_block, gates_row, gates_col, bias, head_norm):
    s = proj.shape[0]
    chunk = ML_CHUNK
    h2 = 2 * ML_HEADS
    qk_w = ML_HEADS * ML_DQK_PAD
    return pl.pallas_call(
        _mlstm_kernel,
        grid=(s // chunk,),
        in_specs=[
            pl.BlockSpec((chunk, qk_w), lambda c: (c, q_block)),
            pl.BlockSpec((chunk, qk_w), lambda c: (c, k_block)),
            pl.BlockSpec((chunk, MAIN_W), lambda c: (c, v_block)),
            pl.BlockSpec((chunk, MAIN_W), lambda c: (c, o_block)),
            pl.BlockSpec((h2, chunk), lambda c: (0, c)),
            pl.BlockSpec((chunk, h2), lambda c: (c, 0)),
            pl.BlockSpec((h2, 1), lambda c: (0, 0)),
            pl.BlockSpec((1, h2), lambda c: (0, 0)),
            pl.BlockSpec((1, MAIN_W), lambda c: (0, 0)),
        ],
        out_specs=pl.BlockSpec((chunk, MAIN_W), lambda c: (c, 0)),
        out_shape=jax.ShapeDtypeStruct((s, MAIN_W), F32),
        scratch_shapes=[pltpu.VMEM((ML_HEADS, ML_DQK_PAD, ML_DV + LANES), F32), pltpu.VMEM((8, LANES), F32)],
        compiler_params=_params("arbitrary"),
        name="mlstm",
    )(proj, proj, proj, proj, gates_row, gates_col, bias.reshape(h2, 1), bias.reshape(1, h2),
      head_norm.reshape(1, MAIN_W))


def _rope(x, cos2, sin2):
    return x * cos2 + pltpu.roll(x, NSA_HD // 2, axis=1) * sin2


def _kv_prep_kernel(kv_ref, cos_ref, sin_ref, kslc_ref, vslc_ref, kwin_ref, vwin_ref):
    cos2 = cos_ref[...]
    sin2 = sin_ref[...]
    for g in range(NSA_KV):
        def col(c):
            start = (c * NSA_KV + g) * NSA_HD
            return kv_ref[:, start:start + NSA_HD]
        kslc_ref[g] = _rope(col(2), cos2, sin2).astype(BF16)
        vslc_ref[g] = col(3).astype(BF16)
        kwin_ref[g] = _rope(col(4), cos2, sin2).astype(BF16)
        vwin_ref[g] = col(5).astype(BF16)


def kv_prep(kv, cos2, sin2, *, tm):
    s, w = kv.shape
    out = jax.ShapeDtypeStruct((NSA_KV, s, NSA_HD), BF16)
    ospec = pl.BlockSpec((NSA_KV, tm, NSA_HD), lambda i: (0, i, 0))
    return pl.pallas_call(
        _kv_prep_kernel,
        grid=(s // tm,),
        in_specs=[
            pl.BlockSpec((tm, w), lambda i: (i, 0)),
            pl.BlockSpec((tm, NSA_HD), lambda i: (i, 0)),
            pl.BlockSpec((tm, NSA_HD), lambda i: (i, 0)),
        ],
        out_specs=[ospec] * 4,
        out_shape=[out] * 4,
        compiler_params=_params("parallel"),
        name="kv_prep",
    )(kv, cos2, sin2)


def _compress_kernel(seg_ref, pos_ref, w1_ref, w2_ref, o_ref):
    n_seg = seg_ref.shape[0]
    first, second = None, None
    for tok in range(CMP_STRIDE):
        x = seg_ref[:, tok, :]
        for half in range(2):
            row = half * CMP_STRIDE + tok
            part = _dot((x + pos_ref[row:row + 1, :]).astype(BF16), w1_ref[row * NSA_HD:(row + 1) * NSA_HD, :])
            if half == 0:
                first = part if first is None else first + part
            else:
                second = part if second is None else second + part
    hidden = first + pltpu.roll(second, n_seg - 1, axis=0)
    act = jax.nn.gelu(hidden, approximate=True)
    o_ref[...] = _dot(act.astype(BF16), w2_ref[...])


def compress(kv, pos, w1, w2):
    s, w = kv.shape
    n_seg = s // CMP_STRIDE
    hid = w1.shape[2]
    wmap = lambda cg: (cg // NSA_KV, 0, 0)
    return pl.pallas_call(
        _compress_kernel,
        grid=(2 * NSA_KV,),
        in_specs=[
            pl.BlockSpec((n_seg, CMP_STRIDE, NSA_HD), lambda cg: (0, 0, cg)),
            pl.BlockSpec((None, CMP_LEN, NSA_HD), wmap),
            pl.BlockSpec((None, CMP_LEN * NSA_HD, hid), wmap),
            pl.BlockSpec((None, hid, NSA_HD), wmap),
        ],
        out_specs=pl.BlockSpec((None, n_seg, NSA_HD), lambda cg: (cg, 0, 0)),
        out_shape=jax.ShapeDtypeStruct((2 * NSA_KV, n_seg, NSA_HD), F32),
        compiler_params=_params("parallel"),
        name="compress",
    )(kv.reshape(n_seg, CMP_STRIDE, w), pos, w1, w2)


def _rope_t(x, cos2t, sin2t):
    half = NSA_HD // 2
    return x * cos2t + jnp.concatenate([x[half:], x[:half]], axis=0) * sin2t


def _col_max(chunks):
    return jnp.max(functools.reduce(jnp.maximum, chunks), axis=0, keepdims=True)


def _nsa_kernel(q_ref, cos_ref, sin_ref, gl_ref, gb_ref, kc_ref, vct_ref, kslc_ref, vslct_ref, kwin_ref, vwint_ref,
                o_ref, qt_ref, qr_ref, mask_ref, p_ref, acc_ref, s_ref, m_ref, ocmp_ref, owin_ref):
    qi = pl.program_id(1)
    qb = NSA_QB
    n_cmp = kc_ref.shape[0]
    n_sel = n_cmp // 4
    t0 = qi * qb
    t_row = t0 + lax.broadcasted_iota(jnp.int32, (1, qb), 1)
    hcols = [slice(j * qb, (j + 1) * qb) for j in range(NSA_GROUP)]

    cos2t = cos_ref[...]
    sin2t = sin_ref[...]
    for j in range(NSA_GROUP):
        qt = q_ref[:, j * NSA_HD:(j + 1) * NSA_HD].astype(F32).T * (NSA_HD ** -0.5 * LOG2E)
        qt_ref[:, hcols[j]] = qt.astype(BF16)
        qr_ref[:, hcols[j]] = _rope_t(qt, cos2t, sin2t).astype(BF16)

    def compressed_and_select(n_chunks):
        rows = 4 * SEL_CHUNK * n_chunks
        n_blk = SEL_CHUNK * n_chunks
        s_pair = [_dot(kc_ref[0:rows, :], qt_ref[:, 2 * pp * qb:2 * (pp + 1) * qb]) for pp in range(NSA_GROUP // 2)]
        blk_i = lax.broadcasted_iota(jnp.int32, (SEL_CHUNK, qb), 0)
        pieces = [(c, r) for c in range(n_chunks) for r in range(4)]

        def piece_rows(c, r):
            return slice((4 * c + r) * SEL_CHUNK, (4 * c + r + 1) * SEL_CHUNK)

        cmp_bias = {(c, r): jnp.where(SLC_LEN * (blk_i + SEL_CHUNK * c) + (CMP_STRIDE * r + CMP_LEN - 1) <= t_row,
                                      0.0, NEG) for c, r in pieces}
        has_valid = (t_row >= CMP_LEN - 1).astype(F32)
        imp = {cr: jnp.zeros((SEL_CHUNK, qb), F32) for cr in pieces}
        for j in range(NSA_GROUP):
            s_j = s_pair[j // 2][:, (j % 2) * qb:(j % 2 + 1) * qb]
            sb = {cr: s_j[piece_rows(*cr), :] + cmp_bias[cr] for cr in pieces}
            m = _col_max(list(sb.values()))
            e = {cr: jnp.exp2(sb[cr] - m) for cr in pieces}
            den = functools.reduce(jnp.add, [jnp.sum(x, axis=0, keepdims=True) for x in e.values()])
            rinv = has_valid / jnp.maximum(den, TINY)
            for cr in pieces:
                p = e[cr] * rinv
                imp[cr] = imp[cr] + p
                p_ref[piece_rows(*cr), hcols[j]] = p.astype(BF16)
        ocmp_ref[...] = _dot(vct_ref[:, 0:rows], p_ref[0:rows, :])

        imp_r = [jnp.concatenate([imp[(c, r)] for c in range(n_chunks)], axis=0) for r in range(4)]
        blk = lax.broadcasted_iota(jnp.int32, (n_blk, qb), 0)
        imp3_prev = jnp.where(blk == 0, 0.0, pltpu.roll(imp_r[3], 1, axis=0))
        p_slc = ((((((imp_r[0] + imp3_prev) + imp_r[1]) + imp_r[0]) + imp_r[2]) + imp_r[1]) + imp_r[3]) + imp_r[2]
        cur = jnp.right_shift(t_row, SLC_LEN.bit_length() - 1)
        forced = (blk == 0) | (blk == cur) | (blk == cur - 1)
        score = jnp.where(blk > cur, NEG, jnp.where(forced, BIG, p_slc))
        blk_f = blk.astype(F32)
        sel = jnp.zeros((n_blk, qb), F32)
        for _ in range(SLC_TOPK):
            top = jnp.max(score, axis=0, keepdims=True)
            first = jnp.min(jnp.where(score == top, blk_f, float(n_blk)), axis=0, keepdims=True)
            hit = blk_f == first
            sel = jnp.where(hit, 1.0, sel)
            score = jnp.where(hit, REMOVED, score)
        unselected = ((sel - 1.0) * BIG).astype(BF16)
        for j in range(NSA_GROUP):
            mask_ref[0:n_blk, hcols[j]] = unselected
        if n_blk < n_sel:
            mask_ref[n_blk:n_sel, :] = jnp.full((n_sel - n_blk, NSA_GROUP * qb), -BIG, BF16)

    chunks_needed = (t0 + qb - 1) // (SLC_LEN * SEL_CHUNK) + 1
    for n_chunks in range(1, n_sel // SEL_CHUNK + 1):
        pl.when(chunks_needed == n_chunks)(functools.partial(compressed_and_select, n_chunks))

    acc_ref[...] = jnp.zeros_like(acc_ref)
    m_ref[...] = jnp.full_like(m_ref, REMOVED)
    tiles_per_window = LANES * SLC_LEN // SLC_TILE
    kpos0 = lax.broadcasted_iota(jnp.int32, (SLC_TILE, qb), 0)

    def scores(kt, slot):
        first = pl.multiple_of((kt // tiles_per_window) * LANES, LANES)
        weights_aug = jnp.concatenate([qr_ref[...], mask_ref[pl.ds(first, LANES), :]], axis=0)
        start = pl.multiple_of(kt * SLC_TILE, SLC_TILE)
        s_ref[slot] = _dot(kslc_ref[pl.ds(start, SLC_TILE), :], weights_aug)

    def weights(kt, slot, causal):
        if causal:
            future = jnp.where(kpos0 + kt * SLC_TILE <= t_row, 0.0, NEG)
        alpha = []
        for j in range(NSA_GROUP):
            s = s_ref[slot, :, hcols[j]]
            if causal:
                s = s + future
            m_old = m_ref[0:1, hcols[j]]
            m_j = jnp.maximum(m_old, jnp.max(s, axis=0, keepdims=True))
            p_ref[slot * SLC_TILE:(slot + 1) * SLC_TILE, hcols[j]] = jnp.exp2(s - m_j).astype(BF16)
            m_ref[0:1, hcols[j]] = m_j
            alpha.append(jnp.exp2(m_old - m_j))
        return jnp.concatenate(alpha, axis=1)

    def weighted_values(kt, slot):
        start = pl.multiple_of(kt * SLC_TILE, SLC_TILE)
        return _dot(vslct_ref[:, pl.ds(start, SLC_TILE)], p_ref[slot * SLC_TILE:(slot + 1) * SLC_TILE, :])

    def absorb(kt, slot, causal):
        alpha = weights(kt, slot, causal)
        acc_ref[...] = acc_ref[...] * alpha + weighted_values(kt, slot)

    last = (t0 + qb - 1) // SLC_TILE
    scores(0, 0)

    span = qb + WIN
    start = pl.multiple_of(jnp.maximum(t0 - WIN, 0), qb)
    k_w = kwin_ref[pl.ds(start, span), :]
    w_pair = [_dot(k_w, qr_ref[:, 2 * pp * qb:2 * (pp + 1) * qb]) for pp in range(NSA_GROUP // 2)]
    spos0 = lax.broadcasted_iota(jnp.int32, (qb, qb), 0)
    win_bias = []
    for c in range(span // qb):
        spos = spos0 + (start + c * qb)
        win_bias.append(jnp.where((spos <= t_row) & (spos > t_row - WIN), 0.0, NEG))
    for j in range(NSA_GROUP):
        s_j = w_pair[j // 2][:, (j % 2) * qb:(j % 2 + 1) * qb]
        sb = [s_j[c * qb:(c + 1) * qb, :] + win_bias[c] for c in range(span // qb)]
        m = _col_max(sb)
        for c in range(span // qb):
            p_ref[c * qb:(c + 1) * qb, hcols[j]] = jnp.exp2(sb[c] - m).astype(BF16)
    pw = _dot(vwint_ref[:, pl.ds(start, span)], p_ref[0:span, :])
    owin_ref[...] = pw[:NSA_HD] / pw[NSA_HD:NSA_HD + 1]

    def pair(i, carry):
        scores(2 * i + 1, 1)
        absorb(2 * i, 0, False)
        scores(jnp.minimum(2 * i + 2, last), 0)
        absorb(2 * i + 1, 1, False)
        return carry

    def quad(i, carry):
        return pair(2 * i + 1, pair(2 * i, carry))

    def octet(i, carry):
        return quad(2 * i + 1, quad(2 * i, carry))

    lax.fori_loop(0, last // 8, octet, 0)
    lax.fori_loop(2 * (last // 8), last // 4, quad, 0)
    lax.fori_loop(2 * (last // 4), last // 2, pair, 0)

    @pl.when(last % 2 == 0)
    def _():
        absorb(last, 0, True)

    @pl.when(last % 2 == 1)
    def _():
        scores(last, 1)
        absorb(last - 1, 0, False)
        absorb(last, 1, True)

    acc = acc_ref[...]
    o_slc = acc[:NSA_HD] / acc[NSA_HD:NSA_HD + 1]

    gates_t = jax.nn.sigmoid(gl_ref[...] + gb_ref[...]).T
    for j in range(NSA_GROUP):
        o_t = (gates_t[3 * j:3 * j + 1] * ocmp_ref[:, hcols[j]]
               + gates_t[3 * j + 1:3 * j + 2] * o_slc[:, hcols[j]]
               + gates_t[3 * j + 2:3 * j + 3] * owin_ref[:, hcols[j]])
        o_ref[:, j * NSA_HD:(j + 1) * NSA_HD] = o_t.T


def nsa_attention(proj, cos2t, sin2t, gate_logits, gate_bias, kc, vct, kslc, vslct, kwin, vwint):
    s = proj.shape[0]
    n_cmp = kc.shape[1]
    rows_aug = NSA_HD + ONES_ROWS
    width = NSA_GROUP * NSA_QB
    qspec = pl.BlockSpec((NSA_QB, GROUP_W), lambda g, i: (i, g))
    tspec = pl.BlockSpec((NSA_HD, NSA_QB), lambda g, i: (0, i))

    def resident(shape):
        return pl.BlockSpec((None,) + shape, lambda g, i: (g, 0, 0), pipeline_mode=pl.Buffered(1))

    return pl.pallas_call(
        _nsa_kernel,
        grid=(NSA_KV, s // NSA_QB),
        in_specs=[
            qspec, tspec, tspec,
            pl.BlockSpec((NSA_QB, LANES), lambda g, i: (i, g)),
            pl.BlockSpec((1, LANES), lambda g, i: (0, g)),
            resident((n_cmp, NSA_HD)), resident((NSA_HD, n_cmp)),
            resident((s, NSA_HD + LANES)), resident((rows_aug, s)),
            resident((s, NSA_HD)), resident((rows_aug, s)),
        ],
        out_specs=qspec,
        out_shape=jax.ShapeDtypeStruct((s, NSA_KV * GROUP_W), F32),
        scratch_shapes=[
            pltpu.VMEM((NSA_HD, width), BF16),
            pltpu.VMEM((NSA_HD, width), BF16),
            pltpu.VMEM((n_cmp // 4, width), BF16),
            pltpu.VMEM((max(n_cmp, NSA_QB + WIN, 2 * SLC_TILE), width), BF16),
            pltpu.VMEM((rows_aug, width), F32),
            pltpu.VMEM((2, SLC_TILE, width), F32),
            pltpu.VMEM((8, width), F32),
            pltpu.VMEM((NSA_HD, width), F32),
            pltpu.VMEM((NSA_HD, width), F32),
        ],
        compiler_params=_params("parallel", "arbitrary"),
        name="nsa_attention",
    )(proj, cos2t, sin2t, gate_logits, gate_bias, kc, vct, kslc, vslct, kwin, vwint)


def _row_tile(s, want):
    return want if s % want == 0 else s


def _col_tile(n):
    return max(t for t in range(LANES, PROJ_COLS + 1, LANES) if n % t == 0)


def _rope_tables(seq):
    inv = ROPE_THETA ** (-jnp.arange(0, NSA_HD, 2, dtype=F32) / NSA_HD)
    ang = jnp.arange(seq, dtype=F32)[:, None] * inv[None, :]
    cos, sin = jnp.cos(ang), jnp.sin(ang)
    return jnp.concatenate([cos, cos], axis=1), jnp.concatenate([-sin, sin], axis=1)


def _pad_cols(a, width):
    return jnp.pad(a, ((0, 0), (0, width - a.shape[1])))


def _mem_kv(mem, gain, w_kv):
    kv = rms_matmul(mem, gain, w_kv.astype(BF16), tm=mem.shape[0], tn=MEM_W, out_dtype=BF16)
    return kv[:, :MEM_W].T, kv[:, MEM_W:]


def _mlstm_layer_heads(x, gain, w_in, gate_bias, head_norm):
    s = x.shape[0]
    qk_w = ML_HEADS * ML_DQK
    w_q, w_k, w_v, w_o, w_g, w_mq = jnp.split(
        w_in, [qk_w, 2 * qk_w, 2 * qk_w + MAIN_W, 2 * qk_w + 2 * MAIN_W, 2 * qk_w + 2 * MAIN_W + 2 * ML_HEADS], axis=1)

    def head_slots(w):
        w = w.reshape(w.shape[0], ML_HEADS, ML_DQK)
        return jnp.pad(w, ((0, 0), (0, 0), (0, ML_DQK_PAD - ML_DQK))).reshape(w.shape[0], ML_HEADS * ML_DQK_PAD)

    w_main = jnp.concatenate([w_v, w_o, head_slots(w_q), head_slots(w_k), w_mq], axis=1).astype(BF16)
    qk_pad_w = ML_HEADS * ML_DQK_PAD
    q_block = 2 * MAIN_W // qk_pad_w
    mq_block = (2 * MAIN_W + 2 * qk_pad_w) // MEM_W
    assert 2 * MAIN_W % qk_pad_w == 0 and (2 * MAIN_W + 2 * qk_pad_w) % MEM_W == 0
    proj, gates = rms_matmul(x, gain, w_main, _pad_cols(w_g, LANES).astype(BF16), tm=_row_tile(s, PROJ_ROWS),
                             tn=_col_tile(w_main.shape[1]), out_dtype=BF16)
    gates = gates[:, :2 * ML_HEADS]
    h = mlstm_heads(proj, q_block, q_block + 1, 0, 1, gates.T, gates, gate_bias, head_norm)
    return h, proj, mq_block


def _with_ones_rows(v):
    g, s, _ = v.shape
    return jnp.concatenate([v.transpose(0, 2, 1), jnp.ones((g, ONES_ROWS, s), v.dtype)], axis=1)


def _shared_kv(x, kv_norm, w_kv, cmp_pos, cmp_w1, cmp_w2, cos2, sin2):
    s = x.shape[0]
    n_seg = s // CMP_STRIDE
    n_sel = s // SLC_LEN
    kv = rms_matmul(x, kv_norm, w_kv.astype(BF16), tm=_row_tile(s, PROJ_ROWS), tn=_col_tile(w_kv.shape[1]))
    kslc, vslc, kwin, vwin = kv_prep(kv, cos2, sin2, tm=_row_tile(s, KV_PREP_ROWS))
    cmp = compress(kv, cmp_pos, cmp_w1.astype(BF16), cmp_w2.astype(BF16))
    cmp = cmp.reshape(2 * NSA_KV, n_sel // SEL_CHUNK, SEL_CHUNK, 4, NSA_HD).transpose(0, 1, 3, 2, 4)
    cmp = cmp.reshape(2 * NSA_KV, n_seg, NSA_HD)
    cmp = cmp.astype(BF16)
    block_in_window = (jnp.arange(s, dtype=jnp.int32) // SLC_LEN) % LANES
    onehot = (block_in_window[:, None] == jnp.arange(LANES, dtype=jnp.int32)[None, :]).astype(BF16)
    kslc_aug = jnp.concatenate([kslc, jnp.broadcast_to(onehot[None], (NSA_KV, s, LANES))], axis=2)
    return dict(kc=cmp[:NSA_KV], vct=cmp[NSA_KV:].transpose(0, 2, 1), kslc=kslc_aug, vslct=_with_ones_rows(vslc),
                kwin=kwin, vwint=_with_ones_rows(vwin))


def _nsa_layer_heads(x, gain, w_in, gate_bias, shared, cos2t, sin2t):
    s = x.shape[0]
    n_gate = 3 * NSA_GROUP
    w_q, w_g, w_mq = jnp.split(w_in, [MAIN_W, MAIN_W + 3 * NSA_HEADS], axis=1)
    w_main = jnp.concatenate([w_q, w_mq], axis=1).astype(BF16)
    w_gate = jnp.concatenate([_pad_cols(w_g[:, g * n_gate:(g + 1) * n_gate], LANES) for g in range(NSA_KV)], axis=1)
    b_gate = jnp.concatenate(
        [_pad_cols(gate_bias[None, g * n_gate:(g + 1) * n_gate], LANES) for g in range(NSA_KV)], axis=1)
    proj, gate_logits = rms_matmul(x, gain, w_main, w_gate.astype(BF16), tm=_row_tile(s, PROJ_ROWS),
                                   tn=_col_tile(w_main.shape[1]), out_dtype=BF16)
    o = nsa_attention(proj, cos2t, sin2t, gate_logits, b_gate, shared["kc"], shared["vct"], shared["kslc"],
                      shared["vslct"], shared["kwin"], shared["vwint"])
    return o, proj, MAIN_W // MEM_W


def kernel(x, mem, norm_gains, ffn_w_gate, ffn_w_up, ffn_w_down, mem_norm, mem_w_kv, w_out, a_w_in, a_gate_bias,
           a_head_norm, kv_norm, w_kv, cmp_pos, cmp_w1, cmp_w2, b_w_in, b_gate_bias):
    batch, seq, d_model = x.shape
    depth = norm_gains.shape[0]
    n_a = a_w_in.shape[0]
    assert seq % (SLC_LEN * LANES) == 0 and seq % ML_CHUNK == 0, "sequence must tile the selection-block lanes"
    cos2, sin2 = _rope_tables(seq)
    cos2t, sin2t = cos2.T, sin2.T
    w_gate16, w_up16, w_down16 = ffn_w_gate.astype(BF16), ffn_w_up.astype(BF16), ffn_w_down.astype(BF16)
    w_out16 = w_out.astype(BF16)

    def ffn(xb, layer, half, g_pre, g_post):
        return half_ffn(xb, g_pre, g_post, w_gate16, w_up16, w_down16, layer, half, tm=_row_tile(seq, FFN_ROWS),
                        tf=FFN_HIDDEN)

    outs = []
    for b in range(batch):
        xb = x[b]
        shared = None
        for layer in range(depth):
            if layer == n_a:
                shared = _shared_kv(xb, kv_norm, w_kv, cmp_pos, cmp_w1, cmp_w2, cos2, sin2)
            g = norm_gains[layer]
            xb = ffn(xb, layer, 0, g[0], g[1])
            mem_kt, mem_v = _mem_kv(mem[b], mem_norm[layer], mem_w_kv[layer])
            if layer < n_a:
                main, proj, mq_block = _mlstm_layer_heads(xb, g[2], a_w_in[layer], a_gate_bias[layer],
                                                          a_head_norm[layer])
            else:
                lb = layer - n_a
                main, proj, mq_block = _nsa_layer_heads(xb, g[2], b_w_in[lb], b_gate_bias[lb], shared, cos2t, sin2t)
            xb = out_proj(main, proj, mq_block, mem_kt, mem_v, xb, w_out16, layer, g[3],
                          tm=_row_tile(seq, OUT_PROJ_ROWS))
            xb = ffn(xb, layer, 1, g[4], g[5])
        outs.append(xb)
    return jnp.stack(outs, axis=0)
```
